```python
import jax, jax.numpy as jnp
from jax import lax
import numpy as np

D_MODEL = 1024
BATCH = 1
SEQ = 16384
DEPTH = 1
DEC_BATCH = 8
DEC_SEQ = 32
PAST_LEN = 1024

CHUNK = 64
MLA_HEADS = 4
QK_NOPE = 128
QK_ROPE = 64
V_HEAD = 128
Q_LORA = 256
KV_LORA = 128
ROPE_THETA = 10000.0
MLA_WIDTH = MLA_HEADS * V_HEAD
Q_BLOCK = 128
RWKV_HEAD = 64
RWKV_HEADS = 8
RWKV_WIDTH = RWKV_HEADS * RWKV_HEAD
DECAY_LORA = 64
ICLR_LORA = 64
GATE_LORA = 128
LN_X_EPS = 64e-5
MLA_COLS = Q_LORA + KV_LORA + QK_ROPE
RWKV_COLS = 3 * RWKV_WIDTH + DECAY_LORA + ICLR_LORA + GATE_LORA
IN_COLS = MLA_COLS + RWKV_COLS
MIX_WIDTH = MLA_WIDTH + RWKV_WIDTH
N_EXPERTS = 32
TOP_K = 4
D_FF = 1024
SWIGLU_ALPHA = 1.702
SWIGLU_LIMIT = 7.0
MOE_BLOCK = 128
NORM_EPS = 1e-6

kernel_name = 'hybrid_mla_rwkv7_moe_stream_step'


def rms_norm(x, g, eps=NORM_EPS):
    xf = x.astype(jnp.float32)
    y = xf * lax.rsqrt(jnp.mean(xf * xf, axis=-1, keepdims=True) + eps)
    return (y * g.astype(jnp.float32)).astype(x.dtype)


def rope_tables(pos, dtype):
    inv = ROPE_THETA ** (-jnp.arange(0, QK_ROPE, 2, dtype=jnp.float32) / QK_ROPE)
    ang = pos.astype(jnp.float32)[:, None] * inv[None, :]
    return jnp.cos(ang).astype(dtype), jnp.sin(ang).astype(dtype)


def apply_rope(x, cos, sin):
    x1, x2 = jnp.split(x, 2, axis=-1)
    return jnp.concatenate([x1 * cos - x2 * sin, x1 * sin + x2 * cos], axis=-1)


def mla_attention(q_nope, q_rope, k_nope, k_rope, v, q_pos, k_pos):
    scale = (QK_NOPE + QK_ROPE) ** -0.5

    def block(args):
        qn, qr, qp = args
        s = (jnp.einsum('bqhd,bkhd->bhqk', qn, k_nope, preferred_element_type=jnp.float32)
             + jnp.einsum('bqhd,bkd->bhqk', qr, k_rope, preferred_element_type=jnp.float32)) * scale
        limit = (qp // CHUNK + 1) * CHUNK
        mask = k_pos[None, :] < limit[:, None]
        s = jnp.where(mask[None, None], s, -jnp.inf)
        p = jax.nn.softmax(s, axis=-1).astype(v.dtype)
        return jnp.einsum('bhqk,bkhd->bqhd', p, v)

    B, Sq = q_nope.shape[0], q_nope.shape[1]
    qb = Q_BLOCK if Sq % Q_BLOCK == 0 else Sq
    nb = Sq // qb
    if nb == 1:
        return block((q_nope, q_rope, q_pos))

    def to_blocks(t):
        return jnp.moveaxis(t.reshape((B, nb, qb) + t.shape[2:]), 1, 0)

    out = lax.map(block, (to_blocks(q_nope), to_blocks(q_rope), q_pos.reshape(nb, qb)))
    return jnp.moveaxis(out, 0, 1).reshape((B, Sq) + out.shape[3:])


def mla_mixer(p_mla, pos, cache_lat, cache_rope, q_norm, w_uq, kv_norm, w_ukv, mla_out_norm):
    B, S, _ = p_mla.shape
    q_lat, kv_lat, kr_raw = jnp.split(p_mla, [Q_LORA, Q_LORA + KV_LORA], axis=-1)
    q = (rms_norm(q_lat, q_norm) @ w_uq).reshape(B, S, MLA_HEADS, QK_NOPE + QK_ROPE)
    q_nope, q_rope = q[..., :QK_NOPE], q[..., QK_NOPE:]
    cos, sin = rope_tables(pos, p_mla.dtype)
    q_rope = apply_rope(q_rope, cos[:, None, :], sin[:, None, :])
    c_kv = rms_norm(kv_lat, kv_norm)
    k_rope = apply_rope(kr_raw, cos, sin)
    if cache_lat is None:
        all_lat, all_rope, k_pos = c_kv, k_rope, pos
    else:
        past = cache_lat.shape[1]
        all_lat = jnp.concatenate([cache_lat.astype(c_kv.dtype), c_kv], axis=1)
        all_rope = jnp.concatenate([cache_rope.astype(k_rope.dtype), k_rope], axis=1)
        k_pos = jnp.concatenate([jnp.arange(past, dtype=jnp.int32), pos])
    Sk = all_lat.shape[1]
    kv = (all_lat @ w_ukv).reshape(B, Sk, MLA_HEADS, QK_NOPE + V_HEAD)
    k_nope, v = kv[..., :QK_NOPE], kv[..., QK_NOPE:]
    o = mla_attention(q_nope, q_rope, k_nope, all_rope, v, pos, k_pos)
    o = rms_norm(o.reshape(B, S, MLA_WIDTH), mla_out_norm)
    return o, c_kv, k_rope


def rwkv7_mixer(p_rw, shift_prev, wkv_prev, shift_mu, decay_w0, decay_w2, iclr_a0, iclr_a2, gate_g2,
                k_k, k_a, r_k, lnx_g, lnx_b):
    B, S, _ = p_rw.shape
    f32 = jnp.float32
    prev = jnp.concatenate([shift_prev.astype(p_rw.dtype)[:, None, :], p_rw[:, :-1]], axis=1)
    xs = p_rw + (prev - p_rw) * shift_mu
    W = RWKV_WIDTH
    r, k, v, wd, ad, gd = jnp.split(xs, [W, 2 * W, 3 * W, 3 * W + DECAY_LORA, 3 * W + DECAY_LORA + ICLR_LORA], axis=-1)
    w_log = -jax.nn.softplus(-(decay_w0 + jnp.tanh(wd) @ decay_w2)) - 0.5
    decay = jnp.exp(-jnp.exp(w_log.astype(f32)))
    a = jax.nn.sigmoid(iclr_a0 + ad @ iclr_a2)
    g = jax.nn.sigmoid(gd) @ gate_g2

    def heads(t):
        return t.astype(f32).reshape(B, S, RWKV_HEADS, RWKV_HEAD)

    kk = heads(k * k_k)
    kk = kk * lax.rsqrt(jnp.maximum(jnp.sum(kk * kk, axis=-1, keepdims=True), 1e-24))
    k_h = heads(k * (1 + (a - 1) * k_a))
    r_h, v_h, a_h, w_h = heads(r), heads(v), heads(a), heads(decay)
    a_vec, b_vec = -kk, kk * a_h

    def step(st, inp):
        r_t, w_t, k_t, v_t, a_t, b_t = inp
        sa = jnp.einsum('bhij,bhj->bhi', st, a_t)
        st = st * w_t[:, :, None, :] + sa[..., None] * b_t[:, :, None, :] + v_t[..., None] * k_t[:, :, None, :]
        return st, jnp.einsum('bhij,bhj->bhi', st, r_t)

    seq = tuple(jnp.moveaxis(t, 1, 0) for t in (r_h, w_h, k_h, v_h, a_vec, b_vec))
    st_fin, y = lax.scan(step, wkv_prev.astype(f32), seq)
    y = jnp.moveaxis(y, 0, 1)
    mu = jnp.mean(y, axis=-1, keepdims=True)
    var = jnp.mean(jnp.square(y - mu), axis=-1, keepdims=True)
    yn = ((y - mu) * lax.rsqrt(var + LN_X_EPS)).reshape(B, S, W) * lnx_g.astype(f32) + lnx_b.astype(f32)
    bonus = (jnp.sum(r_h * k_h * r_k.astype(f32), axis=-1, keepdims=True) * v_h).reshape(B, S, W)
    out = ((yn + bonus) * g.astype(f32)).astype(p_rw.dtype)
    return out, st_fin.astype(p_rw.dtype), p_rw[:, -1]


def moe_ffn(h, w_router, b_router, w_gate_up, b_gate_up, w_down, b_down):
    B, S, D = h.shape
    xt = h.reshape(-1, D)
    T = xt.shape[0]
    f32 = jnp.float32
    logits = jnp.dot(xt, w_router, preferred_element_type=f32) + b_router.astype(f32)
    top_val, top_idx = lax.top_k(logits, TOP_K)
    gates = jax.nn.softmax(top_val, axis=-1)
    n_pairs = T * TOP_K
    flat_e = top_idx.reshape(-1).astype(jnp.int32)
    flat_tok = jnp.arange(n_pairs, dtype=jnp.int32) // TOP_K
    flat_gate = gates.reshape(-1)
    order = jnp.argsort(flat_e)
    se = flat_e[order]
    counts = jnp.zeros((N_EXPERTS,), jnp.int32).at[flat_e].add(1)
    padded = (counts + MOE_BLOCK - 1) // MOE_BLOCK * MOE_BLOCK
    pad_end = jnp.cumsum(padded)
    pad_start = pad_end - padded
    start = jnp.cumsum(counts) - counts
    dest = pad_start[se] + jnp.arange(n_pairs, dtype=jnp.int32) - start[se]
    n_blocks = (n_pairs + N_EXPERTS * (MOE_BLOCK - 1)) // MOE_BLOCK + 1
    n_rows = n_blocks * MOE_BLOCK
    row_tok = jnp.full((n_rows,), T, jnp.int32).at[dest].set(flat_tok[order])
    row_gate = jnp.zeros((n_rows,), f32).at[dest].set(flat_gate[order])
    block_e = jnp.minimum(jnp.searchsorted(pad_end, jnp.arange(n_blocks, dtype=jnp.int32) * MOE_BLOCK, side='right'),
                          N_EXPERTS - 1)
    x_pad = jnp.concatenate([xt, jnp.zeros((1, D), xt.dtype)], axis=0)
    xb = x_pad[row_tok].reshape(n_blocks, MOE_BLOCK, D)

    def expert_block(args):
        xblk, e = args
        gu = xblk @ w_gate_up[e] + b_gate_up[e]
        gate, up = gu[:, :D_FF], gu[:, D_FF:]
        gate = jnp.minimum(gate, SWIGLU_LIMIT)
        up = jnp.clip(up, -SWIGLU_LIMIT, SWIGLU_LIMIT)
        act = (up + 1) * (gate * jax.nn.sigmoid(SWIGLU_ALPHA * gate))
        return act @ w_down[e] + b_down[e]

    yb = lax.map(expert_block, (xb, block_e)).reshape(n_rows, D)
    y = jax.ops.segment_sum(yb.astype(f32) * row_gate[:, None], row_tok, num_segments=T + 1)[:T]
    return y.astype(h.dtype).reshape(B, S, D)


def trunk_layer(x, c, pos, cache_lat, cache_rope, wkv_prev, shift_prev,
                w_ada, b_ada, norm_mix, w_in, q_norm, w_uq, kv_norm, w_ukv, mla_out_norm,
                shift_mu, decay_w0, decay_w2, iclr_a0, iclr_a2, gate_g2, k_k, k_a, r_k, lnx_g, lnx_b,
                w_out, norm_ffn, w_router, b_router, w_gate_up, b_gate_up, w_down, b_down):
    mod = (jax.nn.silu(c) @ w_ada + b_ada)[:, None, :]
    sh1, sc1, g1, sh2, sc2, g2 = jnp.split(mod, 6, axis=-1)
    h = rms_norm(x, norm_mix) * (1 + sc1) + sh1
    p = h @ w_in
    o_mla, new_lat, new_rope = mla_mixer(p[..., :MLA_COLS], pos, cache_lat, cache_rope,
                                         q_norm, w_uq, kv_norm, w_ukv, mla_out_norm)
    o_rw, new_wkv, new_shift = rwkv7_mixer(p[..., MLA_COLS:], shift_prev, wkv_prev, shift_mu, decay_w0, decay_w2,
                                           iclr_a0, iclr_a2, gate_g2, k_k, k_a, r_k, lnx_g, lnx_b)
    x = x + g1 * (jnp.concatenate([o_mla, o_rw], axis=-1) @ w_out)
    h = rms_norm(x, norm_ffn) * (1 + sc2) + sh2
    x = x + g2 * moe_ffn(h, w_router, b_router, w_gate_up, b_gate_up, w_down, b_down)
    return x, new_lat, new_rope, new_wkv, new_shift


def setup_inputs(seed: int = 0) -> dict:
    key = jax.random.key(seed)
    ks = iter(jax.random.split(key, 64))
    L = DEPTH

    def nrm(shape, scale):
        return scale * jax.random.normal(next(ks), shape, jnp.float32)

    def gain(shape):
        return 1.0 + nrm(shape, 0.02)

    def unif(shape, lo, hi):
        return jax.random.uniform(next(ks), shape, jnp.float32, lo, hi)

    return {
        'x_prompt': nrm((BATCH, SEQ, D_MODEL), 1.0),
        'x_sample': nrm((DEC_BATCH, DEC_SEQ, D_MODEL), 1.0),
        'cache_kv_latent': nrm((L, DEC_BATCH, PAST_LEN, KV_LORA), 1.0),
        'cache_k_rope': nrm((L, DEC_BATCH, PAST_LEN, QK_ROPE), 1.0),
        'state_wkv': nrm((L, DEC_BATCH, RWKV_HEADS, RWKV_HEAD, RWKV_HEAD), 0.5),
        'state_shift': nrm((L, DEC_BATCH, RWKV_COLS), 1.0),
        'c_prompt': nrm((BATCH, D_MODEL), 1.0),
        'c_sample': nrm((DEC_BATCH, D_MODEL), 1.0),
        'w_ada': nrm((L, D_MODEL, 6 * D_MODEL), D_MODEL ** -0.5),
        'b_ada': nrm((L, 6 * D_MODEL), 0.02),
        'norm_mix': gain((L, D_MODEL)),
        'w_in': nrm((L, D_MODEL, IN_COLS), D_MODEL ** -0.5),
        'q_norm': gain((L, Q_LORA)),
        'w_uq': nrm((L, Q_LORA, MLA_HEADS * (QK_NOPE + QK_ROPE)), Q_LORA ** -0.5),
        'kv_norm': gain((L, KV_LORA)),
        'w_ukv': nrm((L, KV_LORA, MLA_HEADS * (QK_NOPE + V_HEAD)), KV_LORA ** -0.5),
        'mla_out_norm': gain((L, MLA_WIDTH)),
        'shift_mu': unif((L, RWKV_COLS), 0.0, 1.0),
        'decay_w0': unif((L, RWKV_WIDTH), -2.0, 1.0),
        'decay_w2': nrm((L, DECAY_LORA, RWKV_WIDTH), 0.5 * DECAY_LORA ** -0.5),
        'iclr_a0': nrm((L, RWKV_WIDTH), 0.1),
        'iclr_a2': nrm((L, ICLR_LORA, RWKV_WIDTH), 0.5 * ICLR_LORA ** -0.5),
        'gate_g2': nrm((L, GATE_LORA, RWKV_WIDTH), GATE_LORA ** -0.5),
        'k_k': 0.85 + nrm((L, RWKV_WIDTH), 0.05),
        'k_a': 1.0 + nrm((L, RWKV_WIDTH), 0.05),
        'r_k': nrm((L, RWKV_HEADS, RWKV_HEAD), 0.1),
        'lnx_g': gain((L, RWKV_WIDTH)),
        'lnx_b': nrm((L, RWKV_WIDTH), 0.02),
        'w_out': nrm((L, MIX_WIDTH, D_MODEL), MIX_WIDTH ** -0.5),
        'norm_ffn': gain((L, D_MODEL)),
        'w_router': nrm((L, D_MODEL, N_EXPERTS), D_MODEL ** -0.5),
        'b_router': nrm((L, N_EXPERTS), 0.01),
        'w_gate_up': nrm((L, N_EXPERTS, D_MODEL, 2 * D_FF), D_MODEL ** -0.5),
        'b_gate_up': nrm((L, N_EXPERTS, 2 * D_FF), 0.01),
        'w_down': nrm((L, N_EXPERTS, D_FF, D_MODEL), D_FF ** -0.5),
        'b_down': nrm((L, N_EXPERTS, D_MODEL), 0.01),
        'norm_final': gain((D_MODEL,)),
    }


def reference(x_prompt, x_sample, cache_kv_latent, cache_k_rope, state_wkv, state_shift, c_prompt, c_sample,
              w_ada, b_ada, norm_mix, w_in, q_norm, w_uq, kv_norm, w_ukv, mla_out_norm,
              shift_mu, decay_w0, decay_w2, iclr_a0, iclr_a2, gate_g2, k_k, k_a, r_k, lnx_g, lnx_b,
              w_out, norm_ffn, w_router, b_router, w_gate_up, b_gate_up, w_down, b_down, norm_final):
    bp, seq_p = x_prompt.shape[0], x_prompt.shape[1]
    past, seq_s = cache_kv_latent.shape[2], x_sample.shape[1]
    pos_p = jnp.arange(seq_p, dtype=jnp.int32)
    pos_s = past + jnp.arange(seq_s, dtype=jnp.int32)
    zero_wkv = jnp.zeros((bp, RWKV_HEADS, RWKV_HEAD, RWKV_HEAD), x_prompt.dtype)
    zero_shift = jnp.zeros((bp, RWKV_COLS), x_prompt.dtype)
    hp, hs = x_prompt, x_sample
    lat_p, rope_p, wkv_p, sh_p = [], [], [], []
    lat_s, rope_s, wkv_s, sh_s = [], [], [], []
    for l in range(DEPTH):
        lw = (w_ada[l], b_ada[l], norm_mix[l], w_in[l], q_norm[l], w_uq[l], kv_norm[l], w_ukv[l], mla_out_norm[l],
              shift_mu[l], decay_w0[l], decay_w2[l], iclr_a0[l], iclr_a2[l], gate_g2[l], k_k[l], k_a[l], r_k[l],
              lnx_g[l], lnx_b[l], w_out[l], norm_ffn[l], w_router[l], b_router[l], w_gate_up[l], b_gate_up[l],
              w_down[l], b_down[l])
        hp, a1, a2, a3, a4 = trunk_layer(hp, c_prompt, pos_p, None, None, zero_wkv, zero_shift, *lw)
        lat_p.append(a1); rope_p.append(a2); wkv_p.append(a3); sh_p.append(a4)
        hs, b1, b2, b3, b4 = trunk_layer(hs, c_sample, pos_s, cache_kv_latent[l], cache_k_rope[l],
                                         state_wkv[l], state_shift[l], *lw)
        lat_s.append(b1); rope_s.append(b2); wkv_s.append(b3); sh_s.append(b4)
    y_prompt = rms_norm(hp, norm_final)
    y_sample = rms_norm(hs, norm_final)
    return (y_prompt, y_sample, jnp.stack(lat_p), jnp.stack(rope_p), jnp.stack(wkv_p), jnp.stack(sh_p),
            jnp.stack(lat_s), jnp.stack(rope_s), jnp.stack(wkv_s), jnp.stack(sh_s))
```

```python
import functools

import jax
import jax.numpy as jnp
from jax import lax
from jax.experimental import pallas as pl
from jax.experimental.pallas import tpu as pltpu

f32 = jnp.float32
bf16 = jnp.bfloat16
i32 = jnp.int32

D_MODEL = 1024
CHUNK = 64
MLA_HEADS = 4
QK_NOPE = 128
QK_ROPE = 64
V_HEAD = 128
Q_LORA = 256
KV_LORA = 128
ROPE_THETA = 10000.0
MLA_WIDTH = MLA_HEADS * V_HEAD
RWKV_HEAD = 64
RWKV_HEADS = 8
RWKV_WIDTH = RWKV_HEADS * RWKV_HEAD
DECAY_LORA = 64
ICLR_LORA = 64
GATE_LORA = 128
LN_X_EPS = 64e-5
MLA_COLS = Q_LORA + KV_LORA + QK_ROPE
RWKV_COLS = 3 * RWKV_WIDTH + DECAY_LORA + ICLR_LORA + GATE_LORA
N_EXPERTS = 32
TOP_K = 4
D_FF = 1024
SWIGLU_ALPHA = 1.702
SWIGLU_LIMIT = 7.0
NORM_EPS = 1e-6

MLA_PAD = 512
QK_PAD = 256
N_PAIRS = RWKV_HEADS // 2
PAIR_W = 2 * RWKV_HEAD
ROUTE_W = 128

TOKEN_TILE = 256
ATTN_TILE = 512
SCAN_CHUNK = 64
EXPERT_TILE = 512
VMEM_LIMIT = 56 * 1024 * 1024


def _token_tile(s):
    t = TOKEN_TILE
    while s % t:
        t //= 2
    assert t >= 8, s
    return t


def _cparams(sem, vmem=None):
    return pltpu.CompilerParams(dimension_semantics=sem, vmem_limit_bytes=vmem or VMEM_LIMIT)


def _dot(a, b):
    return jnp.dot(a.astype(bf16), b.astype(bf16), preferred_element_type=f32)


def _dg(a, b, ca, cb):
    return lax.dot_general(a, b, (((ca,), (cb,)), ((), ())), preferred_element_type=f32)


def _split2(x):
    hi = x.astype(bf16)
    lo = (x - hi.astype(f32)).astype(bf16)
    return hi, lo


def _split3(x):
    hi = x.astype(bf16)
    r1 = x - hi.astype(f32)
    mid = r1.astype(bf16)
    lo = (r1 - mid.astype(f32)).astype(bf16)
    return hi, mid, lo


def _dot3(a, b, ca=1, cb=0):
    ah, al = _split2(a)
    bh, bl = _split2(b)
    return _dg(ah, bh, ca, cb) + (_dg(ah, bl, ca, cb) + _dg(al, bh, ca, cb))


def _dot_exact_lhs(a_bf, b):
    bh, bm, bl = _split3(b)
    return _dg(a_bf, bh, 1, 0) + (_dg(a_bf, bm, 1, 0) + _dg(a_bf, bl, 1, 0))


def _seg_sum(x, seg_bf):
    xh, xm, xl = _split3(x)
    return _dg(xh, seg_bf, 1, 0) + (_dg(xm, seg_bf, 1, 0) + _dg(xl, seg_bf, 1, 0))


def _rms(x, g):
    return x * lax.rsqrt(jnp.mean(x * x, axis=-1, keepdims=True) + NORM_EPS) * g


def _sigmoid(x):
    return 1.0 / (1.0 + jnp.exp(-x))


def _mod_kernel(c_ref, w_ref, b_ref, o_ref):
    c = c_ref[...]
    o_ref[...] = _dot3(c * _sigmoid(c), w_ref[...]) + b_ref[...]


def _modulation(c_all, w_ada, b_ada):
    rows, d = c_all.shape
    n = w_ada.shape[1]
    tn = 1536
    return pl.pallas_call(
        _mod_kernel,
        name="adaln_mod",
        out_shape=jax.ShapeDtypeStruct((rows, n), f32),
        grid=(n // tn,),
        in_specs=[pl.BlockSpec((rows, d), lambda j: (0, 0)),
                  pl.BlockSpec((d, tn), lambda j: (0, j)),
                  pl.BlockSpec((1, tn), lambda j: (0, j))],
        out_specs=pl.BlockSpec((rows, tn), lambda j: (0, j)),
        compiler_params=_cparams(("arbitrary",)),
    )(c_all, w_ada, b_ada.reshape(1, n))


def _inproj_kernel(x_ref, mod_ref, nw_ref, w_ref, pm_ref, prw_ref):
    h = _rms(x_ref[0], nw_ref[...])
    h = h * (1.0 + mod_ref[0, 1:2, :]) + mod_ref[0, 0:1, :]
    p = jnp.dot(h.astype(bf16), w_ref[...], preferred_element_type=f32)
    pm_ref[0] = p[:, :MLA_PAD]
    prw_ref[0] = p[:, MLA_PAD:]


def _inproj(x, mod, norm_w, w_in_bf):
    b, s, d = x.shape
    ts = _token_tile(s)
    n = w_in_bf.shape[1]
    return pl.pallas_call(
        _inproj_kernel,
        name="inproj",
        out_shape=(jax.ShapeDtypeStruct((b, s, MLA_PAD), f32),
                   jax.ShapeDtypeStruct((b, s, RWKV_COLS), f32)),
        grid=(b, s // ts),
        in_specs=[pl.BlockSpec((1, ts, d), lambda i, j: (i, j, 0)),
                  pl.BlockSpec((1, 6, d), lambda i, j: (i, 0, 0)),
                  pl.BlockSpec((1, d), lambda i, j: (0, 0)),
                  pl.BlockSpec((d, n), lambda i, j: (0, 0))],
        out_specs=(pl.BlockSpec((1, ts, MLA_PAD), lambda i, j: (i, j, 0)),
                   pl.BlockSpec((1, ts, RWKV_COLS), lambda i, j: (i, j, 0))),
        compiler_params=_cparams(("arbitrary", "arbitrary")),
    )(x, mod, norm_w.reshape(1, d), w_in_bf)


def _mla_prep_kernel(pm_ref, qn_ref, wa_ref, wb_ref, kvn_ref, ct_ref, st_ref, kt_ref,
                     q_ref, lat_ref, rope_ref, rope128_ref):
    pm = pm_ref[0]
    qn = _rms(pm[:, :Q_LORA], qn_ref[...]).astype(bf16)
    qa = jnp.dot(qn, wa_ref[...], preferred_element_type=f32)
    qb = jnp.dot(qn, wb_ref[...], preferred_element_type=f32)
    ct = ct_ref[...]
    st = st_ref[...]
    for h in range(MLA_HEADS):
        sl = slice(h * QK_PAD, (h + 1) * QK_PAD)
        q_ref[0, h] = (qa[:, sl] * ct + qb[:, sl] * st).astype(bf16)
    lat_ref[0] = _rms(pm[:, Q_LORA:Q_LORA + KV_LORA], kvn_ref[...])
    t = pm[:, Q_LORA + KV_LORA:] * kt_ref[...]
    kr = t + pltpu.roll(t, QK_ROPE, axis=1)
    lane = lax.broadcasted_iota(i32, kr.shape, 1)
    rope128_ref[0] = jnp.where(lane < QK_ROPE, kr, 0.0)
    rope_ref[0] = kr[:, :QK_ROPE]


def _mla_prep(pm, q_norm, w_qa, w_qb, kv_norm, ct, st, kt):
    b, s, _ = pm.shape
    ts = _token_tile(s)
    hq = MLA_HEADS * QK_PAD
    return pl.pallas_call(
        _mla_prep_kernel,
        name="mla_prep",
        out_shape=(jax.ShapeDtypeStruct((b, MLA_HEADS, s, QK_PAD), bf16),
                   jax.ShapeDtypeStruct((b, s, KV_LORA), f32),
                   jax.ShapeDtypeStruct((b, s, QK_ROPE), f32),
                   jax.ShapeDtypeStruct((b, s, 2 * QK_ROPE), f32)),
        grid=(b, s // ts),
        in_specs=[pl.BlockSpec((1, ts, MLA_PAD), lambda i, j: (i, j, 0)),
                  pl.BlockSpec((1, Q_LORA), lambda i, j: (0, 0)),
                  pl.BlockSpec((Q_LORA, hq), lambda i, j: (0, 0)),
                  pl.BlockSpec((Q_LORA, hq), lambda i, j: (0, 0)),
                  pl.BlockSpec((1, KV_LORA), lambda i, j: (0, 0)),
                  pl.BlockSpec((ts, QK_PAD), lambda i, j: (j, 0)),
                  pl.BlockSpec((ts, QK_PAD), lambda i, j: (j, 0)),
                  pl.BlockSpec((ts, 2 * QK_ROPE), lambda i, j: (j, 0))],
        out_specs=(pl.BlockSpec((1, MLA_HEADS, ts, QK_PAD), lambda i, j: (i, 0, j, 0)),
                   pl.BlockSpec((1, ts, KV_LORA), lambda i, j: (i, j, 0)),
                   pl.BlockSpec((1, ts, QK_ROPE), lambda i, j: (i, j, 0)),
                   pl.BlockSpec((1, ts, 2 * QK_ROPE), lambda i, j: (i, j, 0))),
        compiler_params=_cparams(("arbitrary", "arbitrary")),
    )(pm, q_norm.reshape(1, -1), w_qa, w_qb, kv_norm.reshape(1, -1), ct, st, kt)


def _kv_proj_kernel(lat_ref, rope_ref, wk_ref, wv_ref, k_ref, v_ref):
    lat = lat_ref[0].astype(bf16)
    kn = jnp.dot(lat, wk_ref[...], preferred_element_type=f32)
    vv = jnp.dot(lat, wv_ref[...], preferred_element_type=f32)
    rope = rope_ref[0].astype(bf16)
    for h in range(MLA_HEADS):
        k_ref[0, h, :, :QK_NOPE] = kn[:, h * QK_NOPE:(h + 1) * QK_NOPE].astype(bf16)
        k_ref[0, h, :, QK_NOPE:] = rope
        v_ref[0, h] = vv[:, h * V_HEAD:(h + 1) * V_HEAD].astype(bf16)


def _kv_proj(lat, rope128, w_kn, w_v):
    b, s, _ = lat.shape
    ts = _token_tile(s)
    return pl.pallas_call(
        _kv_proj_kernel,
        name="kv_proj",
        out_shape=(jax.ShapeDtypeStruct((b, MLA_HEADS, s, QK_PAD), bf16),
                   jax.ShapeDtypeStruct((b, MLA_HEADS, s, V_HEAD), bf16)),
        grid=(b, s // ts),
        in_specs=[pl.BlockSpec((1, ts, KV_LORA), lambda i, j: (i, j, 0)),
                  pl.BlockSpec((1, ts, 2 * QK_ROPE), lambda i, j: (i, j, 0)),
                  pl.BlockSpec((KV_LORA, MLA_HEADS * QK_NOPE), lambda i, j: (0, 0)),
                  pl.BlockSpec((KV_LORA, MLA_HEADS * V_HEAD), lambda i, j: (0, 0))],
        out_specs=(pl.BlockSpec((1, MLA_HEADS, ts, QK_PAD), lambda i, j: (i, 0, j, 0)),
                   pl.BlockSpec((1, MLA_HEADS, ts, V_HEAD), lambda i, j: (i, 0, j, 0))),
        compiler_params=_cparams(("arbitrary", "arbitrary")),
    )(lat, rope128, w_kn, w_v)


def _flash_kernel(qi_ref, kj_ref, q_ref, k_ref, v_ref, o_ref, m_scr, l_scr, acc_scr, *, tile):
    step = pl.program_id(1)
    qi = qi_ref[step]
    kj = kj_ref[step]

    @pl.when(kj == 0)
    def _():
        m_scr[...] = jnp.full(m_scr.shape, -jnp.inf, f32)
        l_scr[...] = jnp.zeros(l_scr.shape, f32)
        acc_scr[...] = jnp.zeros(acc_scr.shape, f32)

    def update(masked):
        s = _dg(q_ref[0, 0], k_ref[0, 0], 1, 1)
        if masked:
            qpos = lax.broadcasted_iota(i32, s.shape, 0)
            kpos = lax.broadcasted_iota(i32, s.shape, 1)
            limit = (qpos // CHUNK + 1) * CHUNK
            s = jnp.where(kpos < limit, s, -jnp.inf)
        m_prev = m_scr[...]
        m_new = jnp.maximum(m_prev, jnp.max(s, axis=1, keepdims=True))
        alpha = jnp.exp(m_prev - m_new)
        p = jnp.exp(s - m_new)
        l_scr[...] = alpha * l_scr[...] + jnp.sum(p, axis=1, keepdims=True)
        acc_scr[...] = alpha * acc_scr[...] + jnp.dot(p.astype(bf16), v_ref[0, 0],
                                                      preferred_element_type=f32)
        m_scr[...] = m_new

    @pl.when(kj < qi)
    def _():
        update(False)

    @pl.when(kj == qi)
    def _():
        update(True)
        o_ref[0] = acc_scr[...] / l_scr[...]


def _flash_attention(q, k, v):
    b, nh, s, _ = q.shape
    assert b == 1
    tile = min(ATTN_TILE, s)
    nt = s // tile
    qi = [i for i in range(nt) for _ in range(i + 1)]
    kj = [j for i in range(nt) for j in range(i + 1)]
    grid_spec = pltpu.PrefetchScalarGridSpec(
        num_scalar_prefetch=2,
        grid=(nh, len(qi)),
        in_specs=[pl.BlockSpec((1, 1, tile, QK_PAD), lambda h, t, qi, kj: (0, h, qi[t], 0)),
                  pl.BlockSpec((1, 1, tile, QK_PAD), lambda h, t, qi, kj: (0, h, kj[t], 0)),
                  pl.BlockSpec((1, 1, tile, V_HEAD), lambda h, t, qi, kj: (0, h, kj[t], 0))],
        out_specs=pl.BlockSpec((1, tile, V_HEAD), lambda h, t, qi, kj: (0, qi[t], h)),
        scratch_shapes=[pltpu.VMEM((tile, 1), f32), pltpu.VMEM((tile, 1), f32),
                        pltpu.VMEM((tile, V_HEAD), f32)])
    return pl.pallas_call(
        functools.partial(_flash_kernel, tile=tile),
        name="flash_attn",
        out_shape=jax.ShapeDtypeStruct((1, s, MLA_WIDTH), f32),
        grid_spec=grid_spec,
        compiler_params=_cparams(("arbitrary", "arbitrary")),
    )(jnp.asarray(qi, i32), jnp.asarray(kj, i32), q, k, v)


def _cached_attn_kernel(q_ref, k_ref, v_ref, o_ref, *, past, n_keys):
    for h in range(MLA_HEADS):
        s = _dg(q_ref[0, h], k_ref[0, h], 1, 1)
        qpos = past + lax.broadcasted_iota(i32, s.shape, 0)
        kpos = lax.broadcasted_iota(i32, s.shape, 1)
        limit = (qpos // CHUNK + 1) * CHUNK
        s = jnp.where((kpos < limit) & (kpos < n_keys), s, -jnp.inf)
        m = jnp.max(s, axis=1, keepdims=True)
        p = jnp.exp(s - m)
        o = jnp.dot(p.astype(bf16), v_ref[0, h], preferred_element_type=f32)
        o_ref[0, :, h * V_HEAD:(h + 1) * V_HEAD] = o / jnp.sum(p, axis=1, keepdims=True)


def _cached_attention(q, k, v, past, n_keys):
    b, nh, sq, _ = q.shape
    sk = k.shape[2]
    return pl.pallas_call(
        functools.partial(_cached_attn_kernel, past=past, n_keys=n_keys),
        name="cached_attn",
        out_shape=jax.ShapeDtypeStruct((b, sq, MLA_WIDTH), f32),
        grid=(b,),
        in_specs=[pl.BlockSpec((1, nh, sq, QK_PAD), lambda i: (i, 0, 0, 0)),
                  pl.BlockSpec((1, nh, sk, QK_PAD), lambda i: (i, 0, 0, 0)),
                  pl.BlockSpec((1, nh, sk, V_HEAD), lambda i: (i, 0, 0, 0))],
        out_specs=pl.BlockSpec((1, sq, MLA_WIDTH), lambda i: (i, 0, 0)),
        compiler_params=_cparams(("arbitrary",)),
    )(q, k, v)


def _rwkv_prep_kernel(p_ref, sp_ref, mu_ref, w0_ref, w2_ref, a0_ref, a2_ref, g2_ref,
                      kk_ref, ka_ref, rk_ref, seg_ref,
                      r_ref, lw_ref, kh_ref, v_ref, al_ref, be_ref, g_ref, bo_ref, ns_ref,
                      carry):
    @pl.when(pl.program_id(1) == 0)
    def _():
        carry[...] = sp_ref[0]

    p = p_ref[0]
    ts = p.shape[0]
    row = lax.broadcasted_iota(i32, p.shape, 0)
    prev = jnp.where(row == 0, carry[...], pltpu.roll(p, 1, axis=0))
    last = p[ts - 1:ts, :]
    carry[...] = last
    ns_ref[0] = last
    xs = p + (prev - p) * mu_ref[...]
    w = RWKV_WIDTH
    r, k, v = xs[:, :w], xs[:, w:2 * w], xs[:, 2 * w:3 * w]
    wa = xs[:, 3 * w:3 * w + DECAY_LORA + ICLR_LORA]
    gd = xs[:, 3 * w + DECAY_LORA + ICLR_LORA:]
    z = -(w0_ref[...] + _dot3(jnp.tanh(wa), w2_ref[...]))
    softplus = jnp.maximum(z, 0.0) + jnp.log(1.0 + jnp.exp(-jnp.abs(z)))
    lw = -jnp.exp(-softplus - 0.5)
    a = _sigmoid(a0_ref[...] + _dot3(wa, a2_ref[...]))
    g = _dot3(_sigmoid(gd), g2_ref[...])
    seg = seg_ref[...]
    kk = k * kk_ref[...]
    kk = kk * lax.rsqrt(jnp.maximum(_seg_sum(kk * kk, seg), 1e-24))
    kh = k * (1.0 + (a - 1.0) * ka_ref[...])
    r_ref[0] = r
    lw_ref[0] = lw
    kh_ref[0] = kh
    v_ref[0] = v
    al_ref[0] = -kk
    be_ref[0] = kk * a
    g_ref[0] = g
    bo_ref[0] = _seg_sum(r * kh * rk_ref[...], seg) * v


def _rwkv_prep(p_rw, shift_prev, wts):
    b, s, n = p_rw.shape
    ts = _token_tile(s)
    w = RWKV_WIDTH
    row = lambda width: pl.BlockSpec((1, width), lambda i, j: (0, 0))
    mat = lambda k: pl.BlockSpec((k, w), lambda i, j: (0, 0))
    tok = pl.BlockSpec((1, ts, w), lambda i, j: (i, j, 0))
    outs = pl.pallas_call(
        _rwkv_prep_kernel,
        name="rwkv_prep",
        out_shape=tuple([jax.ShapeDtypeStruct((b, s, w), f32)] * 8
                        + [jax.ShapeDtypeStruct((b, 1, n), f32)]),
        grid=(b, s // ts),
        in_specs=[pl.BlockSpec((1, ts, n), lambda i, j: (i, j, 0)),
                  pl.BlockSpec((1, 1, n), lambda i, j: (i, 0, 0)),
                  row(n), row(w), mat(PAIR_W), row(w), mat(PAIR_W), mat(GATE_LORA),
                  row(w), row(w), row(w), mat(w)],
        out_specs=tuple([tok] * 8 + [pl.BlockSpec((1, 1, n), lambda i, j: (i, 0, 0))]),
        scratch_shapes=[pltpu.VMEM((1, n), f32)],
        compiler_params=_cparams(("arbitrary", "arbitrary")),
    )(p_rw, shift_prev.reshape(b, 1, n), wts["shift_mu"], wts["decay_w0"], wts["decay_w2p"],
      wts["iclr_a0"], wts["iclr_a2p"], wts["gate_g2"], wts["k_k"], wts["k_a"], wts["r_k"],
      wts["seg"])
    return outs[:8], outs[8].reshape(b, n)


def _unit_lower_inverse(low, eye, n):
    acc = eye + low
    pw = low
    steps = max(n.bit_length() - 2, 0)
    for _ in range(steps):
        pw = _dot3(pw, pw)
        acc = acc + _dot3(acc, pw)
    return acc


def _scan_kernel(r_ref, lw_ref, k_ref, v_ref, al_ref, be_ref, g_ref, bo_ref, s0_ref,
                 lng_ref, lnb_ref, seg_ref, o_ref, sf_ref, st):
    @pl.when(pl.program_id(1) == 0)
    def _():
        st[...] = s0_ref[0]

    c = r_ref.shape[1]
    rowi = lax.broadcasted_iota(i32, (c, c), 0)
    coli = lax.broadcasted_iota(i32, (c, c), 1)
    incl = coli <= rowi
    strict = coli < rowi
    eye = (coli == rowi).astype(f32)

    lw = lw_ref[0]
    cum = _dot_exact_lhs(incl.astype(bf16), lw)
    tot = cum[c - 1:c, :]
    g_fwd = jnp.exp(cum)
    g_inv = jnp.exp(-cum)
    r_t = r_ref[0] * g_fwd
    a_t = al_ref[0] * jnp.exp(cum - lw)
    b_t = be_ref[0] * g_inv
    k_t = k_ref[0] * g_inv
    g_tail = jnp.exp(tot - cum)
    b_c = be_ref[0] * g_tail
    k_c = k_ref[0] * g_tail
    g_tot = jnp.exp(tot)
    v_all = v_ref[0]

    lane = lax.broadcasted_iota(i32, (c, PAIR_W), 1)
    pr = lax.broadcasted_iota(i32, (PAIR_W, PAIR_W), 0) // RWKV_HEAD
    pc = lax.broadcasted_iota(i32, (PAIR_W, PAIR_W), 1) // RWKV_HEAD
    blockdiag = pr == pc

    ys = []
    for p in range(N_PAIRS):
        sl = slice(p * PAIR_W, (p + 1) * PAIR_W)
        s_p = st[p]
        ap, rp, bp, kp, vp = a_t[:, sl], r_t[:, sl], b_t[:, sl], k_t[:, sl], v_all[:, sl]
        u_p = jnp.zeros((c, PAIR_W), f32)
        y_p = jnp.zeros((c, PAIR_W), f32)
        for h in range(2):
            hm = (lane // RWKV_HEAD) == h
            am = jnp.where(hm, ap, 0.0)
            rm = jnp.where(hm, rp, 0.0)
            vm = jnp.where(hm, vp, 0.0)
            a_ab = jnp.where(strict, _dot3(am, bp, 1, 1), 0.0)
            a_ak = jnp.where(strict, _dot3(am, kp, 1, 1), 0.0)
            a_rb = jnp.where(incl, _dot3(rm, bp, 1, 1), 0.0)
            a_rk = jnp.where(incl, _dot3(rm, kp, 1, 1), 0.0)
            t_inv = _unit_lower_inverse(a_ab, eye, c)
            x = _dot3(am, s_p, 1, 1) + _dot3(a_ak, vm)
            u = _dot3(t_inv, x)
            y_p = y_p + (_dot3(rm, s_p, 1, 1) + _dot3(a_rb, u) + _dot3(a_rk, vm))
            u_p = u_p + u
        upd = _dot3(u_p, b_c[:, sl], 0, 0) + _dot3(vp, k_c[:, sl], 0, 0)
        st[p] = s_p * g_tot[:, sl] + jnp.where(blockdiag, upd, 0.0)
        ys.append(y_p)

    y = jnp.concatenate(ys, axis=1)
    seg = seg_ref[...]
    inv_n = 1.0 / RWKV_HEAD
    mu = _seg_sum(y, seg) * inv_n
    yc = y - mu
    var = _seg_sum(yc * yc, seg) * inv_n
    yn = yc * lax.rsqrt(var + LN_X_EPS) * lng_ref[...] + lnb_ref[...]
    o_ref[0] = (yn + bo_ref[0]) * g_ref[0]
    sf_ref[0] = st[...]


def _rwkv_scan(seqs, s0_pairs, lnx_g, lnx_b, seg):
    r, lw, kh, v, al, be, g, bo = seqs
    b, s, w = r.shape
    c = SCAN_CHUNK
    tok = pl.BlockSpec((1, c, w), lambda i, j: (i, j, 0))
    row = pl.BlockSpec((1, w), lambda i, j: (0, 0))
    stt = pl.BlockSpec((1, N_PAIRS, PAIR_W, PAIR_W), lambda i, j: (i, 0, 0, 0))
    return pl.pallas_call(
        _scan_kernel,
        name="rwkv_scan",
        out_shape=(jax.ShapeDtypeStruct((b, s, w), f32),
                   jax.ShapeDtypeStruct((b, N_PAIRS, PAIR_W, PAIR_W), f32)),
        grid=(b, s // c),
        in_specs=[tok] * 8 + [stt, row, row, pl.BlockSpec((w, w), lambda i, j: (0, 0))],
        out_specs=(tok, stt),
        scratch_shapes=[pltpu.VMEM((N_PAIRS, PAIR_W, PAIR_W), f32)],
        compiler_params=_cparams(("arbitrary", "arbitrary")),
    )(r, lw, kh, v, al, be, g, bo, s0_pairs, lnx_g, lnx_b, seg)


def _outproj_kernel(om_ref, orw_ref, x_ref, mod_ref, mn_ref, wo_ref, nf_ref, wr_ref, br_ref,
                    tri_ref, cnt0_ref, x1_ref, h2_ref, gate_ref, idx_ref, rank_ref, cnt_ref,
                    run):
    @pl.when((pl.program_id(0) == 0) & (pl.program_id(1) == 0))
    def _():
        run[...] = cnt0_ref[...]

    om = _rms(om_ref[0], mn_ref[...])
    mix = (jnp.dot(om.astype(bf16), wo_ref[:MLA_WIDTH, :], preferred_element_type=f32)
           + jnp.dot(orw_ref[0].astype(bf16), wo_ref[MLA_WIDTH:, :], preferred_element_type=f32))
    x1 = x_ref[0] + mod_ref[0, 2:3, :] * mix
    x1_ref[0] = x1
    h2 = _rms(x1, nf_ref[...]) * (1.0 + mod_ref[0, 4:5, :]) + mod_ref[0, 3:4, :]
    h2_ref[0] = h2

    logits = _dot3(h2, wr_ref[...]) + br_ref[...]
    ts = logits.shape[0]
    lane_e = lax.broadcasted_iota(i32, logits.shape, 1)
    lane_w = lax.broadcasted_iota(i32, (ts, ROUTE_W), 1)
    tri = tri_ref[...]
    base = run[...]
    gates = jnp.zeros((ts, ROUTE_W), f32)
    idxs = jnp.zeros((ts, ROUTE_W), i32)
    ranks = jnp.zeros((ts, ROUTE_W), i32)
    vals = []
    work = logits
    for k in range(TOP_K):
        m = jnp.max(work, axis=1, keepdims=True)
        sel = jnp.min(jnp.where(work == m, lane_e, N_EXPERTS), axis=1, keepdims=True)
        hit = lane_e == sel
        work = jnp.where(hit, -jnp.inf, work)
        vals.append(m)
        onehot = hit.astype(bf16)
        before = jnp.dot(tri, onehot, preferred_element_type=f32)
        rank = jnp.sum(jnp.where(hit, before + base, 0.0), axis=1, keepdims=True)
        base = base + jnp.sum(hit.astype(f32), axis=0, keepdims=True)
        idxs = jnp.where(lane_w == k, sel, idxs)
        ranks = jnp.where(lane_w == k, rank.astype(i32), ranks)
    run[...] = base
    es = [jnp.exp(vk - vals[0]) for vk in vals]
    inv = 1.0 / (es[0] + es[1] + es[2] + es[3])
    for k in range(TOP_K):
        gates = jnp.where(lane_w == k, es[k] * inv, gates)
    gate_ref[0] = gates
    idx_ref[0] = idxs
    rank_ref[0] = ranks
    cnt_ref[...] = base


def _outproj(o_mla, o_rw, x, mod, wts, counts0):
    b, s, d = x.shape
    ts = _token_tile(s)
    tok = lambda width: pl.BlockSpec((1, ts, width), lambda i, j: (i, j, 0))
    full = lambda r, c: pl.BlockSpec((r, c), lambda i, j: (0, 0))
    tri = (lax.broadcasted_iota(i32, (ts, ts), 1) < lax.broadcasted_iota(i32, (ts, ts), 0)
           ).astype(bf16)
    return pl.pallas_call(
        _outproj_kernel,
        name="outproj_route",
        out_shape=(jax.ShapeDtypeStruct((b, s, d), f32),
                   jax.ShapeDtypeStruct((b, s, d), f32),
                   jax.ShapeDtypeStruct((b, s, ROUTE_W), f32),
                   jax.ShapeDtypeStruct((b, s, ROUTE_W), i32),
                   jax.ShapeDtypeStruct((b, s, ROUTE_W), i32),
                   jax.ShapeDtypeStruct((1, N_EXPERTS), f32)),
        grid=(b, s // ts),
        in_specs=[tok(MLA_WIDTH), tok(RWKV_WIDTH), tok(d),
                  pl.BlockSpec((1, 6, d), lambda i, j: (i, 0, 0)),
                  full(1, MLA_WIDTH), full(d, d), full(1, d), full(d, N_EXPERTS),
                  full(1, N_EXPERTS), full(ts, ts), full(1, N_EXPERTS)],
        out_specs=(tok(d), tok(d), tok(ROUTE_W), tok(ROUTE_W), tok(ROUTE_W),
                   full(1, N_EXPERTS)),
        scratch_shapes=[pltpu.VMEM((1, N_EXPERTS), f32)],
        compiler_params=_cparams(("arbitrary", "arbitrary")),
    )(o_mla, o_rw, x, mod, wts["mla_out_norm"], wts["w_out_bf"], wts["norm_ffn"],
      wts["w_router"], wts["b_router"], tri, counts0)


def _dispatch_kernel(dest_ref, h_ref, xs_in_ref, xs_ref, sem):
    del xs_in_ref
    tt = h_ref.shape[1]

    def row_copy(t, d):
        return pltpu.make_async_copy(h_ref.at[0, pl.ds(t, 1), :], xs_ref.at[pl.ds(d, 1), :], sem)

    def issue(t, carry):
        for k in range(TOP_K):
            row_copy(t, dest_ref[0, 0, TOP_K * t + k]).start()
        return carry

    def drain(t, carry):
        for k in range(TOP_K):
            row_copy(0, 0).wait()
        return carry

    lax.fori_loop(0, tt, issue, 0)
    lax.fori_loop(0, tt, drain, 0)


def _dispatch(h2, dest, xs):
    b, s, d = h2.shape
    tt = _token_tile(s)
    nt = s // tt
    dest_t = dest.reshape(b * nt, 1, tt * TOP_K)
    return pl.pallas_call(
        _dispatch_kernel,
        name="moe_dispatch",
        out_shape=jax.ShapeDtypeStruct(xs.shape, xs.dtype),
        grid=(b, nt),
        in_specs=[pl.BlockSpec((1, 1, tt * TOP_K), lambda i, j: (i * nt + j, 0, 0),
                               memory_space=pltpu.SMEM),
                  pl.BlockSpec((1, tt, d), lambda i, j: (i, j, 0)),
                  pl.BlockSpec(memory_space=pl.ANY)],
        out_specs=pl.BlockSpec(memory_space=pl.ANY),
        scratch_shapes=[pltpu.SemaphoreType.DMA(())],
        input_output_aliases={2: 0},
        compiler_params=_cparams(("arbitrary", "arbitrary")),
    )(dest_t, h2, xs)


def _expert_kernel(be_ref, nu_ref, x_ref, wgu_ref, bgu_ref, wd_ref, bd_ref, y_ref,
                   wgu_bf, wd_bf):
    i = pl.program_id(0)
    prev = be_ref[jnp.maximum(i - 1, 0)]
    fresh = (i == 0) | (be_ref[i] != prev)

    @pl.when(fresh & (i < nu_ref[0]))
    def _():
        wgu_bf[...] = wgu_ref[0].astype(bf16)
        wd_bf[...] = wd_ref[0].astype(bf16)

    @pl.when(i < nu_ref[0])
    def _():
        gu = jnp.dot(x_ref[...].astype(bf16), wgu_bf[...], preferred_element_type=f32) + bgu_ref[0]
        gate = jnp.minimum(gu[:, :D_FF], SWIGLU_LIMIT)
        up = jnp.clip(gu[:, D_FF:], -SWIGLU_LIMIT, SWIGLU_LIMIT)
        act = (up + 1.0) * (gate * _sigmoid(SWIGLU_ALPHA * gate))
        y_ref[...] = jnp.dot(act.astype(bf16), wd_bf[...], preferred_element_type=f32) + bd_ref[0]

    @pl.when(i >= nu_ref[0])
    def _():
        y_ref[...] = jnp.zeros(y_ref.shape, f32)


def _experts(xs, block_e, n_used, w_gate_up, b_gate_up, w_down, b_down):
    n_rows, d = xs.shape
    tm = EXPERT_TILE
    nb = n_rows // tm
    last = lambda i, be, nu: jnp.minimum(i, nu[0] - 1)
    grid_spec = pltpu.PrefetchScalarGridSpec(
        num_scalar_prefetch=2,
        grid=(nb,),
        in_specs=[pl.BlockSpec((tm, d), lambda i, be, nu: (last(i, be, nu), 0)),
                  pl.BlockSpec((1, d, 2 * D_FF), lambda i, be, nu: (be[last(i, be, nu)], 0, 0)),
                  pl.BlockSpec((1, 1, 2 * D_FF), lambda i, be, nu: (be[last(i, be, nu)], 0, 0)),
                  pl.BlockSpec((1, D_FF, d), lambda i, be, nu: (be[last(i, be, nu)], 0, 0)),
                  pl.BlockSpec((1, 1, d), lambda i, be, nu: (be[last(i, be, nu)], 0, 0))],
        out_specs=pl.BlockSpec((tm, d), lambda i, be, nu: (i, 0)),
        scratch_shapes=[pltpu.VMEM((d, 2 * D_FF), bf16), pltpu.VMEM((D_FF, d), bf16)])
    return pl.pallas_call(
        _expert_kernel,
        name="moe_experts",
        out_shape=jax.ShapeDtypeStruct((n_rows, d), f32),
        grid_spec=grid_spec,
        compiler_params=_cparams(("arbitrary",)),
    )(block_e, n_used, xs, w_gate_up, b_gate_up.reshape(N_EXPERTS, 1, -1), w_down,
      b_down.reshape(N_EXPERTS, 1, -1))


def _combine_kernel(dest_ref, x1_ref, gate_ref, mod_ref, nf_ref, yb_ref, o_ref, buf, sem, *, final):
    tt = x1_ref.shape[1]

    def row_copy(t, k, d):
        return pltpu.make_async_copy(yb_ref.at[pl.ds(d, 1), :], buf.at[k, pl.ds(t, 1), :], sem)

    def issue(t, carry):
        for k in range(TOP_K):
            row_copy(t, k, dest_ref[0, 0, TOP_K * t + k]).start()
        return carry

    def drain(t, carry):
        for k in range(TOP_K):
            row_copy(0, 0, 0).wait()
        return carry

    lax.fori_loop(0, tt, issue, 0)
    lax.fori_loop(0, tt, drain, 0)
    gates = gate_ref[0]
    y = gates[:, 0:1] * buf[0]
    for k in range(1, TOP_K):
        y = y + gates[:, k:k + 1] * buf[k]
    x = x1_ref[0] + mod_ref[0, 5:6, :] * y
    o_ref[0] = _rms(x, nf_ref[...]) if final else x


def _combine(x1, gates, mod, dest, yb, norm_final, final):
    b, s, d = x1.shape
    tt = _token_tile(s)
    nt = s // tt
    dest_t = dest.reshape(b * nt, 1, tt * TOP_K)
    return pl.pallas_call(
        functools.partial(_combine_kernel, final=final),
        name="moe_combine",
        out_shape=jax.ShapeDtypeStruct((b, s, d), f32),
        grid=(b, nt),
        in_specs=[pl.BlockSpec((1, 1, tt * TOP_K), lambda i, j: (i * nt + j, 0, 0),
                               memory_space=pltpu.SMEM),
                  pl.BlockSpec((1, tt, d), lambda i, j: (i, j, 0)),
                  pl.BlockSpec((1, tt, ROUTE_W), lambda i, j: (i, j, 0)),
                  pl.BlockSpec((1, 6, d), lambda i, j: (i, 0, 0)),
                  pl.BlockSpec((1, d), lambda i, j: (0, 0)),
                  pl.BlockSpec(memory_space=pl.ANY)],
        out_specs=pl.BlockSpec((1, tt, d), lambda i, j: (i, j, 0)),
        scratch_shapes=[pltpu.VMEM((TOP_K, tt, d), f32), pltpu.SemaphoreType.DMA(())],
        compiler_params=_cparams(("arbitrary", "arbitrary")),
    )(dest_t, x1, gates, mod, norm_final.reshape(1, d), yb)


def _rot_half_cols(w):
    half = w.shape[-1] // 2
    return jnp.concatenate([-w[..., half:], w[..., :half]], axis=-1)


def _layer_weights(w_in, q_norm, w_uq, kv_norm, w_ukv, mla_out_norm, shift_mu, decay_w0, decay_w2,
                   iclr_a0, iclr_a2, gate_g2, k_k, k_a, r_k, lnx_g, lnx_b, w_out, norm_ffn,
                   w_router, b_router):
    d = w_in.shape[0]
    kr_cols = w_in[:, Q_LORA + KV_LORA:MLA_COLS]
    w_in_ext = jnp.concatenate([w_in[:, :MLA_COLS], _rot_half_cols(kr_cols), w_in[:, MLA_COLS:]],
                               axis=1)
    uq = w_uq.reshape(Q_LORA, MLA_HEADS, QK_NOPE + QK_ROPE)
    zq = jnp.zeros((Q_LORA, MLA_HEADS, QK_PAD - QK_NOPE - QK_ROPE), f32)
    w_qa = jnp.concatenate([uq, zq], axis=2).reshape(Q_LORA, MLA_HEADS * QK_PAD)
    w_qb = jnp.concatenate([jnp.zeros_like(uq[..., :QK_NOPE]), _rot_half_cols(uq[..., QK_NOPE:]), zq],
                           axis=2).reshape(Q_LORA, MLA_HEADS * QK_PAD)
    ukv = w_ukv.reshape(KV_LORA, MLA_HEADS, QK_NOPE + V_HEAD)
    zl = jnp.zeros((DECAY_LORA, RWKV_WIDTH), f32)
    hid = jnp.arange(RWKV_WIDTH) // RWKV_HEAD
    return dict(
        w_in_bf=w_in_ext.astype(bf16),
        q_norm=q_norm, kv_norm=kv_norm,
        w_qa=w_qa.astype(bf16), w_qb=w_qb.astype(bf16),
        w_kn=ukv[..., :QK_NOPE].reshape(KV_LORA, -1).astype(bf16),
        w_v=ukv[..., QK_NOPE:].reshape(KV_LORA, -1).astype(bf16),
        mla_out_norm=mla_out_norm.reshape(1, -1),
        shift_mu=shift_mu.reshape(1, -1), decay_w0=decay_w0.reshape(1, -1),
        decay_w2p=jnp.concatenate([decay_w2, zl], axis=0),
        iclr_a0=iclr_a0.reshape(1, -1),
        iclr_a2p=jnp.concatenate([zl, iclr_a2], axis=0),
        gate_g2=gate_g2, k_k=k_k.reshape(1, -1), k_a=k_a.reshape(1, -1), r_k=r_k.reshape(1, -1),
        lnx_g=lnx_g.reshape(1, -1), lnx_b=lnx_b.reshape(1, -1),
        seg=(hid[:, None] == hid[None, :]).astype(bf16),
        w_out_bf=w_out.astype(bf16), norm_ffn=norm_ffn.reshape(1, d),
        w_router=w_router, b_router=b_router.reshape(1, -1),
    )


def _rope_tables(pos):
    inv = ROPE_THETA ** (-jnp.arange(0, QK_ROPE, 2, dtype=f32) / QK_ROPE)
    ang = pos.astype(f32)[:, None] * inv[None, :]
    cos, sin = jnp.cos(ang), jnp.sin(ang)
    cos2 = jnp.concatenate([cos, cos], axis=1)
    sin2 = jnp.concatenate([sin, sin], axis=1)
    scale = (QK_NOPE + QK_ROPE) ** -0.5
    n = pos.shape[0]
    zpad = jnp.zeros((n, QK_PAD - QK_NOPE - QK_ROPE), f32)
    ct = jnp.concatenate([jnp.full((n, QK_NOPE), scale, f32), cos2 * scale, zpad], axis=1)
    st = jnp.concatenate([jnp.zeros((n, QK_NOPE), f32), sin2 * scale, zpad], axis=1)
    kt = jnp.concatenate([cos2, sin2], axis=1)
    return ct, st, kt


def _pair_states(state):
    b = state.shape[0]
    s = state.reshape(b, N_PAIRS, 2, RWKV_HEAD, RWKV_HEAD)
    z = jnp.zeros_like(s[:, :, 0])
    top = jnp.concatenate([s[:, :, 0], z], axis=-1)
    bot = jnp.concatenate([z, s[:, :, 1]], axis=-1)
    return jnp.concatenate([top, bot], axis=-2)


def _unpair_states(sp):
    b = sp.shape[0]
    h0 = sp[:, :, :RWKV_HEAD, :RWKV_HEAD]
    h1 = sp[:, :, RWKV_HEAD:, RWKV_HEAD:]
    return jnp.stack([h0, h1], axis=2).reshape(b, RWKV_HEADS, RWKV_HEAD, RWKV_HEAD)


def _mix_path(x, mod, pos, cache_lat, cache_rope, wkv_prev, shift_prev, wts, counts0):
    b, s, d = x.shape
    pm, p_rw = _inproj(x, mod, wts["norm_mix"], wts["w_in_bf"])
    ct, st, kt = _rope_tables(pos)
    q, new_lat, new_rope, rope128 = _mla_prep(pm, wts["q_norm"], wts["w_qa"], wts["w_qb"],
                                              wts["kv_norm"], ct, st, kt)
    if cache_lat is None:
        k, v = _kv_proj(new_lat, rope128, wts["w_kn"], wts["w_v"])
        o_mla = _flash_attention(q, k, v)
    else:
        past = cache_lat.shape[1]
        n_keys = past + s
        sk = -(-n_keys // 128) * 128
        lat_all = jnp.concatenate([cache_lat, new_lat, jnp.zeros((b, sk - n_keys, KV_LORA), f32)],
                                  axis=1)
        cache_rope128 = jnp.concatenate([cache_rope, jnp.zeros_like(cache_rope)], axis=-1)
        rope_all = jnp.concatenate([cache_rope128, rope128,
                                    jnp.zeros((b, sk - n_keys, 2 * QK_ROPE), f32)], axis=1)
        k, v = _kv_proj(lat_all, rope_all, wts["w_kn"], wts["w_v"])
        o_mla = _cached_attention(q, k, v, past, n_keys)

    seqs, new_shift = _rwkv_prep(p_rw, shift_prev, wts)
    s_pad = -(-s // SCAN_CHUNK) * SCAN_CHUNK
    if s_pad != s:
        seqs = tuple(jnp.pad(t, ((0, 0), (0, s_pad - s), (0, 0))) for t in seqs)
    o_rw, s_fin = _rwkv_scan(seqs, _pair_states(wkv_prev), wts["lnx_g"], wts["lnx_b"], wts["seg"])
    o_rw = o_rw[:, :s]
    new_wkv = _unpair_states(s_fin)

    x1, h2, gates, idx, rank, counts = _outproj(o_mla, o_rw, x, mod, wts, counts0)
    return dict(x1=x1, h2=h2, gates=gates, idx=idx, rank=rank, counts=counts,
                new_lat=new_lat, new_rope=new_rope, new_wkv=new_wkv, new_shift=new_shift)


def kernel(x_prompt, x_sample, cache_kv_latent, cache_k_rope, state_wkv, state_shift, c_prompt, c_sample, w_ada, b_ada, norm_mix, w_in, q_norm, w_uq, kv_norm, w_ukv, mla_out_norm, shift_mu, decay_w0, decay_w2, iclr_a0, iclr_a2, gate_g2, k_k, k_a, r_k, lnx_g, lnx_b, w_out, norm_ffn, w_router, b_router, w_gate_up, b_gate_up, w_down, b_down, norm_final):
    depth = w_ada.shape[0]
    bp, seq_p, d = x_prompt.shape
    bs, seq_s, _ = x_sample.shape
    past = cache_kv_latent.shape[2]
    pos_p = jnp.arange(seq_p, dtype=i32)
    pos_s = past + jnp.arange(seq_s, dtype=i32)
    zero_wkv = jnp.zeros((bp, RWKV_HEADS, RWKV_HEAD, RWKV_HEAD), f32)
    zero_shift = jnp.zeros((bp, RWKV_COLS), f32)
    n_c = bp + bs
    c_rows = -(-n_c // 8) * 8
    c_all = jnp.concatenate([c_prompt, c_sample, jnp.zeros((c_rows - n_c, d), f32)], axis=0)

    hp, hs = x_prompt, x_sample
    outs_p = [[], [], [], []]
    outs_s = [[], [], [], []]
    tm = EXPERT_TILE
    n_pairs = (bp * seq_p + bs * seq_s) * TOP_K
    n_blocks = (n_pairs + N_EXPERTS * (tm - 1) + tm - 1) // tm
    for l in range(depth):
        wts = _layer_weights(w_in[l], q_norm[l], w_uq[l], kv_norm[l], w_ukv[l], mla_out_norm[l],
                             shift_mu[l], decay_w0[l], decay_w2[l], iclr_a0[l], iclr_a2[l],
                             gate_g2[l], k_k[l], k_a[l], r_k[l], lnx_g[l], lnx_b[l], w_out[l],
                             norm_ffn[l], w_router[l], b_router[l])
        wts["norm_mix"] = norm_mix[l]
        mod = _modulation(c_all, w_ada[l], b_ada[l]).reshape(c_rows, 6, d)
        mod_p, mod_s = mod[:bp], mod[bp:n_c]

        rp = _mix_path(hp, mod_p, pos_p, None, None, zero_wkv, zero_shift, wts,
                       jnp.zeros((1, N_EXPERTS), f32))
        rs = _mix_path(hs, mod_s, pos_s, cache_kv_latent[l], cache_k_rope[l], state_wkv[l],
                       state_shift[l], wts, rp["counts"])

        counts = rs["counts"][0].astype(i32)
        padded = (counts + tm - 1) // tm * tm
        pad_end = jnp.cumsum(padded)
        pad_start = pad_end - padded
        dest_p = pad_start[rp["idx"][..., :TOP_K]] + rp["rank"][..., :TOP_K]
        dest_s = pad_start[rs["idx"][..., :TOP_K]] + rs["rank"][..., :TOP_K]
        block_e = jnp.minimum(
            jnp.searchsorted(pad_end, jnp.arange(n_blocks, dtype=i32) * tm, side="right"),
            N_EXPERTS - 1).astype(i32)
        n_used = (pad_end[-1:] // tm).astype(i32)

        xs = jnp.zeros((n_blocks * tm, d), f32)
        xs = _dispatch(rp["h2"], dest_p, xs)
        xs = _dispatch(rs["h2"], dest_s, xs)
        yb = _experts(xs, block_e, n_used, w_gate_up[l], b_gate_up[l], w_down[l], b_down[l])

        last = l == depth - 1
        hp = _combine(rp["x1"], rp["gates"], mod_p, dest_p, yb, norm_final, last)
        hs = _combine(rs["x1"], rs["gates"], mod_s, dest_s, yb, norm_final, last)
        for acc, r in ((outs_p, rp), (outs_s, rs)):
            acc[0].append(r["new_lat"])
            acc[1].append(r["new_rope"])
            acc[2].append(r["new_wkv"])
            acc[3].append(r["new_shift"])

    return (hp, hs, jnp.stack(outs_p[0]), jnp.stack(outs_p[1]), jnp.stack(outs_p[2]),
            jnp.stack(outs_p[3]), jnp.stack(outs_s[0]), jnp.stack(outs_s[1]),
            jnp.stack(outs_s[2]), jnp.stack(outs_s[3]))
```

```python
import functools

import jax
import jax.numpy as jnp
from jax import lax
from jax.experimental import pallas as pl
from jax.experimental.pallas import tpu as pltpu

f32 = jnp.float32
bf16 = jnp.bfloat16
i32 = jnp.int32

D_MODEL = 1024
CHUNK = 64
MLA_HEADS = 4
QK_NOPE = 128
QK_ROPE = 64
V_HEAD = 128
Q_LORA = 256
KV_LORA = 128
ROPE_THETA = 10000.0
MLA_WIDTH = MLA_HEADS * V_HEAD
RWKV_HEAD = 64
RWKV_HEADS = 8
RWKV_WIDTH = RWKV_HEADS * RWKV_HEAD
DECAY_LORA = 64
ICLR_LORA = 64
GATE_LORA = 128
LN_X_EPS = 64e-5
MLA_COLS = Q_LORA + KV_LORA + QK_ROPE
RWKV_COLS = 3 * RWKV_WIDTH + DECAY_LORA + ICLR_LORA + GATE_LORA
N_EXPERTS = 32
TOP_K = 4
D_FF = 1024
SWIGLU_ALPHA = 1.702
SWIGLU_LIMIT = 7.0
NORM_EPS = 1e-6
LOG2_E = 1.4426950408889634

MLA_PAD = 512
QK_PAD = 256
N_PAIRS = RWKV_HEADS // 2
PAIR_W = 2 * RWKV_HEAD
ROUTE_W = 128

TOKEN_TILE = 256
ATTN_TILE = 1024
ATTN_SUB = 256
SCAN_CHUNK = 64
SCAN_STEP_CHUNKS = 2
SCAN_PASSES_GRAM = 1
SCAN_PASSES_INV = 1
SCAN_PASSES_STATE = 3
SCAN_PASSES_OUT = 1
EXPERT_TILE = 512
VMEM_LIMIT = 56 * 1024 * 1024


def _token_tile(s):
    t = TOKEN_TILE
    while s % t:
        t //= 2
    assert t >= 8, s
    return t


def _cparams(sem, vmem=None):
    return pltpu.CompilerParams(dimension_semantics=sem, vmem_limit_bytes=vmem or VMEM_LIMIT)


def _dot(a, b):
    return jnp.dot(a.astype(bf16), b.astype(bf16), preferred_element_type=f32)


def _dg(a, b, ca, cb):
    return lax.dot_general(a, b, (((ca,), (cb,)), ((), ())), preferred_element_type=f32)


def _split2(x):
    hi = x.astype(bf16)
    lo = (x - hi.astype(f32)).astype(bf16)
    return hi, lo


def _split3(x):
    hi = x.astype(bf16)
    r1 = x - hi.astype(f32)
    mid = r1.astype(bf16)
    lo = (r1 - mid.astype(f32)).astype(bf16)
    return hi, mid, lo


def _dot3(a, b, ca=1, cb=0):
    ah, al = _split2(a)
    bh, bl = _split2(b)
    return _dg(ah, bh, ca, cb) + (_dg(ah, bl, ca, cb) + _dg(al, bh, ca, cb))


def _dot_exact_lhs(a_bf, b):
    bh, bm, bl = _split3(b)
    return _dg(a_bf, bh, 1, 0) + (_dg(a_bf, bm, 1, 0) + _dg(a_bf, bl, 1, 0))


def _seg_sum(x, seg_bf):
    xh, xm, xl = _split3(x)
    return _dg(xh, seg_bf, 1, 0) + (_dg(xm, seg_bf, 1, 0) + _dg(xl, seg_bf, 1, 0))


def _rms(x, g):
    return x * lax.rsqrt(jnp.mean(x * x, axis=-1, keepdims=True) + NORM_EPS) * g


def _sigmoid(x):
    return 1.0 / (1.0 + jnp.exp(-x))


def _mod_kernel(c_ref, w_ref, b_ref, o_ref):
    c = c_ref[...]
    o_ref[...] = _dot3(c * _sigmoid(c), w_ref[...]) + b_ref[...]


def _modulation(c_all, w_ada, b_ada):
    rows, d = c_all.shape
    n = w_ada.shape[1]
    tn = 1536
    return pl.pallas_call(
        _mod_kernel,
        name="adaln_mod",
        out_shape=jax.ShapeDtypeStruct((rows, n), f32),
        grid=(n // tn,),
        in_specs=[pl.BlockSpec((rows, d), lambda j: (0, 0)),
                  pl.BlockSpec((d, tn), lambda j: (0, j)),
                  pl.BlockSpec((1, tn), lambda j: (0, j))],
        out_specs=pl.BlockSpec((rows, tn), lambda j: (0, j)),
        compiler_params=_cparams(("arbitrary",)),
    )(c_all, w_ada, b_ada.reshape(1, n))


def _inproj_kernel(x_ref, mod_ref, nw_ref, w_ref, pm_ref, prw_ref):
    h = _rms(x_ref[0], nw_ref[...])
    h = h * (1.0 + mod_ref[0, 1:2, :]) + mod_ref[0, 0:1, :]
    p = jnp.dot(h.astype(bf16), w_ref[...], preferred_element_type=f32)
    pm_ref[0] = p[:, :MLA_PAD]
    prw_ref[0] = p[:, MLA_PAD:]


def _inproj(x, mod, norm_w, w_in_bf):
    b, s, d = x.shape
    ts = _token_tile(s)
    n = w_in_bf.shape[1]
    return pl.pallas_call(
        _inproj_kernel,
        name="inproj",
        out_shape=(jax.ShapeDtypeStruct((b, s, MLA_PAD), f32),
                   jax.ShapeDtypeStruct((b, s, RWKV_COLS), f32)),
        grid=(b, s // ts),
        in_specs=[pl.BlockSpec((1, ts, d), lambda i, j: (i, j, 0)),
                  pl.BlockSpec((1, 6, d), lambda i, j: (i, 0, 0)),
                  pl.BlockSpec((1, d), lambda i, j: (0, 0)),
                  pl.BlockSpec((d, n), lambda i, j: (0, 0))],
        out_specs=(pl.BlockSpec((1, ts, MLA_PAD), lambda i, j: (i, j, 0)),
                   pl.BlockSpec((1, ts, RWKV_COLS), lambda i, j: (i, j, 0))),
        compiler_params=_cparams(("arbitrary", "arbitrary")),
    )(x, mod, norm_w.reshape(1, d), w_in_bf)


def _mla_prep_kernel(pm_ref, qn_ref, wa_ref, wb_ref, kvn_ref, ct_ref, st_ref, kt_ref,
                     q_ref, lat_ref, rope_ref, rope128_ref):
    pm = pm_ref[0]
    qn = _rms(pm[:, :Q_LORA], qn_ref[...]).astype(bf16)
    qa = jnp.dot(qn, wa_ref[...], preferred_element_type=f32)
    qb = jnp.dot(qn, wb_ref[...], preferred_element_type=f32)
    ct = ct_ref[...]
    st = st_ref[...]
    for h in range(MLA_HEADS):
        sl = slice(h * QK_PAD, (h + 1) * QK_PAD)
        q_ref[0, h] = (qa[:, sl] * ct + qb[:, sl] * st).astype(bf16)
    lat_ref[0] = _rms(pm[:, Q_LORA:Q_LORA + KV_LORA], kvn_ref[...])
    t = pm[:, Q_LORA + KV_LORA:] * kt_ref[...]
    kr = t + pltpu.roll(t, QK_ROPE, axis=1)
    lane = lax.broadcasted_iota(i32, kr.shape, 1)
    rope128_ref[0] = jnp.where(lane < QK_ROPE, kr, 0.0)
    rope_ref[0] = kr[:, :QK_ROPE]


def _mla_prep(pm, q_norm, w_qa, w_qb, kv_norm, ct, st, kt):
    b, s, _ = pm.shape
    ts = _token_tile(s)
    hq = MLA_HEADS * QK_PAD
    return pl.pallas_call(
        _mla_prep_kernel,
        name="mla_prep",
        out_shape=(jax.ShapeDtypeStruct((b, MLA_HEADS, s, QK_PAD), bf16),
                   jax.ShapeDtypeStruct((b, s, KV_LORA), f32),
                   jax.ShapeDtypeStruct((b, s, QK_ROPE), f32),
                   jax.ShapeDtypeStruct((b, s, 2 * QK_ROPE), f32)),
        grid=(b, s // ts),
        in_specs=[pl.BlockSpec((1, ts, MLA_PAD), lambda i, j: (i, j, 0)),
                  pl.BlockSpec((1, Q_LORA), lambda i, j: (0, 0)),
                  pl.BlockSpec((Q_LORA, hq), lambda i, j: (0, 0)),
                  pl.BlockSpec((Q_LORA, hq), lambda i, j: (0, 0)),
                  pl.BlockSpec((1, KV_LORA), lambda i, j: (0, 0)),
                  pl.BlockSpec((ts, QK_PAD), lambda i, j: (j, 0)),
                  pl.BlockSpec((ts, QK_PAD), lambda i, j: (j, 0)),
                  pl.BlockSpec((ts, 2 * QK_ROPE), lambda i, j: (j, 0))],
        out_specs=(pl.BlockSpec((1, MLA_HEADS, ts, QK_PAD), lambda i, j: (i, 0, j, 0)),
                   pl.BlockSpec((1, ts, KV_LORA), lambda i, j: (i, j, 0)),
                   pl.BlockSpec((1, ts, QK_ROPE), lambda i, j: (i, j, 0)),
                   pl.BlockSpec((1, ts, 2 * QK_ROPE), lambda i, j: (i, j, 0))),
        compiler_params=_cparams(("arbitrary", "arbitrary")),
    )(pm, q_norm.reshape(1, -1), w_qa, w_qb, kv_norm.reshape(1, -1), ct, st, kt)


def _kv_proj_kernel(lat_ref, rope_ref, wk_ref, wv_ref, k_ref, v_ref):
    lat = lat_ref[0].astype(bf16)
    kn = jnp.dot(lat, wk_ref[...], preferred_element_type=f32)
    vv = jnp.dot(lat, wv_ref[...], preferred_element_type=f32)
    rope = rope_ref[0].astype(bf16)
    for h in range(MLA_HEADS):
        k_ref[0, h, :, :QK_NOPE] = kn[:, h * QK_NOPE:(h + 1) * QK_NOPE].astype(bf16)
        k_ref[0, h, :, QK_NOPE:] = rope
        v_ref[0, h] = vv[:, h * V_HEAD:(h + 1) * V_HEAD].astype(bf16)


def _kv_proj(lat, rope128, w_kn, w_v):
    b, s, _ = lat.shape
    ts = _token_tile(s)
    return pl.pallas_call(
        _kv_proj_kernel,
        name="kv_proj",
        out_shape=(jax.ShapeDtypeStruct((b, MLA_HEADS, s, QK_PAD), bf16),
                   jax.ShapeDtypeStruct((b, MLA_HEADS, s, V_HEAD), bf16)),
        grid=(b, s // ts),
        in_specs=[pl.BlockSpec((1, ts, KV_LORA), lambda i, j: (i, j, 0)),
                  pl.BlockSpec((1, ts, 2 * QK_ROPE), lambda i, j: (i, j, 0)),
                  pl.BlockSpec((KV_LORA, MLA_HEADS * QK_NOPE), lambda i, j: (0, 0)),
                  pl.BlockSpec((KV_LORA, MLA_HEADS * V_HEAD), lambda i, j: (0, 0))],
        out_specs=(pl.BlockSpec((1, MLA_HEADS, ts, QK_PAD), lambda i, j: (i, 0, j, 0)),
                   pl.BlockSpec((1, MLA_HEADS, ts, V_HEAD), lambda i, j: (i, 0, j, 0))),
        compiler_params=_cparams(("arbitrary", "arbitrary")),
    )(lat, rope128, w_kn, w_v)


def _flash_kernel(qi_ref, kj_ref, q_ref, k_ref, v_ref, o_ref, m_scr, l_scr, acc_scr, *, tile, sub):
    step = pl.program_id(1)
    qi = qi_ref[step]
    kj = kj_ref[step]
    lanes = m_scr.shape[1]

    @pl.when(kj == 0)
    def _():
        m_scr[...] = jnp.full(m_scr.shape, -jnp.inf, f32)
        l_scr[...] = jnp.zeros(l_scr.shape, f32)
        acc_scr[...] = jnp.zeros(acc_scr.shape, f32)

    def update(c, r0, masked):
        rows = pl.ds(r0, tile - r0)
        keys = pl.ds(c * sub, sub)
        s = _dg(q_ref[0, 0, rows, :], k_ref[0, 0, keys, :], 1, 1)
        if masked:
            qpos = r0 + lax.broadcasted_iota(i32, s.shape, 0)
            kpos = c * sub + lax.broadcasted_iota(i32, s.shape, 1)
            s = jnp.where(kpos < (qpos // CHUNK + 1) * CHUNK, s, -jnp.inf)
        m_prev = m_scr[rows, :]
        m_new = jnp.maximum(m_prev, jnp.max(s, axis=1, keepdims=True))
        alpha = jnp.exp2(m_prev - m_new)
        p = jnp.exp2(s - jnp.concatenate([m_new] * (sub // lanes), axis=1))
        psum = p[:, :lanes]
        for t in range(1, sub // lanes):
            psum = psum + p[:, t * lanes:(t + 1) * lanes]
        l_scr[rows, :] = alpha * l_scr[rows, :] + psum
        acc_scr[rows, :] = alpha * acc_scr[rows, :] + jnp.dot(
            p.astype(bf16), v_ref[0, 0, keys, :], preferred_element_type=f32)
        m_scr[rows, :] = m_new

    n_sub = tile // sub

    @pl.when(kj < qi)
    def _():
        for c in range(n_sub):
            update(c, 0, False)

    @pl.when(kj == qi)
    def _():
        for c in range(n_sub):
            update(c, c * sub, True)
        o_ref[0] = acc_scr[...] / jnp.sum(l_scr[...], axis=1, keepdims=True)


def _flash_attention(q, k, v):
    b, nh, s, _ = q.shape
    assert b == 1
    tile = min(ATTN_TILE, s)
    sub = min(ATTN_SUB, tile)
    assert sub % CHUNK == 0 and tile % sub == 0
    nt = s // tile
    qi = [i for i in range(nt) for _ in range(i + 1)]
    kj = [j for i in range(nt) for j in range(i + 1)]
    grid_spec = pltpu.PrefetchScalarGridSpec(
        num_scalar_prefetch=2,
        grid=(nh, len(qi)),
        in_specs=[pl.BlockSpec((1, 1, tile, QK_PAD), lambda h, t, qi, kj: (0, h, qi[t], 0)),
                  pl.BlockSpec((1, 1, tile, QK_PAD), lambda h, t, qi, kj: (0, h, kj[t], 0)),
                  pl.BlockSpec((1, 1, tile, V_HEAD), lambda h, t, qi, kj: (0, h, kj[t], 0))],
        out_specs=pl.BlockSpec((1, tile, V_HEAD), lambda h, t, qi, kj: (0, qi[t], h)),
        scratch_shapes=[pltpu.VMEM((tile, V_HEAD), f32), pltpu.VMEM((tile, V_HEAD), f32),
                        pltpu.VMEM((tile, V_HEAD), f32)])
    return pl.pallas_call(
        functools.partial(_flash_kernel, tile=tile, sub=sub),
        name="flash_attn",
        out_shape=jax.ShapeDtypeStruct((1, s, MLA_WIDTH), f32),
        grid_spec=grid_spec,
        compiler_params=_cparams(("arbitrary", "arbitrary")),
    )(jnp.asarray(qi, i32), jnp.asarray(kj, i32), q, k, v)


def _cached_attn_kernel(q_ref, k_ref, v_ref, o_ref, *, past, n_keys):
    for h in range(MLA_HEADS):
        s = _dg(q_ref[0, h], k_ref[0, h], 1, 1)
        qpos = past + lax.broadcasted_iota(i32, s.shape, 0)
        kpos = lax.broadcasted_iota(i32, s.shape, 1)
        limit = (qpos // CHUNK + 1) * CHUNK
        s = jnp.where((kpos < limit) & (kpos < n_keys), s, -jnp.inf)
        m = jnp.max(s, axis=1, keepdims=True)
        p = jnp.exp2(s - m)
        o = jnp.dot(p.astype(bf16), v_ref[0, h], preferred_element_type=f32)
        o_ref[0, :, h * V_HEAD:(h + 1) * V_HEAD] = o / jnp.sum(p, axis=1, keepdims=True)


def _cached_attention(q, k, v, past, n_keys):
    b, nh, sq, _ = q.shape
    sk = k.shape[2]
    return pl.pallas_call(
        functools.partial(_cached_attn_kernel, past=past, n_keys=n_keys),
        name="cached_attn",
        out_shape=jax.ShapeDtypeStruct((b, sq, MLA_WIDTH), f32),
        grid=(b,),
        in_specs=[pl.BlockSpec((1, nh, sq, QK_PAD), lambda i: (i, 0, 0, 0)),
                  pl.BlockSpec((1, nh, sk, QK_PAD), lambda i: (i, 0, 0, 0)),
                  pl.BlockSpec((1, nh, sk, V_HEAD), lambda i: (i, 0, 0, 0))],
        out_specs=pl.BlockSpec((1, sq, MLA_WIDTH), lambda i: (i, 0, 0)),
        compiler_params=_cparams(("arbitrary",)),
    )(q, k, v)


def _rwkv_prep_kernel(p_ref, sp_ref, mu_ref, w0_ref, w2_ref, a0_ref, a2_ref, g2_ref,
                      kk_ref, ka_ref, rk_ref, seg_ref,
                      r_ref, lw_ref, kh_ref, v_ref, al_ref, be_ref, g_ref, bo_ref, ns_ref,
                      carry):
    @pl.when(pl.program_id(1) == 0)
    def _():
        carry[...] = sp_ref[0]

    p = p_ref[0]
    ts = p.shape[0]
    row = lax.broadcasted_iota(i32, p.shape, 0)
    prev = jnp.where(row == 0, carry[...], pltpu.roll(p, 1, axis=0))
    last = p[ts - 1:ts, :]
    carry[...] = last
    ns_ref[0] = last
    xs = p + (prev - p) * mu_ref[...]
    w = RWKV_WIDTH
    r, k, v = xs[:, :w], xs[:, w:2 * w], xs[:, 2 * w:3 * w]
    wa = xs[:, 3 * w:3 * w + DECAY_LORA + ICLR_LORA]
    gd = xs[:, 3 * w + DECAY_LORA + ICLR_LORA:]
    z = -(w0_ref[...] + _dot3(jnp.tanh(wa), w2_ref[...]))
    softplus = jnp.maximum(z, 0.0) + jnp.log(1.0 + jnp.exp(-jnp.abs(z)))
    lw = -jnp.exp(-softplus - 0.5)
    a = _sigmoid(a0_ref[...] + _dot3(wa, a2_ref[...]))
    g = _dot3(_sigmoid(gd), g2_ref[...])
    seg = seg_ref[...]
    kk = k * kk_ref[...]
    kk = kk * lax.rsqrt(jnp.maximum(_seg_sum(kk * kk, seg), 1e-24))
    kh = k * (1.0 + (a - 1.0) * ka_ref[...])
    r_ref[0] = r
    lw_ref[0] = lw
    kh_ref[0] = kh
    v_ref[0] = v
    al_ref[0] = -kk
    be_ref[0] = kk * a
    g_ref[0] = g
    bo_ref[0] = _seg_sum(r * kh * rk_ref[...], seg) * v


def _rwkv_prep(p_rw, shift_prev, wts):
    b, s, n = p_rw.shape
    ts = _token_tile(s)
    w = RWKV_WIDTH
    row = lambda width: pl.BlockSpec((1, width), lambda i, j: (0, 0))
    mat = lambda k: pl.BlockSpec((k, w), lambda i, j: (0, 0))
    tok = pl.BlockSpec((1, ts, w), lambda i, j: (i, j, 0))
    outs = pl.pallas_call(
        _rwkv_prep_kernel,
        name="rwkv_prep",
        out_shape=tuple([jax.ShapeDtypeStruct((b, s, w), f32)] * 8
                        + [jax.ShapeDtypeStruct((b, 1, n), f32)]),
        grid=(b, s // ts),
        in_specs=[pl.BlockSpec((1, ts, n), lambda i, j: (i, j, 0)),
                  pl.BlockSpec((1, 1, n), lambda i, j: (i, 0, 0)),
                  row(n), row(w), mat(PAIR_W), row(w), mat(PAIR_W), mat(GATE_LORA),
                  row(w), row(w), row(w), mat(w)],
        out_specs=tuple([tok] * 8 + [pl.BlockSpec((1, 1, n), lambda i, j: (i, 0, 0))]),
        scratch_shapes=[pltpu.VMEM((1, n), f32)],
        compiler_params=_cparams(("arbitrary", "arbitrary")),
    )(p_rw, shift_prev.reshape(b, 1, n), wts["shift_mu"], wts["decay_w0"], wts["decay_w2p"],
      wts["iclr_a0"], wts["iclr_a2p"], wts["gate_g2"], wts["k_k"], wts["k_a"], wts["r_k"],
      wts["seg"])
    return outs[:8], outs[8].reshape(b, n)


def _mm(a, b, ca, cb, passes):
    if passes == 1:
        return _dg(a.astype(bf16), b.astype(bf16), ca, cb)
    return _dot3(a, b, ca, cb)


def _scan_kernel(r_ref, lw_ref, k_ref, v_ref, al_ref, be_ref, s0_ref, y_ref, sf_ref, st, *, c):
    @pl.when(pl.program_id(1) == 0)
    def _():
        st[...] = s0_ref[0]

    rows = r_ref.shape[1]
    nch = rows // c
    c2 = 2 * c
    rowb = lax.broadcasted_iota(i32, (rows, rows), 0)
    colb = lax.broadcasted_iota(i32, (rows, rows), 1)
    same_chunk = (rowb // c) == (colb // c)

    lw = lw_ref[0]
    cum = _dot_exact_lhs((same_chunk & (colb <= rowb)).astype(bf16), lw)
    tot = _dot_exact_lhs(same_chunk.astype(bf16), lw)
    g_inv = jnp.exp(-cum)
    g_tail = jnp.exp(tot - cum)
    r_t = r_ref[0] * jnp.exp(cum)
    a_t = al_ref[0] * jnp.exp(cum - lw)
    b_t = be_ref[0] * g_inv
    k_t = k_ref[0] * g_inv
    b_c = be_ref[0] * g_tail
    k_c = k_ref[0] * g_tail
    v_all = v_ref[0]

    ri = lax.broadcasted_iota(i32, (c2, c2), 0)
    ci = lax.broadcasted_iota(i32, (c2, c2), 1)
    same = (ri // c) == (ci // c)
    strict = same & (ci < ri)
    incl = same & (ci <= ri)
    eye = (ri == ci).astype(f32)
    head0 = lax.broadcasted_iota(i32, (c, PAIR_W), 1) < RWKV_HEAD
    ones = jnp.ones((c, PAIR_W), bf16)

    def stack(x):
        return jnp.concatenate([jnp.where(head0, x, 0.0), jnp.where(head0, 0.0, x)], axis=0)

    chains = [(j, p) for j in range(nch) for p in range(N_PAIRS)]

    def part(t, j, p):
        return t[j * c:(j + 1) * c, p * PAIR_W:(p + 1) * PAIR_W]

    ops = {}
    for j, p in chains:
        ops[j, p] = {n: stack(part(t, j, p)) for n, t in
                     (("a", a_t), ("r", r_t), ("b", b_t), ("k", k_t), ("v", v_all),
                      ("bc", b_c), ("kc", k_c))}

    gram = {jp: _mm(jnp.concatenate([o["a"], o["r"]], axis=0),
                    jnp.concatenate([o["b"], o["k"]], axis=0), 1, 1, SCAN_PASSES_GRAM)
            for jp, o in ops.items()}
    a_ab = {jp: jnp.where(strict, g[:c2, :c2], 0.0) for jp, g in gram.items()}
    a_ak = {jp: jnp.where(strict, g[:c2, c2:], 0.0) for jp, g in gram.items()}
    a_rb = {jp: jnp.where(incl, g[c2:, :c2], 0.0) for jp, g in gram.items()}
    a_rk = {jp: jnp.where(incl, g[c2:, c2:], 0.0) for jp, g in gram.items()}
    t_inv = {jp: eye + low for jp, low in a_ab.items()}
    pw = dict(a_ab)
    for _ in range(max(c.bit_length() - 2, 0)):
        pw = {jp: _mm(m, m, 1, 0, SCAN_PASSES_INV) for jp, m in pw.items()}
        t_inv = {jp: t_inv[jp] + _mm(t_inv[jp], pw[jp], 1, 0, SCAN_PASSES_INV) for jp in pw}
    decay = {}
    for j, p in chains:
        lh, lm, ll = _split3(part(lw, j, p))
        decay[j, p] = jnp.exp(_dg(lh, ones, 0, 0) + (_dg(lm, ones, 0, 0) + _dg(ll, ones, 0, 0)))

    states = [st[p] for p in range(N_PAIRS)]
    y_rows = []
    prs = range(N_PAIRS)
    for j in range(nch):
        x = [_mm(jnp.concatenate([ops[j, p]["a"], a_ak[j, p]], axis=1),
                 jnp.concatenate([states[p], ops[j, p]["v"]], axis=0), 1, 0, SCAN_PASSES_STATE)
             for p in prs]
        u = [_mm(t_inv[j, p], x[p], 1, 0, SCAN_PASSES_STATE) for p in prs]
        upd = [_mm(jnp.concatenate([ops[j, p]["bc"], ops[j, p]["kc"]], axis=0),
                   jnp.concatenate([u[p], ops[j, p]["v"]], axis=0), 0, 0, SCAN_PASSES_STATE)
               for p in prs]
        y = [_mm(jnp.concatenate([ops[j, p]["r"], a_rb[j, p], a_rk[j, p]], axis=1),
                 jnp.concatenate([states[p], u[p], ops[j, p]["v"]], axis=0), 1, 0, SCAN_PASSES_OUT)
             for p in prs]
        states = [states[p] * decay[j, p] + upd[p] for p in prs]
        y_rows.append(jnp.concatenate([yp[:c] + yp[c:] for yp in y], axis=1))

    y_ref[0] = jnp.concatenate(y_rows, axis=0)
    for p in prs:
        st[p] = states[p]
        sf_ref[0, p] = states[p]


def _rwkv_scan(seqs, s0_pairs):
    r, lw, kh, v, al, be = seqs
    b, s, w = r.shape
    rows = SCAN_CHUNK * SCAN_STEP_CHUNKS
    tok = pl.BlockSpec((1, rows, w), lambda i, j: (i, j, 0))
    stt = pl.BlockSpec((1, N_PAIRS, PAIR_W, PAIR_W), lambda i, j: (i, 0, 0, 0))
    return pl.pallas_call(
        functools.partial(_scan_kernel, c=SCAN_CHUNK),
        name="rwkv_scan",
        out_shape=(jax.ShapeDtypeStruct((b, s, w), f32),
                   jax.ShapeDtypeStruct((b, N_PAIRS, PAIR_W, PAIR_W), f32)),
        grid=(b, s // rows),
        in_specs=[tok] * 6 + [stt],
        out_specs=(tok, stt),
        scratch_shapes=[pltpu.VMEM((N_PAIRS, PAIR_W, PAIR_W), f32)],
        compiler_params=_cparams(("arbitrary", "arbitrary")),
    )(r, lw, kh, v, al, be, s0_pairs)


def _outproj_kernel(om_ref, y_ref, bo_ref, g_ref, x_ref, mod_ref, mn_ref, lng_ref, lnb_ref,
                    seg_ref, wo_ref, nf_ref, wr_ref, br_ref, tri_ref, cnt0_ref,
                    x1_ref, h2_ref, gate_ref, idx_ref, rank_ref, cnt_ref, run):
    @pl.when((pl.program_id(0) == 0) & (pl.program_id(1) == 0))
    def _():
        run[...] = cnt0_ref[...]

    y = y_ref[0]
    seg = seg_ref[...]
    inv_n = 1.0 / RWKV_HEAD
    yc = y - _seg_sum(y, seg) * inv_n
    var = _seg_sum(yc * yc, seg) * inv_n
    o_rw = (yc * lax.rsqrt(var + LN_X_EPS) * lng_ref[...] + lnb_ref[...] + bo_ref[0]) * g_ref[0]

    om = _rms(om_ref[0], mn_ref[...])
    mix = (jnp.dot(om.astype(bf16), wo_ref[:MLA_WIDTH, :], preferred_element_type=f32)
           + jnp.dot(o_rw.astype(bf16), wo_ref[MLA_WIDTH:, :], preferred_element_type=f32))
    x1 = x_ref[0] + mod_ref[0, 2:3, :] * mix
    x1_ref[0] = x1
    h2 = _rms(x1, nf_ref[...]) * (1.0 + mod_ref[0, 4:5, :]) + mod_ref[0, 3:4, :]
    h2_ref[0] = h2

    logits = _dot3(h2, wr_ref[...]) + br_ref[...]
    ts = logits.shape[0]
    lane_e = lax.broadcasted_iota(i32, logits.shape, 1)
    lane_w = lax.broadcasted_iota(i32, (ts, ROUTE_W), 1)
    tri = tri_ref[...]
    base = run[...]
    gates = jnp.zeros((ts, ROUTE_W), f32)
    idxs = jnp.zeros((ts, ROUTE_W), i32)
    ranks = jnp.zeros((ts, ROUTE_W), i32)
    vals = []
    work = logits
    for k in range(TOP_K):
        m = jnp.max(work, axis=1, keepdims=True)
        sel = jnp.min(jnp.where(work == m, lane_e, N_EXPERTS), axis=1, keepdims=True)
        hit = lane_e == sel
        work = jnp.where(hit, -jnp.inf, work)
        vals.append(m)
        onehot = hit.astype(bf16)
        before = jnp.dot(tri, onehot, preferred_element_type=f32)
        rank = jnp.sum(jnp.where(hit, before + base, 0.0), axis=1, keepdims=True)
        base = base + jnp.sum(hit.astype(f32), axis=0, keepdims=True)
        idxs = jnp.where(lane_w == k, sel, idxs)
        ranks = jnp.where(lane_w == k, rank.astype(i32), ranks)
    run[...] = base
    es = [jnp.exp(vk - vals[0]) for vk in vals]
    inv = 1.0 / (es[0] + es[1] + es[2] + es[3])
    for k in range(TOP_K):
        gates = jnp.where(lane_w == k, es[k] * inv, gates)
    gate_ref[0] = gates
    idx_ref[0] = idxs
    rank_ref[0] = ranks
    cnt_ref[...] = base


def _outproj(o_mla, y_rw, bonus, gate_rw, x, mod, wts, counts0):
    b, s, d = x.shape
    w = RWKV_WIDTH
    ts = _token_tile(s)
    tok = lambda width: pl.BlockSpec((1, ts, width), lambda i, j: (i, j, 0))
    full = lambda r, c: pl.BlockSpec((r, c), lambda i, j: (0, 0))
    tri = (lax.broadcasted_iota(i32, (ts, ts), 1) < lax.broadcasted_iota(i32, (ts, ts), 0)
           ).astype(bf16)
    return pl.pallas_call(
        _outproj_kernel,
        name="outproj_route",
        out_shape=(jax.ShapeDtypeStruct((b, s, d), f32),
                   jax.ShapeDtypeStruct((b, s, d), f32),
                   jax.ShapeDtypeStruct((b, s, ROUTE_W), f32),
                   jax.ShapeDtypeStruct((b, s, ROUTE_W), i32),
                   jax.ShapeDtypeStruct((b, s, ROUTE_W), i32),
                   jax.ShapeDtypeStruct((1, N_EXPERTS), f32)),
        grid=(b, s // ts),
        in_specs=[tok(MLA_WIDTH), tok(w), tok(w), tok(w), tok(d),
                  pl.BlockSpec((1, 6, d), lambda i, j: (i, 0, 0)),
                  full(1, MLA_WIDTH), full(1, w), full(1, w), full(w, w),
                  full(d, d), full(1, d), full(d, N_EXPERTS),
                  full(1, N_EXPERTS), full(ts, ts), full(1, N_EXPERTS)],
        out_specs=(tok(d), tok(d), tok(ROUTE_W), tok(ROUTE_W), tok(ROUTE_W),
                   full(1, N_EXPERTS)),
        scratch_shapes=[pltpu.VMEM((1, N_EXPERTS), f32)],
        compiler_params=_cparams(("arbitrary", "arbitrary")),
    )(o_mla, y_rw, bonus, gate_rw, x, mod, wts["mla_out_norm"], wts["lnx_g"], wts["lnx_b"],
      wts["seg"], wts["w_out_bf"], wts["norm_ffn"], wts["w_router"], wts["b_router"], tri, counts0)


def _dispatch_kernel(dest_ref, h_ref, xs_in_ref, xs_ref, sem):
    del xs_in_ref
    tt = h_ref.shape[1]

    def row_copy(t, d):
        return pltpu.make_async_copy(h_ref.at[0, pl.ds(t, 1), :], xs_ref.at[pl.ds(d, 1), :], sem)

    def issue(t, carry):
        for k in range(TOP_K):
            row_copy(t, dest_ref[0, 0, TOP_K * t + k]).start()
        return carry

    def drain(t, carry):
        for k in range(TOP_K):
            row_copy(0, 0).wait()
        return carry

    lax.fori_loop(0, tt, issue, 0)
    lax.fori_loop(0, tt, drain, 0)


def _dispatch(h2, dest, xs):
    b, s, d = h2.shape
    tt = _token_tile(s)
    nt = s // tt
    dest_t = dest.reshape(b * nt, 1, tt * TOP_K)
    return pl.pallas_call(
        _dispatch_kernel,
        name="moe_dispatch",
        out_shape=jax.ShapeDtypeStruct(xs.shape, xs.dtype),
        grid=(b, nt),
        in_specs=[pl.BlockSpec((1, 1, tt * TOP_K), lambda i, j: (i * nt + j, 0, 0),
                               memory_space=pltpu.SMEM),
                  pl.BlockSpec((1, tt, d), lambda i, j: (i, j, 0)),
                  pl.BlockSpec(memory_space=pl.ANY)],
        out_specs=pl.BlockSpec(memory_space=pl.ANY),
        scratch_shapes=[pltpu.SemaphoreType.DMA(())],
        input_output_aliases={2: 0},
        compiler_params=_cparams(("arbitrary", "arbitrary")),
    )(dest_t, h2, xs)


def _expert_kernel(be_ref, nu_ref, x_ref, wgu_ref, bgu_ref, wd_ref, bd_ref, y_ref,
                   wgu_bf, wd_bf):
    i = pl.program_id(0)
    prev = be_ref[jnp.maximum(i - 1, 0)]
    fresh = (i == 0) | (be_ref[i] != prev)

    @pl.when(fresh & (i < nu_ref[0]))
    def _():
        wgu_bf[...] = wgu_ref[0].astype(bf16)
        wd_bf[...] = wd_ref[0].astype(bf16)

    @pl.when(i < nu_ref[0])
    def _():
        gu = jnp.dot(x_ref[...].astype(bf16), wgu_bf[...], preferred_element_type=f32) + bgu_ref[0]
        gate = jnp.minimum(gu[:, :D_FF], SWIGLU_LIMIT)
        up = jnp.clip(gu[:, D_FF:], -SWIGLU_LIMIT, SWIGLU_LIMIT)
        act = (up + 1.0) * (gate * _sigmoid(SWIGLU_ALPHA * gate))
        y_ref[...] = jnp.dot(act.astype(bf16), wd_bf[...], preferred_element_type=f32) + bd_ref[0]

    @pl.when(i >= nu_ref[0])
    def _():
        y_ref[...] = jnp.zeros(y_ref.shape, f32)


def _experts(xs, block_e, n_used, w_gate_up, b_gate_up, w_down, b_down):
    n_rows, d = xs.shape
    tm = EXPERT_TILE
    nb = n_rows // tm
    last = lambda i, be, nu: jnp.minimum(i, nu[0] - 1)
    grid_spec = pltpu.PrefetchScalarGridSpec(
        num_scalar_prefetch=2,
        grid=(nb,),
        in_specs=[pl.BlockSpec((tm, d), lambda i, be, nu: (last(i, be, nu), 0)),
                  pl.BlockSpec((1, d, 2 * D_FF), lambda i, be, nu: (be[last(i, be, nu)], 0, 0)),
                  pl.BlockSpec((1, 1, 2 * D_FF), lambda i, be, nu: (be[last(i, be, nu)], 0, 0)),
                  pl.BlockSpec((1, D_FF, d), lambda i, be, nu: (be[last(i, be, nu)], 0, 0)),
                  pl.BlockSpec((1, 1, d), lambda i, be, nu: (be[last(i, be, nu)], 0, 0))],
        out_specs=pl.BlockSpec((tm, d), lambda i, be, nu: (i, 0)),
        scratch_shapes=[pltpu.VMEM((d, 2 * D_FF), bf16), pltpu.VMEM((D_FF, d), bf16)])
    return pl.pallas_call(
        _expert_kernel,
        name="moe_experts",
        out_shape=jax.ShapeDtypeStruct((n_rows, d), f32),
        grid_spec=grid_spec,
        compiler_params=_cparams(("arbitrary",)),
    )(block_e, n_used, xs, w_gate_up, b_gate_up.reshape(N_EXPERTS, 1, -1), w_down,
      b_down.reshape(N_EXPERTS, 1, -1))


def _combine_kernel(dest_ref, x1_ref, gate_ref, mod_ref, nf_ref, yb_ref, o_ref, buf, sem, *, final):
    tt = x1_ref.shape[1]

    def row_copy(t, k, d):
        return pltpu.make_async_copy(yb_ref.at[pl.ds(d, 1), :], buf.at[k, pl.ds(t, 1), :], sem)

    def issue(t, carry):
        for k in range(TOP_K):
            row_copy(t, k, dest_ref[0, 0, TOP_K * t + k]).start()
        return carry

    def drain(t, carry):
        for k in range(TOP_K):
            row_copy(0, 0, 0).wait()
        return carry

    lax.fori_loop(0, tt, issue, 0)
    lax.fori_loop(0, tt, drain, 0)
    gates = gate_ref[0]
    y = gates[:, 0:1] * buf[0]
    for k in range(1, TOP_K):
        y = y + gates[:, k:k + 1] * buf[k]
    x = x1_ref[0] + mod_ref[0, 5:6, :] * y
    o_ref[0] = _rms(x, nf_ref[...]) if final else x


def _combine(x1, gates, mod, dest, yb, norm_final, final):
    b, s, d = x1.shape
    tt = _token_tile(s)
    nt = s // tt
    dest_t = dest.reshape(b * nt, 1, tt * TOP_K)
    return pl.pallas_call(
        functools.partial(_combine_kernel, final=final),
        name="moe_combine",
        out_shape=jax.ShapeDtypeStruct((b, s, d), f32),
        grid=(b, nt),
        in_specs=[pl.BlockSpec((1, 1, tt * TOP_K), lambda i, j: (i * nt + j, 0, 0),
                               memory_space=pltpu.SMEM),
                  pl.BlockSpec((1, tt, d), lambda i, j: (i, j, 0)),
                  pl.BlockSpec((1, tt, ROUTE_W), lambda i, j: (i, j, 0)),
                  pl.BlockSpec((1, 6, d), lambda i, j: (i, 0, 0)),
                  pl.BlockSpec((1, d), lambda i, j: (0, 0)),
                  pl.BlockSpec(memory_space=pl.ANY)],
        out_specs=pl.BlockSpec((1, tt, d), lambda i, j: (i, j, 0)),
        scratch_shapes=[pltpu.VMEM((TOP_K, tt, d), f32), pltpu.SemaphoreType.DMA(())],
        compiler_params=_cparams(("arbitrary", "arbitrary")),
    )(dest_t, x1, gates, mod, norm_final.reshape(1, d), yb)


def _rot_half_cols(w):
    half = w.shape[-1] // 2
    return jnp.concatenate([-w[..., half:], w[..., :half]], axis=-1)


def _layer_weights(w_in, q_norm, w_uq, kv_norm, w_ukv, mla_out_norm, shift_mu, decay_w0, decay_w2,
                   iclr_a0, iclr_a2, gate_g2, k_k, k_a, r_k, lnx_g, lnx_b, w_out, norm_ffn,
                   w_router, b_router):
    d = w_in.shape[0]
    kr_cols = w_in[:, Q_LORA + KV_LORA:MLA_COLS]
    w_in_ext = jnp.concatenate([w_in[:, :MLA_COLS], _rot_half_cols(kr_cols), w_in[:, MLA_COLS:]],
                               axis=1)
    uq = w_uq.reshape(Q_LORA, MLA_HEADS, QK_NOPE + QK_ROPE)
    zq = jnp.zeros((Q_LORA, MLA_HEADS, QK_PAD - QK_NOPE - QK_ROPE), f32)
    w_qa = jnp.concatenate([uq, zq], axis=2).reshape(Q_LORA, MLA_HEADS * QK_PAD)
    w_qb = jnp.concatenate([jnp.zeros_like(uq[..., :QK_NOPE]), _rot_half_cols(uq[..., QK_NOPE:]), zq],
                           axis=2).reshape(Q_LORA, MLA_HEADS * QK_PAD)
    ukv = w_ukv.reshape(KV_LORA, MLA_HEADS, QK_NOPE + V_HEAD)
    zl = jnp.zeros((DECAY_LORA, RWKV_WIDTH), f32)
    hid = jnp.arange(RWKV_WIDTH) // RWKV_HEAD
    return dict(
        w_in_bf=w_in_ext.astype(bf16),
        q_norm=q_norm, kv_norm=kv_norm,
        w_qa=w_qa.astype(bf16), w_qb=w_qb.astype(bf16),
        w_kn=ukv[..., :QK_NOPE].reshape(KV_LORA, -1).astype(bf16),
        w_v=ukv[..., QK_NOPE:].reshape(KV_LORA, -1).astype(bf16),
        mla_out_norm=mla_out_norm.reshape(1, -1),
        shift_mu=shift_mu.reshape(1, -1), decay_w0=decay_w0.reshape(1, -1),
        decay_w2p=jnp.concatenate([decay_w2, zl], axis=0),
        iclr_a0=iclr_a0.reshape(1, -1),
        iclr_a2p=jnp.concatenate([zl, iclr_a2], axis=0),
        gate_g2=gate_g2, k_k=k_k.reshape(1, -1), k_a=k_a.reshape(1, -1), r_k=r_k.reshape(1, -1),
        lnx_g=lnx_g.reshape(1, -1), lnx_b=lnx_b.reshape(1, -1),
        seg=(hid[:, None] == hid[None, :]).astype(bf16),
        w_out_bf=w_out.astype(bf16), norm_ffn=norm_ffn.reshape(1, d),
        w_router=w_router, b_router=b_router.reshape(1, -1),
    )


def _rope_tables(pos):
    inv = ROPE_THETA ** (-jnp.arange(0, QK_ROPE, 2, dtype=f32) / QK_ROPE)
    ang = pos.astype(f32)[:, None] * inv[None, :]
    cos, sin = jnp.cos(ang), jnp.sin(ang)
    cos2 = jnp.concatenate([cos, cos], axis=1)
    sin2 = jnp.concatenate([sin, sin], axis=1)
    scale = (QK_NOPE + QK_ROPE) ** -0.5 * LOG2_E
    n = pos.shape[0]
    zpad = jnp.zeros((n, QK_PAD - QK_NOPE - QK_ROPE), f32)
    ct = jnp.concatenate([jnp.full((n, QK_NOPE), scale, f32), cos2 * scale, zpad], axis=1)
    st = jnp.concatenate([jnp.zeros((n, QK_NOPE), f32), sin2 * scale, zpad], axis=1)
    kt = jnp.concatenate([cos2, sin2], axis=1)
    return ct, st, kt


def _pair_states(state):
    b = state.shape[0]
    s = jnp.swapaxes(state, -1, -2).reshape(b, N_PAIRS, 2, RWKV_HEAD, RWKV_HEAD)
    z = jnp.zeros_like(s[:, :, 0])
    top = jnp.concatenate([s[:, :, 0], z], axis=-1)
    bot = jnp.concatenate([z, s[:, :, 1]], axis=-1)
    return jnp.concatenate([top, bot], axis=-2)


def _unpair_states(sp):
    b = sp.shape[0]
    h0 = sp[:, :, :RWKV_HEAD, :RWKV_HEAD]
    h1 = sp[:, :, RWKV_HEAD:, RWKV_HEAD:]
    heads = jnp.stack([h0, h1], axis=2).reshape(b, RWKV_HEADS, RWKV_HEAD, RWKV_HEAD)
    return jnp.swapaxes(heads, -1, -2)


def _mix_path(x, mod, pos, cache_lat, cache_rope, wkv_prev, shift_prev, wts, counts0):
    b, s, d = x.shape
    pm, p_rw = _inproj(x, mod, wts["norm_mix"], wts["w_in_bf"])
    ct, st, kt = _rope_tables(pos)
    q, new_lat, new_rope, rope128 = _mla_prep(pm, wts["q_norm"], wts["w_qa"], wts["w_qb"],
                                              wts["kv_norm"], ct, st, kt)
    if cache_lat is None:
        k, v = _kv_proj(new_lat, rope128, wts["w_kn"], wts["w_v"])
        o_mla = _flash_attention(q, k, v)
    else:
        past = cache_lat.shape[1]
        n_keys = past + s
        sk = -(-n_keys // 128) * 128
        lat_all = jnp.concatenate([cache_lat, new_lat, jnp.zeros((b, sk - n_keys, KV_LORA), f32)],
                                  axis=1)
        cache_rope128 = jnp.concatenate([cache_rope, jnp.zeros_like(cache_rope)], axis=-1)
        rope_all = jnp.concatenate([cache_rope128, rope128,
                                    jnp.zeros((b, sk - n_keys, 2 * QK_ROPE), f32)], axis=1)
        k, v = _kv_proj(lat_all, rope_all, wts["w_kn"], wts["w_v"])
        o_mla = _cached_attention(q, k, v, past, n_keys)

    seqs, new_shift = _rwkv_prep(p_rw, shift_prev, wts)
    scan_in, gate_rw, bonus = seqs[:6], seqs[6], seqs[7]
    scan_rows = SCAN_CHUNK * SCAN_STEP_CHUNKS
    s_pad = -(-s // scan_rows) * scan_rows
    if s_pad != s:
        scan_in = tuple(jnp.pad(t, ((0, 0), (0, s_pad - s), (0, 0))) for t in scan_in)
    y_rw, s_fin = _rwkv_scan(scan_in, _pair_states(wkv_prev))
    new_wkv = _unpair_states(s_fin)

    x1, h2, gates, idx, rank, counts = _outproj(o_mla, y_rw[:, :s], bonus, gate_rw, x, mod, wts,
                                                counts0)
    return dict(x1=x1, h2=h2, gates=gates, idx=idx, rank=rank, counts=counts,
                new_lat=new_lat, new_rope=new_rope, new_wkv=new_wkv, new_shift=new_shift)


def kernel(x_prompt, x_sample, cache_kv_latent, cache_k_rope, state_wkv, state_shift, c_prompt, c_sample, w_ada, b_ada, norm_mix, w_in, q_norm, w_uq, kv_norm, w_ukv, mla_out_norm, shift_mu, decay_w0, decay_w2, iclr_a0, iclr_a2, gate_g2, k_k, k_a, r_k, lnx_g, lnx_b, w_out, norm_ffn, w_router, b_router, w_gate_up, b_gate_up, w_down, b_down, norm_final):
    depth = w_ada.shape[0]
    bp, seq_p, d = x_prompt.shape
    bs, seq_s, _ = x_sample.shape
    past = cache_kv_latent.shape[2]
    pos_p = jnp.arange(seq_p, dtype=i32)
    pos_s = past + jnp.arange(seq_s, dtype=i32)
    zero_wkv = jnp.zeros((bp, RWKV_HEADS, RWKV_HEAD, RWKV_HEAD), f32)
    zero_shift = jnp.zeros((bp, RWKV_COLS), f32)
    n_c = bp + bs
    c_rows = -(-n_c // 8) * 8
    c_all = jnp.concatenate([c_prompt, c_sample, jnp.zeros((c_rows - n_c, d), f32)], axis=0)

    hp, hs = x_prompt, x_sample
    outs_p = [[], [], [], []]
    outs_s = [[], [], [], []]
    tm = EXPERT_TILE
    n_pairs = (bp * seq_p + bs * seq_s) * TOP_K
    n_blocks = (n_pairs + N_EXPERTS * (tm - 1) + tm - 1) // tm
    for l in range(depth):
        wts = _layer_weights(w_in[l], q_norm[l], w_uq[l], kv_norm[l], w_ukv[l], mla_out_norm[l],
                             shift_mu[l], decay_w0[l], decay_w2[l], iclr_a0[l], iclr_a2[l],
                             gate_g2[l], k_k[l], k_a[l], r_k[l], lnx_g[l], lnx_b[l], w_out[l],
                             norm_ffn[l], w_router[l], b_router[l])
        wts["norm_mix"] = norm_mix[l]
        mod = _modulation(c_all, w_ada[l], b_ada[l]).reshape(c_rows, 6, d)
        mod_p, mod_s = mod[:bp], mod[bp:n_c]

        rp = _mix_path(hp, mod_p, pos_p, None, None, zero_wkv, zero_shift, wts,
                       jnp.zeros((1, N_EXPERTS), f32))
        rs = _mix_path(hs, mod_s, pos_s, cache_kv_latent[l], cache_k_rope[l], state_wkv[l],
                       state_shift[l], wts, rp["counts"])

        counts = rs["counts"][0].astype(i32)
        padded = (counts + tm - 1) // tm * tm
        pad_end = jnp.cumsum(padded)
        pad_start = pad_end - padded
        dest_p = pad_start[rp["idx"][..., :TOP_K]] + rp["rank"][..., :TOP_K]
        dest_s = pad_start[rs["idx"][..., :TOP_K]] + rs["rank"][..., :TOP_K]
        block_e = jnp.minimum(
            jnp.searchsorted(pad_end, jnp.arange(n_blocks, dtype=i32) * tm, side="right"),
            N_EXPERTS - 1).astype(i32)
        n_used = (pad_end[-1:] // tm).astype(i32)

        xs = jnp.zeros((n_blocks * tm, d), f32)
        xs = _dispatch(rp["h2"], dest_p, xs)
        xs = _dispatch(rs["h2"], dest_s, xs)
        yb = _experts(xs, block_e, n_used, w_gate_up[l], b_gate_up[l], w_down[l], b_down[l])

        last = l == depth - 1
        hp = _combine(rp["x1"], rp["gates"], mod_p, dest_p, yb, norm_final, last)
        hs = _combine(rs["x1"], rs["gates"], mod_s, dest_s, yb, norm_final, last)
        for acc, r in ((outs_p, rp), (outs_s, rs)):
            acc[0].append(r["new_lat"])
            acc[1].append(r["new_rope"])
            acc[2].append(r["new_wkv"])
            acc[3].append(r["new_shift"])

    return (hp, hs, jnp.stack(outs_p[0]), jnp.stack(outs_p[1]), jnp.stack(outs_p[2]),
            jnp.stack(outs_p[3]), jnp.stack(outs_s[0]), jnp.stack(outs_s[1]),
            jnp.stack(outs_s[2]), jnp.stack(outs_s[3]))
```

```python
import functools

import jax
import jax.numpy as jnp
from jax import lax
from jax.experimental import pallas as pl
from jax.experimental.pallas import tpu as pltpu

f32 = jnp.float32
bf16 = jnp.bfloat16
i32 = jnp.int32

D_MODEL = 1024
CHUNK = 64
MLA_HEADS = 4
QK_NOPE = 128
QK_ROPE = 64
V_HEAD = 128
Q_LORA = 256
KV_LORA = 128
ROPE_THETA = 10000.0
MLA_WIDTH = MLA_HEADS * V_HEAD
RWKV_HEAD = 64
RWKV_HEADS = 8
RWKV_WIDTH = RWKV_HEADS * RWKV_HEAD
DECAY_LORA = 64
ICLR_LORA = 64
GATE_LORA = 128
LN_X_EPS = 64e-5
MLA_COLS = Q_LORA + KV_LORA + QK_ROPE
RWKV_COLS = 3 * RWKV_WIDTH + DECAY_LORA + ICLR_LORA + GATE_LORA
N_EXPERTS = 32
TOP_K = 4
D_FF = 1024
SWIGLU_ALPHA = 1.702
SWIGLU_LIMIT = 7.0
NORM_EPS = 1e-6
LOG2_E = 1.4426950408889634

MLA_PAD = 512
QK_PAD = 256
N_PAIRS = RWKV_HEADS // 2
PAIR_W = 2 * RWKV_HEAD
ROUTE_W = 128

TOKEN_TILE = 256
ATTN_TILE = 1024
ATTN_SUB = 512
SCAN_CHUNK = 64
SCAN_STEP_CHUNKS = 2
SCAN_PASSES_GRAM = 1
SCAN_PASSES_INV = 1
SCAN_PASSES_STATE = 3
SCAN_PASSES_OUT = 1
EXPERT_TILE = 512
ROUTE_TILE = 512
COMBINE_PARTS = 4
VMEM_LIMIT = 56 * 1024 * 1024


def _token_tile(s):
    t = TOKEN_TILE
    while s % t:
        t //= 2
    assert t >= 8, s
    return t


def _cparams(sem, vmem=None):
    return pltpu.CompilerParams(dimension_semantics=sem, vmem_limit_bytes=vmem or VMEM_LIMIT)


def _dot(a, b):
    return jnp.dot(a.astype(bf16), b.astype(bf16), preferred_element_type=f32)


def _dg(a, b, ca, cb):
    return lax.dot_general(a, b, (((ca,), (cb,)), ((), ())), preferred_element_type=f32)


def _split2(x):
    hi = x.astype(bf16)
    lo = (x - hi.astype(f32)).astype(bf16)
    return hi, lo


def _split3(x):
    hi = x.astype(bf16)
    r1 = x - hi.astype(f32)
    mid = r1.astype(bf16)
    lo = (r1 - mid.astype(f32)).astype(bf16)
    return hi, mid, lo


def _dot3(a, b, ca=1, cb=0):
    ah, al = _split2(a)
    bh, bl = _split2(b)
    return _dg(ah, bh, ca, cb) + (_dg(ah, bl, ca, cb) + _dg(al, bh, ca, cb))


def _dot_exact_lhs(a_bf, b):
    bh, bm, bl = _split3(b)
    return _dg(a_bf, bh, 1, 0) + (_dg(a_bf, bm, 1, 0) + _dg(a_bf, bl, 1, 0))


def _seg_sum(x, seg_bf):
    xh, xl = _split2(x)
    return _dg(xh, seg_bf, 1, 0) + _dg(xl, seg_bf, 1, 0)


def _rms(x, g):
    return x * lax.rsqrt(jnp.mean(x * x, axis=-1, keepdims=True) + NORM_EPS) * g


def _sigmoid(x):
    return 1.0 / (1.0 + jnp.exp(-x))


def _mod_kernel(c_ref, w_ref, b_ref, o_ref):
    c = c_ref[...]
    o_ref[...] = _dot3(c * _sigmoid(c), w_ref[...]) + b_ref[...]


def _modulation(c_all, w_ada, b_ada):
    rows, d = c_all.shape
    n = w_ada.shape[1]
    tn = 1536
    return pl.pallas_call(
        _mod_kernel,
        name="adaln_mod",
        out_shape=jax.ShapeDtypeStruct((rows, n), f32),
        grid=(n // tn,),
        in_specs=[pl.BlockSpec((rows, d), lambda j: (0, 0)),
                  pl.BlockSpec((d, tn), lambda j: (0, j)),
                  pl.BlockSpec((1, tn), lambda j: (0, j))],
        out_specs=pl.BlockSpec((rows, tn), lambda j: (0, j)),
        compiler_params=_cparams(("arbitrary",)),
    )(c_all, w_ada, b_ada.reshape(1, n))


def _inproj_kernel(x_ref, mod_ref, nw_ref, w_ref, pm_ref, prw_ref):
    h = _rms(x_ref[0], nw_ref[...])
    h = h * (1.0 + mod_ref[0, 1:2, :]) + mod_ref[0, 0:1, :]
    p = jnp.dot(h.astype(bf16), w_ref[...], preferred_element_type=f32)
    pm_ref[0] = p[:, :MLA_PAD]
    prw_ref[0] = p[:, MLA_PAD:]


def _inproj(x, mod, norm_w, w_in_bf):
    b, s, d = x.shape
    ts = _token_tile(s)
    n = w_in_bf.shape[1]
    return pl.pallas_call(
        _inproj_kernel,
        name="inproj",
        out_shape=(jax.ShapeDtypeStruct((b, s, MLA_PAD), f32),
                   jax.ShapeDtypeStruct((b, s, RWKV_COLS), f32)),
        grid=(b, s // ts),
        in_specs=[pl.BlockSpec((1, ts, d), lambda i, j: (i, j, 0)),
                  pl.BlockSpec((1, 6, d), lambda i, j: (i, 0, 0)),
                  pl.BlockSpec((1, d), lambda i, j: (0, 0)),
                  pl.BlockSpec((d, n), lambda i, j: (0, 0))],
        out_specs=(pl.BlockSpec((1, ts, MLA_PAD), lambda i, j: (i, j, 0)),
                   pl.BlockSpec((1, ts, RWKV_COLS), lambda i, j: (i, j, 0))),
        compiler_params=_cparams(("arbitrary", "arbitrary")),
    )(x, mod, norm_w.reshape(1, d), w_in_bf)


def _mla_prep_kernel(pm_ref, qn_ref, wa_ref, wb_ref, kvn_ref, ct_ref, st_ref, kt_ref,
                     q_ref, lat_ref, rope_ref, rope128_ref):
    pm = pm_ref[0]
    qn = _rms(pm[:, :Q_LORA], qn_ref[...]).astype(bf16)
    qa = jnp.dot(qn, wa_ref[...], preferred_element_type=f32)
    qb = jnp.dot(qn, wb_ref[...], preferred_element_type=f32)
    ct = ct_ref[...]
    st = st_ref[...]
    for h in range(MLA_HEADS):
        sl = slice(h * QK_PAD, (h + 1) * QK_PAD)
        q_ref[0, h] = (qa[:, sl] * ct + qb[:, sl] * st).astype(bf16)
    lat_ref[0] = _rms(pm[:, Q_LORA:Q_LORA + KV_LORA], kvn_ref[...])
    t = pm[:, Q_LORA + KV_LORA:] * kt_ref[...]
    kr = t + pltpu.roll(t, QK_ROPE, axis=1)
    lane = lax.broadcasted_iota(i32, kr.shape, 1)
    rope128_ref[0] = jnp.where(lane < QK_ROPE, kr, 0.0)
    rope_ref[0] = kr[:, :QK_ROPE]


def _mla_prep(pm, q_norm, w_qa, w_qb, kv_norm, ct, st, kt):
    b, s, _ = pm.shape
    ts = _token_tile(s)
    hq = MLA_HEADS * QK_PAD
    return pl.pallas_call(
        _mla_prep_kernel,
        name="mla_prep",
        out_shape=(jax.ShapeDtypeStruct((b, MLA_HEADS, s, QK_PAD), bf16),
                   jax.ShapeDtypeStruct((b, s, KV_LORA), f32),
                   jax.ShapeDtypeStruct((b, s, QK_ROPE), f32),
                   jax.ShapeDtypeStruct((b, s, 2 * QK_ROPE), f32)),
        grid=(b, s // ts),
        in_specs=[pl.BlockSpec((1, ts, MLA_PAD), lambda i, j: (i, j, 0)),
                  pl.BlockSpec((1, Q_LORA), lambda i, j: (0, 0)),
                  pl.BlockSpec((Q_LORA, hq), lambda i, j: (0, 0)),
                  pl.BlockSpec((Q_LORA, hq), lambda i, j: (0, 0)),
                  pl.BlockSpec((1, KV_LORA), lambda i, j: (0, 0)),
                  pl.BlockSpec((ts, QK_PAD), lambda i, j: (j, 0)),
                  pl.BlockSpec((ts, QK_PAD), lambda i, j: (j, 0)),
                  pl.BlockSpec((ts, 2 * QK_ROPE), lambda i, j: (j, 0))],
        out_specs=(pl.BlockSpec((1, MLA_HEADS, ts, QK_PAD), lambda i, j: (i, 0, j, 0)),
                   pl.BlockSpec((1, ts, KV_LORA), lambda i, j: (i, j, 0)),
                   pl.BlockSpec((1, ts, QK_ROPE), lambda i, j: (i, j, 0)),
                   pl.BlockSpec((1, ts, 2 * QK_ROPE), lambda i, j: (i, j, 0))),
        compiler_params=_cparams(("arbitrary", "arbitrary")),
    )(pm, q_norm.reshape(1, -1), w_qa, w_qb, kv_norm.reshape(1, -1), ct, st, kt)


def _kv_proj_kernel(lat_ref, rope_ref, wk_ref, wv_ref, k_ref, v_ref):
    lat = lat_ref[0].astype(bf16)
    kn = jnp.dot(lat, wk_ref[...], preferred_element_type=f32)
    vv = jnp.dot(lat, wv_ref[...], preferred_element_type=f32)
    rope = rope_ref[0].astype(bf16)
    for h in range(MLA_HEADS):
        k_ref[0, h, :, :QK_NOPE] = kn[:, h * QK_NOPE:(h + 1) * QK_NOPE].astype(bf16)
        k_ref[0, h, :, QK_NOPE:] = rope
        v_ref[0, h] = vv[:, h * V_HEAD:(h + 1) * V_HEAD].astype(bf16)


def _kv_proj(lat, rope128, w_kn, w_v):
    b, s, _ = lat.shape
    ts = _token_tile(s)
    return pl.pallas_call(
        _kv_proj_kernel,
        name="kv_proj",
        out_shape=(jax.ShapeDtypeStruct((b, MLA_HEADS, s, QK_PAD), bf16),
                   jax.ShapeDtypeStruct((b, MLA_HEADS, s, V_HEAD), bf16)),
        grid=(b, s // ts),
        in_specs=[pl.BlockSpec((1, ts, KV_LORA), lambda i, j: (i, j, 0)),
                  pl.BlockSpec((1, ts, 2 * QK_ROPE), lambda i, j: (i, j, 0)),
                  pl.BlockSpec((KV_LORA, MLA_HEADS * QK_NOPE), lambda i, j: (0, 0)),
                  pl.BlockSpec((KV_LORA, MLA_HEADS * V_HEAD), lambda i, j: (0, 0))],
        out_specs=(pl.BlockSpec((1, MLA_HEADS, ts, QK_PAD), lambda i, j: (i, 0, j, 0)),
                   pl.BlockSpec((1, MLA_HEADS, ts, V_HEAD), lambda i, j: (i, 0, j, 0))),
        compiler_params=_cparams(("arbitrary", "arbitrary")),
    )(lat, rope128, w_kn, w_v)


def _flash_kernel(qi_ref, kj_ref, q_ref, k_ref, v_ref, o_ref, m_scr, l_scr, acc_scr, *, tile, sub):
    step = pl.program_id(1)
    qi = qi_ref[step]
    kj = kj_ref[step]
    lanes = m_scr.shape[1]

    @pl.when(kj == 0)
    def _():
        m_scr[...] = jnp.full(m_scr.shape, -jnp.inf, f32)
        l_scr[...] = jnp.zeros(l_scr.shape, f32)
        acc_scr[...] = jnp.zeros(acc_scr.shape, f32)

    def update(c, r0, masked):
        rows = pl.ds(r0, tile - r0)
        keys = pl.ds(c * sub, sub)
        s = _dg(q_ref[0, 0, rows, :], k_ref[0, 0, keys, :], 1, 1)
        if masked:
            qpos = r0 + lax.broadcasted_iota(i32, s.shape, 0)
            kpos = c * sub + lax.broadcasted_iota(i32, s.shape, 1)
            s = jnp.where(kpos < (qpos // CHUNK + 1) * CHUNK, s, -jnp.inf)
        m_prev = m_scr[rows, :]
        m_new = jnp.maximum(m_prev, jnp.max(s, axis=1, keepdims=True))
        alpha = jnp.exp2(m_prev - m_new)
        p = jnp.exp2(s - jnp.concatenate([m_new] * (sub // lanes), axis=1))
        psum = p[:, :lanes]
        for t in range(1, sub // lanes):
            psum = psum + p[:, t * lanes:(t + 1) * lanes]
        l_scr[rows, :] = alpha * l_scr[rows, :] + psum
        acc_scr[rows, :] = alpha * acc_scr[rows, :] + jnp.dot(
            p.astype(bf16), v_ref[0, 0, keys, :], preferred_element_type=f32)
        m_scr[rows, :] = m_new

    n_sub = tile // sub

    @pl.when(kj < qi)
    def _():
        for c in range(n_sub):
            update(c, 0, False)

    @pl.when(kj == qi)
    def _():
        for c in range(n_sub):
            update(c, c * sub, True)
        o_ref[0] = acc_scr[...] / jnp.sum(l_scr[...], axis=1, keepdims=True)


def _flash_attention(q, k, v):
    b, nh, s, _ = q.shape
    assert b == 1
    tile = min(ATTN_TILE, s)
    sub = min(ATTN_SUB, tile)
    assert sub % CHUNK == 0 and tile % sub == 0
    nt = s // tile
    qi = [i for i in range(nt) for _ in range(i + 1)]
    kj = [j for i in range(nt) for j in range(i + 1)]
    grid_spec = pltpu.PrefetchScalarGridSpec(
        num_scalar_prefetch=2,
        grid=(nh, len(qi)),
        in_specs=[pl.BlockSpec((1, 1, tile, QK_PAD), lambda h, t, qi, kj: (0, h, qi[t], 0)),
                  pl.BlockSpec((1, 1, tile, QK_PAD), lambda h, t, qi, kj: (0, h, kj[t], 0)),
                  pl.BlockSpec((1, 1, tile, V_HEAD), lambda h, t, qi, kj: (0, h, kj[t], 0))],
        out_specs=pl.BlockSpec((1, tile, V_HEAD), lambda h, t, qi, kj: (0, qi[t], h)),
        scratch_shapes=[pltpu.VMEM((tile, V_HEAD), f32), pltpu.VMEM((tile, V_HEAD), f32),
                        pltpu.VMEM((tile, V_HEAD), f32)])
    return pl.pallas_call(
        functools.partial(_flash_kernel, tile=tile, sub=sub),
        name="flash_attn",
        out_shape=jax.ShapeDtypeStruct((1, s, MLA_WIDTH), f32),
        grid_spec=grid_spec,
        compiler_params=_cparams(("arbitrary", "arbitrary")),
    )(jnp.asarray(qi, i32), jnp.asarray(kj, i32), q, k, v)


def _cached_attn_kernel(q_ref, k_ref, v_ref, o_ref, *, past, n_keys):
    for h in range(MLA_HEADS):
        s = _dg(q_ref[0, h], k_ref[0, h], 1, 1)
        qpos = past + lax.broadcasted_iota(i32, s.shape, 0)
        kpos = lax.broadcasted_iota(i32, s.shape, 1)
        limit = (qpos // CHUNK + 1) * CHUNK
        s = jnp.where((kpos < limit) & (kpos < n_keys), s, -jnp.inf)
        m = jnp.max(s, axis=1, keepdims=True)
        p = jnp.exp2(s - m)
        o = jnp.dot(p.astype(bf16), v_ref[0, h], preferred_element_type=f32)
        o_ref[0, :, h * V_HEAD:(h + 1) * V_HEAD] = o / jnp.sum(p, axis=1, keepdims=True)


def _cached_attention(q, k, v, past, n_keys):
    b, nh, sq, _ = q.shape
    sk = k.shape[2]
    return pl.pallas_call(
        functools.partial(_cached_attn_kernel, past=past, n_keys=n_keys),
        name="cached_attn",
        out_shape=jax.ShapeDtypeStruct((b, sq, MLA_WIDTH), f32),
        grid=(b,),
        in_specs=[pl.BlockSpec((1, nh, sq, QK_PAD), lambda i: (i, 0, 0, 0)),
                  pl.BlockSpec((1, nh, sk, QK_PAD), lambda i: (i, 0, 0, 0)),
                  pl.BlockSpec((1, nh, sk, V_HEAD), lambda i: (i, 0, 0, 0))],
        out_specs=pl.BlockSpec((1, sq, MLA_WIDTH), lambda i: (i, 0, 0)),
        compiler_params=_cparams(("arbitrary",)),
    )(q, k, v)


def _rwkv_prep_kernel(p_ref, sp_ref, mu_ref, w0_ref, w2_ref, a0_ref, a2_ref, g2_ref,
                      kk_ref, ka_ref, rk_ref, seg_ref,
                      r_ref, lw_ref, kh_ref, v_ref, al_ref, be_ref, g_ref, bo_ref, ns_ref,
                      carry):
    @pl.when(pl.program_id(1) == 0)
    def _():
        carry[...] = sp_ref[0]

    p = p_ref[0]
    ts = p.shape[0]
    row = lax.broadcasted_iota(i32, p.shape, 0)
    prev = jnp.where(row == 0, carry[...], pltpu.roll(p, 1, axis=0))
    last = p[ts - 1:ts, :]
    carry[...] = last
    ns_ref[0] = last
    xs = p + (prev - p) * mu_ref[...]
    w = RWKV_WIDTH
    r, k, v = xs[:, :w], xs[:, w:2 * w], xs[:, 2 * w:3 * w]
    wa = xs[:, 3 * w:3 * w + DECAY_LORA + ICLR_LORA]
    gd = xs[:, 3 * w + DECAY_LORA + ICLR_LORA:]
    z = -(w0_ref[...] + _dot3(jnp.tanh(wa), w2_ref[...]))
    softplus = jnp.maximum(z, 0.0) + jnp.log(1.0 + jnp.exp(-jnp.abs(z)))
    lw = -jnp.exp(-softplus - 0.5)
    a = _sigmoid(a0_ref[...] + _dot3(wa, a2_ref[...]))
    g = _dot3(_sigmoid(gd), g2_ref[...])
    seg = seg_ref[...]
    kk = k * kk_ref[...]
    kk = kk * lax.rsqrt(jnp.maximum(_seg_sum(kk * kk, seg), 1e-24))
    kh = k * (1.0 + (a - 1.0) * ka_ref[...])
    r_ref[0] = r
    lw_ref[0] = lw
    kh_ref[0] = kh
    v_ref[0] = v
    al_ref[0] = -kk
    be_ref[0] = kk * a
    g_ref[0] = g
    bo_ref[0] = _seg_sum(r * kh * rk_ref[...], seg) * v


def _rwkv_prep(p_rw, shift_prev, wts):
    b, s, n = p_rw.shape
    ts = _token_tile(s)
    w = RWKV_WIDTH
    row = lambda width: pl.BlockSpec((1, width), lambda i, j: (0, 0))
    mat = lambda k: pl.BlockSpec((k, w), lambda i, j: (0, 0))
    tok = pl.BlockSpec((1, ts, w), lambda i, j: (i, j, 0))
    outs = pl.pallas_call(
        _rwkv_prep_kernel,
        name="rwkv_prep",
        out_shape=tuple([jax.ShapeDtypeStruct((b, s, w), f32)] * 8
                        + [jax.ShapeDtypeStruct((b, 1, n), f32)]),
        grid=(b, s // ts),
        in_specs=[pl.BlockSpec((1, ts, n), lambda i, j: (i, j, 0)),
                  pl.BlockSpec((1, 1, n), lambda i, j: (i, 0, 0)),
                  row(n), row(w), mat(PAIR_W), row(w), mat(PAIR_W), mat(GATE_LORA),
                  row(w), row(w), row(w), mat(w)],
        out_specs=tuple([tok] * 8 + [pl.BlockSpec((1, 1, n), lambda i, j: (i, 0, 0))]),
        scratch_shapes=[pltpu.VMEM((1, n), f32)],
        compiler_params=_cparams(("arbitrary", "arbitrary")),
    )(p_rw, shift_prev.reshape(b, 1, n), wts["shift_mu"], wts["decay_w0"], wts["decay_w2p"],
      wts["iclr_a0"], wts["iclr_a2p"], wts["gate_g2"], wts["k_k"], wts["k_a"], wts["r_k"],
      wts["seg"])
    return outs[:8], outs[8].reshape(b, n)


def _mm(a, b, ca, cb, passes):
    if passes == 1:
        return _dg(a.astype(bf16), b.astype(bf16), ca, cb)
    return _dot3(a, b, ca, cb)


def _scan_kernel(r_ref, lw_ref, k_ref, v_ref, al_ref, be_ref, s0_ref, y_ref, sf_ref, st, *, c):
    @pl.when(pl.program_id(1) == 0)
    def _():
        st[...] = s0_ref[0]

    rows = r_ref.shape[1]
    nch = rows // c
    c2 = 2 * c
    rowb = lax.broadcasted_iota(i32, (rows, rows), 0)
    colb = lax.broadcasted_iota(i32, (rows, rows), 1)
    same_chunk = (rowb // c) == (colb // c)

    lw = lw_ref[0]
    cum = _dot_exact_lhs((same_chunk & (colb <= rowb)).astype(bf16), lw)
    tot = _dot_exact_lhs(same_chunk.astype(bf16), lw)
    g_inv = jnp.exp(-cum)
    g_tail = jnp.exp(tot - cum)
    r_t = r_ref[0] * jnp.exp(cum)
    a_t = al_ref[0] * jnp.exp(cum - lw)
    b_t = be_ref[0] * g_inv
    k_t = k_ref[0] * g_inv
    b_c = be_ref[0] * g_tail
    k_c = k_ref[0] * g_tail
    v_all = v_ref[0]

    ri = lax.broadcasted_iota(i32, (c2, c2), 0)
    ci = lax.broadcasted_iota(i32, (c2, c2), 1)
    same = (ri // c) == (ci // c)
    strict = same & (ci < ri)
    incl = same & (ci <= ri)
    eye = (ri == ci).astype(f32)
    head0 = lax.broadcasted_iota(i32, (c, PAIR_W), 1) < RWKV_HEAD
    ones = jnp.ones((c, PAIR_W), bf16)

    def stack(x):
        return jnp.concatenate([jnp.where(head0, x, 0.0), jnp.where(head0, 0.0, x)], axis=0)

    chains = [(j, p) for j in range(nch) for p in range(N_PAIRS)]

    def part(t, j, p):
        return t[j * c:(j + 1) * c, p * PAIR_W:(p + 1) * PAIR_W]

    ops = {}
    for j, p in chains:
        ops[j, p] = {n: stack(part(t, j, p)) for n, t in
                     (("a", a_t), ("r", r_t), ("b", b_t), ("k", k_t), ("v", v_all),
                      ("bc", b_c), ("kc", k_c))}

    gram = {jp: _mm(jnp.concatenate([o["a"], o["r"]], axis=0),
                    jnp.concatenate([o["b"], o["k"]], axis=0), 1, 1, SCAN_PASSES_GRAM)
            for jp, o in ops.items()}
    a_ab = {jp: jnp.where(strict, g[:c2, :c2], 0.0) for jp, g in gram.items()}
    a_ak = {jp: jnp.where(strict, g[:c2, c2:], 0.0) for jp, g in gram.items()}
    a_rb = {jp: jnp.where(incl, g[c2:, :c2], 0.0) for jp, g in gram.items()}
    a_rk = {jp: jnp.where(incl, g[c2:, c2:], 0.0) for jp, g in gram.items()}
    t_inv = {jp: eye + low for jp, low in a_ab.items()}
    pw = dict(a_ab)
    for _ in range(max(c.bit_length() - 2, 0)):
        pw = {jp: _mm(m, m, 1, 0, SCAN_PASSES_INV) for jp, m in pw.items()}
        t_inv = {jp: t_inv[jp] + _mm(t_inv[jp], pw[jp], 1, 0, SCAN_PASSES_INV) for jp in pw}
    decay = {}
    for j, p in chains:
        lh, lm, ll = _split3(part(lw, j, p))
        decay[j, p] = jnp.exp(_dg(lh, ones, 0, 0) + (_dg(lm, ones, 0, 0) + _dg(ll, ones, 0, 0)))

    states = [st[p] for p in range(N_PAIRS)]
    y_rows = []
    prs = range(N_PAIRS)
    for j in range(nch):
        x = [_mm(jnp.concatenate([ops[j, p]["a"], a_ak[j, p]], axis=1),
                 jnp.concatenate([states[p], ops[j, p]["v"]], axis=0), 1, 0, SCAN_PASSES_STATE)
             for p in prs]
        u = [_mm(t_inv[j, p], x[p], 1, 0, SCAN_PASSES_STATE) for p in prs]
        upd = [_mm(jnp.concatenate([ops[j, p]["bc"], ops[j, p]["kc"]], axis=0),
                   jnp.concatenate([u[p], ops[j, p]["v"]], axis=0), 0, 0, SCAN_PASSES_STATE)
               for p in prs]
        y = [_mm(jnp.concatenate([ops[j, p]["r"], a_rb[j, p], a_rk[j, p]], axis=1),
                 jnp.concatenate([states[p], u[p], ops[j, p]["v"]], axis=0), 1, 0, SCAN_PASSES_OUT)
             for p in prs]
        states = [states[p] * decay[j, p] + upd[p] for p in prs]
        y_rows.append(jnp.concatenate([yp[:c] + yp[c:] for yp in y], axis=1))

    y_ref[0] = jnp.concatenate(y_rows, axis=0)
    for p in prs:
        st[p] = states[p]
        sf_ref[0, p] = states[p]


def _rwkv_scan(seqs, s0_pairs):
    r, lw, kh, v, al, be = seqs
    b, s, w = r.shape
    rows = SCAN_CHUNK * SCAN_STEP_CHUNKS
    tok = pl.BlockSpec((1, rows, w), lambda i, j: (i, j, 0))
    stt = pl.BlockSpec((1, N_PAIRS, PAIR_W, PAIR_W), lambda i, j: (i, 0, 0, 0))
    return pl.pallas_call(
        functools.partial(_scan_kernel, c=SCAN_CHUNK),
        name="rwkv_scan",
        out_shape=(jax.ShapeDtypeStruct((b, s, w), f32),
                   jax.ShapeDtypeStruct((b, N_PAIRS, PAIR_W, PAIR_W), f32)),
        grid=(b, s // rows),
        in_specs=[tok] * 6 + [stt],
        out_specs=(tok, stt),
        scratch_shapes=[pltpu.VMEM((N_PAIRS, PAIR_W, PAIR_W), f32)],
        compiler_params=_cparams(("arbitrary", "arbitrary")),
    )(r, lw, kh, v, al, be, s0_pairs)


def _outproj_kernel(om_ref, y_ref, bo_ref, g_ref, x_ref, mod_ref, mn_ref, lng_ref, lnb_ref,
                    seg_ref, wo_ref, nf_ref, wr_ref, br_ref, tri_ref, cnt0_ref,
                    x1_ref, h2_ref, gate_ref, idx_ref, rank_ref, cnt_ref, run):
    @pl.when((pl.program_id(0) == 0) & (pl.program_id(1) == 0))
    def _():
        run[...] = cnt0_ref[...]

    y = y_ref[0]
    seg = seg_ref[...]
    inv_n = 1.0 / RWKV_HEAD
    yc = y - _seg_sum(y, seg) * inv_n
    var = _seg_sum(yc * yc, seg) * inv_n
    o_rw = (yc * lax.rsqrt(var + LN_X_EPS) * lng_ref[...] + lnb_ref[...] + bo_ref[0]) * g_ref[0]

    om = _rms(om_ref[0], mn_ref[...])
    mix = (jnp.dot(om.astype(bf16), wo_ref[:MLA_WIDTH, :], preferred_element_type=f32)
           + jnp.dot(o_rw.astype(bf16), wo_ref[MLA_WIDTH:, :], preferred_element_type=f32))
    x1 = x_ref[0] + mod_ref[0, 2:3, :] * mix
    x1_ref[0] = x1
    h2 = _rms(x1, nf_ref[...]) * (1.0 + mod_ref[0, 4:5, :]) + mod_ref[0, 3:4, :]
    h2_ref[0] = h2

    logits = _dot3(h2, wr_ref[...]) + br_ref[...]
    ts = logits.shape[0]
    lane_e = lax.broadcasted_iota(i32, logits.shape, 1)
    lane_w = lax.broadcasted_iota(i32, (ts, ROUTE_W), 1)
    tri = tri_ref[...]
    base = run[...]
    gates = jnp.zeros((ts, ROUTE_W), f32)
    idxs = jnp.zeros((ts, ROUTE_W), i32)
    ranks = jnp.zeros((ts, ROUTE_W), i32)
    vals = []
    work = logits
    for k in range(TOP_K):
        m = jnp.max(work, axis=1, keepdims=True)
        sel = jnp.min(jnp.where(work == m, lane_e, N_EXPERTS), axis=1, keepdims=True)
        hit = lane_e == sel
        work = jnp.where(hit, -jnp.inf, work)
        vals.append(m)
        onehot = hit.astype(bf16)
        before = jnp.dot(tri, onehot, preferred_element_type=f32)
        rank = jnp.sum(jnp.where(hit, before + base, 0.0), axis=1, keepdims=True)
        base = base + jnp.sum(hit.astype(f32), axis=0, keepdims=True)
        idxs = jnp.where(lane_w == k, sel, idxs)
        ranks = jnp.where(lane_w == k, rank.astype(i32), ranks)
    run[...] = base
    es = [jnp.exp(vk - vals[0]) for vk in vals]
    inv = 1.0 / (es[0] + es[1] + es[2] + es[3])
    for k in range(TOP_K):
        gates = jnp.where(lane_w == k, es[k] * inv, gates)
    gate_ref[0] = gates
    idx_ref[0] = idxs
    rank_ref[0] = ranks
    cnt_ref[...] = base


def _outproj(o_mla, y_rw, bonus, gate_rw, x, mod, wts, counts0):
    b, s, d = x.shape
    w = RWKV_WIDTH
    ts = _token_tile(s)
    tok = lambda width: pl.BlockSpec((1, ts, width), lambda i, j: (i, j, 0))
    full = lambda r, c: pl.BlockSpec((r, c), lambda i, j: (0, 0))
    tri = (lax.broadcasted_iota(i32, (ts, ts), 1) < lax.broadcasted_iota(i32, (ts, ts), 0)
           ).astype(bf16)
    return pl.pallas_call(
        _outproj_kernel,
        name="outproj_route",
        out_shape=(jax.ShapeDtypeStruct((b, s, d), f32),
                   jax.ShapeDtypeStruct((b, s, d), f32),
                   jax.ShapeDtypeStruct((b, s, ROUTE_W), f32),
                   jax.ShapeDtypeStruct((b, s, ROUTE_W), i32),
                   jax.ShapeDtypeStruct((b, s, ROUTE_W), i32),
                   jax.ShapeDtypeStruct((1, N_EXPERTS), f32)),
        grid=(b, s // ts),
        in_specs=[tok(MLA_WIDTH), tok(w), tok(w), tok(w), tok(d),
                  pl.BlockSpec((1, 6, d), lambda i, j: (i, 0, 0)),
                  full(1, MLA_WIDTH), full(1, w), full(1, w), full(w, w),
                  full(d, d), full(1, d), full(d, N_EXPERTS),
                  full(1, N_EXPERTS), full(ts, ts), full(1, N_EXPERTS)],
        out_specs=(tok(d), tok(d), tok(ROUTE_W), tok(ROUTE_W), tok(ROUTE_W),
                   full(1, N_EXPERTS)),
        scratch_shapes=[pltpu.VMEM((1, N_EXPERTS), f32)],
        compiler_params=_cparams(("arbitrary", "arbitrary")),
    )(o_mla, y_rw, bonus, gate_rw, x, mod, wts["mla_out_norm"], wts["lnx_g"], wts["lnx_b"],
      wts["seg"], wts["w_out_bf"], wts["norm_ffn"], wts["w_router"], wts["b_router"], tri, counts0)


def _scatter_rows(dest_ref, h_ref, xs_ref, sem):
    tt = h_ref.shape[1]

    def row_copy(t, d):
        return pltpu.make_async_copy(h_ref.at[0, pl.ds(t, 1), :], xs_ref.at[pl.ds(d, 1), :], sem)

    def issue(t, carry):
        for k in range(TOP_K):
            row_copy(t, dest_ref[0, 0, TOP_K * t + k]).start(priority=k % 2)
        return carry

    def drain(t, carry):
        for k in range(TOP_K):
            row_copy(0, 0).wait()
        return carry

    lax.fori_loop(0, tt, issue, 0)
    lax.fori_loop(0, tt, drain, 0)


def _dispatch_kernel(zrow_ref, dest_ref, h_ref, dest2_ref, h2_ref, xs_ref, zeros, sem, zsem):
    tm = zeros.shape[0]

    @pl.when(pl.program_id(0) == 0)
    def _():
        zeros[...] = jnp.zeros(zeros.shape, f32)

        def tile_copy(row):
            row = pl.multiple_of(row, tm)
            return pltpu.make_async_copy(zeros, xs_ref.at[pl.ds(row, tm), :], zsem)

        def issue(e, carry):
            @pl.when(zrow_ref[e] >= 0)
            def _():
                tile_copy(zrow_ref[e]).start()
            return carry

        def drain(e, carry):
            @pl.when(zrow_ref[e] >= 0)
            def _():
                tile_copy(0).wait()
            return carry

        lax.fori_loop(0, N_EXPERTS, issue, 0)
        lax.fori_loop(0, N_EXPERTS, drain, 0)

        n_used = zrow_ref[N_EXPERTS]
        n_tiles = xs_ref.shape[0] // tm

        def issue_tail(i, carry):
            tile_copy(i * tm).start()
            return carry

        def drain_tail(i, carry):
            tile_copy(0).wait()
            return carry

        lax.fori_loop(n_used, n_tiles, issue_tail, 0)
        lax.fori_loop(n_used, n_tiles, drain_tail, 0)

    _scatter_rows(dest_ref, h_ref, xs_ref, sem)

    @pl.when(pl.program_id(0) == pl.num_programs(0) - 1)
    def _():
        _scatter_rows(dest2_ref, h2_ref, xs_ref, sem)


def _route_tile(s):
    t = ROUTE_TILE
    while s % t:
        t //= 2
    assert t >= 8, s
    return t


def _dispatch(h_big, dest_big, h_small, dest_small, zrow, n_rows):
    d = h_big.shape[-1]
    h_big = h_big.reshape(1, -1, d)
    h_small = h_small.reshape(1, -1, d)
    s, s2 = h_big.shape[1], h_small.shape[1]
    tt = _route_tile(s)
    nt = s // tt
    grid_spec = pltpu.PrefetchScalarGridSpec(
        num_scalar_prefetch=1,
        grid=(nt,),
        in_specs=[pl.BlockSpec((1, 1, tt * TOP_K), lambda j, z: (j, 0, 0),
                               memory_space=pltpu.SMEM),
                  pl.BlockSpec((1, tt, d), lambda j, z: (0, j, 0)),
                  pl.BlockSpec((1, 1, s2 * TOP_K), lambda j, z: (0, 0, 0),
                               memory_space=pltpu.SMEM),
                  pl.BlockSpec((1, s2, d), lambda j, z: (0, 0, 0))],
        out_specs=pl.BlockSpec(memory_space=pl.ANY),
        scratch_shapes=[pltpu.VMEM((EXPERT_TILE, d), f32), pltpu.SemaphoreType.DMA(()),
                        pltpu.SemaphoreType.DMA(())])
    return pl.pallas_call(
        _dispatch_kernel,
        name="moe_dispatch",
        out_shape=jax.ShapeDtypeStruct((n_rows, d), f32),
        grid_spec=grid_spec,
        compiler_params=_cparams(("arbitrary",)),
    )(zrow, dest_big.reshape(nt, 1, tt * TOP_K), h_big, dest_small.reshape(1, 1, s2 * TOP_K),
      h_small)


def _expert_kernel(be_ref, nu_ref, x_ref, wgu_ref, bgu_ref, wd_ref, bd_ref, y_ref,
                   wgu_bf, wd_bf):
    i = pl.program_id(0)
    prev = be_ref[jnp.maximum(i - 1, 0)]
    fresh = (i == 0) | (be_ref[i] != prev)

    @pl.when(fresh & (i < nu_ref[0]))
    def _():
        wgu_bf[...] = wgu_ref[0].astype(bf16)
        wd_bf[...] = wd_ref[0].astype(bf16)

    @pl.when(i < nu_ref[0])
    def _():
        gu = jnp.dot(x_ref[...].astype(bf16), wgu_bf[...], preferred_element_type=f32) + bgu_ref[0]
        gate = jnp.minimum(gu[:, :D_FF], SWIGLU_LIMIT)
        up = jnp.clip(gu[:, D_FF:], -SWIGLU_LIMIT, SWIGLU_LIMIT)
        act = (up + 1.0) * (gate * _sigmoid(SWIGLU_ALPHA * gate))
        y_ref[...] = jnp.dot(act.astype(bf16), wd_bf[...], preferred_element_type=f32) + bd_ref[0]

    @pl.when(i >= nu_ref[0])
    def _():
        y_ref[...] = jnp.zeros(y_ref.shape, f32)


def _experts(xs, block_e, n_used, w_gate_up, b_gate_up, w_down, b_down):
    n_rows, d = xs.shape
    tm = EXPERT_TILE
    nb = n_rows // tm
    last = lambda i, be, nu: jnp.minimum(i, nu[0] - 1)
    grid_spec = pltpu.PrefetchScalarGridSpec(
        num_scalar_prefetch=2,
        grid=(nb,),
        in_specs=[pl.BlockSpec((tm, d), lambda i, be, nu: (last(i, be, nu), 0)),
                  pl.BlockSpec((1, d, 2 * D_FF), lambda i, be, nu: (be[last(i, be, nu)], 0, 0)),
                  pl.BlockSpec((1, 1, 2 * D_FF), lambda i, be, nu: (be[last(i, be, nu)], 0, 0)),
                  pl.BlockSpec((1, D_FF, d), lambda i, be, nu: (be[last(i, be, nu)], 0, 0)),
                  pl.BlockSpec((1, 1, d), lambda i, be, nu: (be[last(i, be, nu)], 0, 0))],
        out_specs=pl.BlockSpec((tm, d), lambda i, be, nu: (i, 0)),
        scratch_shapes=[pltpu.VMEM((d, 2 * D_FF), bf16), pltpu.VMEM((D_FF, d), bf16)])
    return pl.pallas_call(
        _expert_kernel,
        name="moe_experts",
        out_shape=jax.ShapeDtypeStruct((n_rows, d), f32),
        grid_spec=grid_spec,
        compiler_params=_cparams(("arbitrary",)),
    )(block_e, n_used, xs, w_gate_up, b_gate_up.reshape(N_EXPERTS, 1, -1), w_down,
      b_down.reshape(N_EXPERTS, 1, -1))


def _combine_kernel(dest_ref, x1_ref, gate_ref, mod_ref, nf_ref, yb_ref, o_ref, buf, sems, *,
                    final, parts):
    tt = x1_ref.shape[1]
    pt = tt // parts

    def row_copy(t, k, d, q):
        return pltpu.make_async_copy(yb_ref.at[pl.ds(d, 1), :], buf.at[k, pl.ds(t, 1), :],
                                     sems.at[q])

    for q in range(parts):
        def issue(t, carry, q=q):
            for k in range(TOP_K):
                row_copy(t, k, dest_ref[0, 0, TOP_K * t + k], q).start(priority=k % 2)
            return carry
        lax.fori_loop(q * pt, (q + 1) * pt, issue, 0)

    for q in range(parts):
        def drain(t, carry, q=q):
            for k in range(TOP_K):
                row_copy(0, 0, 0, q).wait()
            return carry
        lax.fori_loop(0, pt, drain, 0)
        rows = pl.ds(q * pt, pt)
        gates = gate_ref[0, rows, :]
        y = gates[:, 0:1] * buf[0, rows, :]
        for k in range(1, TOP_K):
            y = y + gates[:, k:k + 1] * buf[k, rows, :]
        x = x1_ref[0, rows, :] + mod_ref[0, 5:6, :] * y
        o_ref[0, rows, :] = _rms(x, nf_ref[...]) if final else x


def _combine(x1, gates, mod, dest, yb, norm_final, final):
    b, s, d = x1.shape
    tt = _route_tile(s)
    nt = s // tt
    parts = min(COMBINE_PARTS, tt // 8)
    dest_t = dest.reshape(b * nt, 1, tt * TOP_K)
    return pl.pallas_call(
        functools.partial(_combine_kernel, final=final, parts=parts),
        name="moe_combine",
        out_shape=jax.ShapeDtypeStruct((b, s, d), f32),
        grid=(b, nt),
        in_specs=[pl.BlockSpec((1, 1, tt * TOP_K), lambda i, j: (i * nt + j, 0, 0),
                               memory_space=pltpu.SMEM),
                  pl.BlockSpec((1, tt, d), lambda i, j: (i, j, 0)),
                  pl.BlockSpec((1, tt, ROUTE_W), lambda i, j: (i, j, 0)),
                  pl.BlockSpec((1, 6, d), lambda i, j: (i, 0, 0)),
                  pl.BlockSpec((1, d), lambda i, j: (0, 0)),
                  pl.BlockSpec(memory_space=pl.ANY)],
        out_specs=pl.BlockSpec((1, tt, d), lambda i, j: (i, j, 0)),
        scratch_shapes=[pltpu.VMEM((TOP_K, tt, d), f32), pltpu.SemaphoreType.DMA((parts,))],
        compiler_params=_cparams(("arbitrary", "arbitrary")),
    )(dest_t, x1, gates, mod, norm_final.reshape(1, d), yb)


def _rot_half_cols(w):
    half = w.shape[-1] // 2
    return jnp.concatenate([-w[..., half:], w[..., :half]], axis=-1)


def _layer_weights(w_in, q_norm, w_uq, kv_norm, w_ukv, mla_out_norm, shift_mu, decay_w0, decay_w2,
                   iclr_a0, iclr_a2, gate_g2, k_k, k_a, r_k, lnx_g, lnx_b, w_out, norm_ffn,
                   w_router, b_router):
    d = w_in.shape[0]
    kr_cols = w_in[:, Q_LORA + KV_LORA:MLA_COLS]
    w_in_ext = jnp.concatenate([w_in[:, :MLA_COLS], _rot_half_cols(kr_cols), w_in[:, MLA_COLS:]],
                               axis=1)
    uq = w_uq.reshape(Q_LORA, MLA_HEADS, QK_NOPE + QK_ROPE)
    zq = jnp.zeros((Q_LORA, MLA_HEADS, QK_PAD - QK_NOPE - QK_ROPE), f32)
    w_qa = jnp.concatenate([uq, zq], axis=2).reshape(Q_LORA, MLA_HEADS * QK_PAD)
    w_qb = jnp.concatenate([jnp.zeros_like(uq[..., :QK_NOPE]), _rot_half_cols(uq[..., QK_NOPE:]), zq],
                           axis=2).reshape(Q_LORA, MLA_HEADS * QK_PAD)
    ukv = w_ukv.reshape(KV_LORA, MLA_HEADS, QK_NOPE + V_HEAD)
    zl = jnp.zeros((DECAY_LORA, RWKV_WIDTH), f32)
    hid = jnp.arange(RWKV_WIDTH) // RWKV_HEAD
    return dict(
        w_in_bf=w_in_ext.astype(bf16),
        q_norm=q_norm, kv_norm=kv_norm,
        w_qa=w_qa.astype(bf16), w_qb=w_qb.astype(bf16),
        w_kn=ukv[..., :QK_NOPE].reshape(KV_LORA, -1).astype(bf16),
        w_v=ukv[..., QK_NOPE:].reshape(KV_LORA, -1).astype(bf16),
        mla_out_norm=mla_out_norm.reshape(1, -1),
        shift_mu=shift_mu.reshape(1, -1), decay_w0=decay_w0.reshape(1, -1),
        decay_w2p=jnp.concatenate([decay_w2, zl], axis=0),
        iclr_a0=iclr_a0.reshape(1, -1),
        iclr_a2p=jnp.concatenate([zl, iclr_a2], axis=0),
        gate_g2=gate_g2, k_k=k_k.reshape(1, -1), k_a=k_a.reshape(1, -1), r_k=r_k.reshape(1, -1),
        lnx_g=lnx_g.reshape(1, -1), lnx_b=lnx_b.reshape(1, -1),
        seg=(hid[:, None] == hid[None, :]).astype(bf16),
        w_out_bf=w_out.astype(bf16), norm_ffn=norm_ffn.reshape(1, d),
        w_router=w_router, b_router=b_router.reshape(1, -1),
    )


def _rope_tables(pos):
    inv = ROPE_THETA ** (-jnp.arange(0, QK_ROPE, 2, dtype=f32) / QK_ROPE)
    ang = pos.astype(f32)[:, None] * inv[None, :]
    cos, sin = jnp.cos(ang), jnp.sin(ang)
    cos2 = jnp.concatenate([cos, cos], axis=1)
    sin2 = jnp.concatenate([sin, sin], axis=1)
    scale = (QK_NOPE + QK_ROPE) ** -0.5 * LOG2_E
    n = pos.shape[0]
    zpad = jnp.zeros((n, QK_PAD - QK_NOPE - QK_ROPE), f32)
    ct = jnp.concatenate([jnp.full((n, QK_NOPE), scale, f32), cos2 * scale, zpad], axis=1)
    st = jnp.concatenate([jnp.zeros((n, QK_NOPE), f32), sin2 * scale, zpad], axis=1)
    kt = jnp.concatenate([cos2, sin2], axis=1)
    return ct, st, kt


def _pair_states(state):
    b = state.shape[0]
    s = jnp.swapaxes(state, -1, -2).reshape(b, N_PAIRS, 2, RWKV_HEAD, RWKV_HEAD)
    z = jnp.zeros_like(s[:, :, 0])
    top = jnp.concatenate([s[:, :, 0], z], axis=-1)
    bot = jnp.concatenate([z, s[:, :, 1]], axis=-1)
    return jnp.concatenate([top, bot], axis=-2)


def _unpair_states(sp):
    b = sp.shape[0]
    h0 = sp[:, :, :RWKV_HEAD, :RWKV_HEAD]
    h1 = sp[:, :, RWKV_HEAD:, RWKV_HEAD:]
    heads = jnp.stack([h0, h1], axis=2).reshape(b, RWKV_HEADS, RWKV_HEAD, RWKV_HEAD)
    return jnp.swapaxes(heads, -1, -2)


def _mix_path(x, mod, pos, cache_lat, cache_rope, wkv_prev, shift_prev, wts, counts0):
    b, s, d = x.shape
    pm, p_rw = _inproj(x, mod, wts["norm_mix"], wts["w_in_bf"])
    ct, st, kt = _rope_tables(pos)
    q, new_lat, new_rope, rope128 = _mla_prep(pm, wts["q_norm"], wts["w_qa"], wts["w_qb"],
                                              wts["kv_norm"], ct, st, kt)
    if cache_lat is None:
        k, v = _kv_proj(new_lat, rope128, wts["w_kn"], wts["w_v"])
        o_mla = _flash_attention(q, k, v)
    else:
        past = cache_lat.shape[1]
        n_keys = past + s
        sk = -(-n_keys // 128) * 128
        lat_all = jnp.concatenate([cache_lat, new_lat, jnp.zeros((b, sk - n_keys, KV_LORA), f32)],
                                  axis=1)
        cache_rope128 = jnp.concatenate([cache_rope, jnp.zeros_like(cache_rope)], axis=-1)
        rope_all = jnp.concatenate([cache_rope128, rope128,
                                    jnp.zeros((b, sk - n_keys, 2 * QK_ROPE), f32)], axis=1)
        k, v = _kv_proj(lat_all, rope_all, wts["w_kn"], wts["w_v"])
        o_mla = _cached_attention(q, k, v, past, n_keys)

    seqs, new_shift = _rwkv_prep(p_rw, shift_prev, wts)
    scan_in, gate_rw, bonus = seqs[:6], seqs[6], seqs[7]
    scan_rows = SCAN_CHUNK * SCAN_STEP_CHUNKS
    s_pad = -(-s // scan_rows) * scan_rows
    if s_pad != s:
        scan_in = tuple(jnp.pad(t, ((0, 0), (0, s_pad - s), (0, 0))) for t in scan_in)
    y_rw, s_fin = _rwkv_scan(scan_in, _pair_states(wkv_prev))
    new_wkv = _unpair_states(s_fin)

    x1, h2, gates, idx, rank, counts = _outproj(o_mla, y_rw[:, :s], bonus, gate_rw, x, mod, wts,
                                                counts0)
    return dict(x1=x1, h2=h2, gates=gates, idx=idx, rank=rank, counts=counts,
                new_lat=new_lat, new_rope=new_rope, new_wkv=new_wkv, new_shift=new_shift)


def kernel(x_prompt, x_sample, cache_kv_latent, cache_k_rope, state_wkv, state_shift, c_prompt, c_sample, w_ada, b_ada, norm_mix, w_in, q_norm, w_uq, kv_norm, w_ukv, mla_out_norm, shift_mu, decay_w0, decay_w2, iclr_a0, iclr_a2, gate_g2, k_k, k_a, r_k, lnx_g, lnx_b, w_out, norm_ffn, w_router, b_router, w_gate_up, b_gate_up, w_down, b_down, norm_final):
    depth = w_ada.shape[0]
    bp, seq_p, d = x_prompt.shape
    bs, seq_s, _ = x_sample.shape
    past = cache_kv_latent.shape[2]
    pos_p = jnp.arange(seq_p, dtype=i32)
    pos_s = past + jnp.arange(seq_s, dtype=i32)
    zero_wkv = jnp.zeros((bp, RWKV_HEADS, RWKV_HEAD, RWKV_HEAD), f32)
    zero_shift = jnp.zeros((bp, RWKV_COLS), f32)
    n_c = bp + bs
    c_rows = -(-n_c // 8) * 8
    c_all = jnp.concatenate([c_prompt, c_sample, jnp.zeros((c_rows - n_c, d), f32)], axis=0)

    hp, hs = x_prompt, x_sample
    outs_p = [[], [], [], []]
    outs_s = [[], [], [], []]
    tm = EXPERT_TILE
    n_pairs = (bp * seq_p + bs * seq_s) * TOP_K
    n_blocks = (n_pairs + N_EXPERTS * (tm - 1) + tm - 1) // tm
    for l in range(depth):
        wts = _layer_weights(w_in[l], q_norm[l], w_uq[l], kv_norm[l], w_ukv[l], mla_out_norm[l],
                             shift_mu[l], decay_w0[l], decay_w2[l], iclr_a0[l], iclr_a2[l],
                             gate_g2[l], k_k[l], k_a[l], r_k[l], lnx_g[l], lnx_b[l], w_out[l],
                             norm_ffn[l], w_router[l], b_router[l])
        wts["norm_mix"] = norm_mix[l]
        mod = _modulation(c_all, w_ada[l], b_ada[l]).reshape(c_rows, 6, d)
        mod_p, mod_s = mod[:bp], mod[bp:n_c]

        rp = _mix_path(hp, mod_p, pos_p, None, None, zero_wkv, zero_shift, wts,
                       jnp.zeros((1, N_EXPERTS), f32))
        rs = _mix_path(hs, mod_s, pos_s, cache_kv_latent[l], cache_k_rope[l], state_wkv[l],
                       state_shift[l], wts, rp["counts"])

        counts = rs["counts"][0].astype(i32)
        padded = (counts + tm - 1) // tm * tm
        pad_end = jnp.cumsum(padded)
        pad_start = pad_end - padded
        experts = jnp.arange(N_EXPERTS, dtype=i32)

        def slots(r):
            idx = r["idx"][..., :TOP_K]
            start = jnp.sum(jnp.where(idx[..., None] == experts, pad_start, 0), axis=-1)
            return start + r["rank"][..., :TOP_K]

        dest_p, dest_s = slots(rp), slots(rs)
        tile_row = jnp.arange(n_blocks, dtype=i32) * tm
        block_e = jnp.minimum(jnp.sum((pad_end[None, :] <= tile_row[:, None]).astype(i32), axis=1),
                              N_EXPERTS - 1)
        n_used = (pad_end[-1:] // tm).astype(i32)
        zrow = jnp.concatenate([jnp.where(padded > 0, pad_end - tm, -1), n_used]).astype(i32)

        xs = _dispatch(rp["h2"], dest_p, rs["h2"], dest_s, zrow, n_blocks * tm)
        yb = _experts(xs, block_e, n_used, w_gate_up[l], b_gate_up[l], w_down[l], b_down[l])

        last = l == depth - 1
        hp = _combine(rp["x1"], rp["gates"], mod_p, dest_p, yb, norm_final, last)
        hs = _combine(rs["x1"], rs["gates"], mod_s, dest_s, yb, norm_final, last)
        for acc, r in ((outs_p, rp), (outs_s, rs)):
            acc[0].append(r["new_lat"])
            acc[1].append(r["new_rope"])
            acc[2].append(r["new_wkv"])
            acc[3].append(r["new_shift"])

    return (hp, hs, jnp.stack(outs_p[0]), jnp.stack(outs_p[1]), jnp.stack(outs_p[2]),
            jnp.stack(outs_p[3]), jnp.stack(outs_s[0]), jnp.stack(outs_s[1]),
            jnp.stack(outs_s[2]), jnp.stack(outs_s[3]))
```

```python
import functools

import jax
import jax.numpy as jnp
from jax import lax
from jax.experimental import pallas as pl
from jax.experimental.pallas import tpu as pltpu

f32 = jnp.float32
bf16 = jnp.bfloat16
i32 = jnp.int32

D_MODEL = 1024
CHUNK = 64
MLA_HEADS = 4
QK_NOPE = 128
QK_ROPE = 64
V_HEAD = 128
Q_LORA = 256
KV_LORA = 128
ROPE_THETA = 10000.0
MLA_WIDTH = MLA_HEADS * V_HEAD
RWKV_HEAD = 64
RWKV_HEADS = 8
RWKV_WIDTH = RWKV_HEADS * RWKV_HEAD
DECAY_LORA = 64
ICLR_LORA = 64
GATE_LORA = 128
LN_X_EPS = 64e-5
MLA_COLS = Q_LORA + KV_LORA + QK_ROPE
RWKV_COLS = 3 * RWKV_WIDTH + DECAY_LORA + ICLR_LORA + GATE_LORA
N_EXPERTS = 32
TOP_K = 4
D_FF = 1024
SWIGLU_ALPHA = 1.702
SWIGLU_LIMIT = 7.0
NORM_EPS = 1e-6
LOG2_E = 1.4426950408889634

MLA_PAD = 512
QK_PAD = 256
N_PAIRS = RWKV_HEADS // 2
PAIR_W = 2 * RWKV_HEAD
ROUTE_W = 128
ROW_TILE = (8, 128)

TOKEN_TILE = 256
KV_WHOLE_MAX = 2048
ATTN_TILE = 2048
ATTN_SUB = 512
SCAN_CHUNK = 64
SCAN_STEP_CHUNKS = 4
SCAN_PASSES_GRAM = 1
SCAN_PASSES_INV = 1
SCAN_PASSES_STATE = 1
SCAN_PASSES_OUT = 1
EXPERT_TILE = 512
ROUTE_TILE = 512
COMBINE_PARTS = 4
VMEM_LIMIT = 56 * 1024 * 1024


def _token_tile(s):
    t = TOKEN_TILE
    while s % t:
        t //= 2
    assert t >= 8, s
    return t


def _cparams(sem, vmem=None):
    return pltpu.CompilerParams(dimension_semantics=sem, vmem_limit_bytes=vmem or VMEM_LIMIT)


def _dot(a, b):
    return jnp.dot(a.astype(bf16), b.astype(bf16), preferred_element_type=f32)


def _dg(a, b, ca, cb):
    return lax.dot_general(a, b, (((ca,), (cb,)), ((), ())), preferred_element_type=f32)


def _split2(x):
    hi = x.astype(bf16)
    lo = (x - hi.astype(f32)).astype(bf16)
    return hi, lo


def _split3(x):
    hi = x.astype(bf16)
    r1 = x - hi.astype(f32)
    mid = r1.astype(bf16)
    lo = (r1 - mid.astype(f32)).astype(bf16)
    return hi, mid, lo


def _dot3(a, b, ca=1, cb=0):
    ah, al = _split2(a)
    bh, bl = _split2(b)
    return _dg(ah, bh, ca, cb) + (_dg(ah, bl, ca, cb) + _dg(al, bh, ca, cb))


def _dot_exact_lhs(a_bf, b):
    bh, bm, bl = _split3(b)
    return _dg(a_bf, bh, 1, 0) + (_dg(a_bf, bm, 1, 0) + _dg(a_bf, bl, 1, 0))


def _seg_sum(x, seg_bf):
    xh, xl = _split2(x)
    return _dg(xh, seg_bf, 1, 0) + _dg(xl, seg_bf, 1, 0)


def _rms(x, g):
    return x * lax.rsqrt(jnp.mean(x * x, axis=-1, keepdims=True) + NORM_EPS) * g


def _sigmoid(x):
    return 1.0 / (1.0 + jnp.exp(-x))


def _mod_kernel(c_ref, w_ref, b_ref, o_ref):
    c = c_ref[...]
    o_ref[...] = _dot3(c * _sigmoid(c), w_ref[...]) + b_ref[...]


def _modulation(c_all, w_ada, b_ada):
    rows, d = c_all.shape
    n = w_ada.shape[1]
    tn = 1536
    return pl.pallas_call(
        _mod_kernel,
        name="adaln_mod",
        out_shape=jax.ShapeDtypeStruct((rows, n), f32),
        grid=(n // tn,),
        in_specs=[pl.BlockSpec((rows, d), lambda j: (0, 0)),
                  pl.BlockSpec((d, tn), lambda j: (0, j)),
                  pl.BlockSpec((1, tn), lambda j: (0, j))],
        out_specs=pl.BlockSpec((rows, tn), lambda j: (0, j)),
        compiler_params=_cparams(("arbitrary",)),
    )(c_all, w_ada, b_ada.reshape(1, n))


def _inproj_kernel(x_ref, mod_ref, nw_ref, w_ref, pm_ref, prw_ref):
    h = _rms(x_ref[0], nw_ref[...])
    h = h * (1.0 + mod_ref[0, 1:2, :]) + mod_ref[0, 0:1, :]
    p = jnp.dot(h.astype(bf16), w_ref[...], preferred_element_type=f32)
    pm_ref[0] = p[:, :MLA_PAD]
    prw_ref[0] = p[:, MLA_PAD:]


def _inproj(x, mod, norm_w, w_in_bf):
    b, s, d = x.shape
    ts = _token_tile(s)
    n = w_in_bf.shape[1]
    return pl.pallas_call(
        _inproj_kernel,
        name="inproj",
        out_shape=(jax.ShapeDtypeStruct((b, s, MLA_PAD), f32),
                   jax.ShapeDtypeStruct((b, s, RWKV_COLS), f32)),
        grid=(b, s // ts),
        in_specs=[pl.BlockSpec((1, ts, d), lambda i, j: (i, j, 0)),
                  pl.BlockSpec((1, 6, d), lambda i, j: (i, 0, 0)),
                  pl.BlockSpec((1, d), lambda i, j: (0, 0)),
                  pl.BlockSpec((d, n), lambda i, j: (0, 0))],
        out_specs=(pl.BlockSpec((1, ts, MLA_PAD), lambda i, j: (i, j, 0)),
                   pl.BlockSpec((1, ts, RWKV_COLS), lambda i, j: (i, j, 0))),
        compiler_params=_cparams(("arbitrary", "arbitrary")),
    )(x, mod, norm_w.reshape(1, d), w_in_bf)


def _mla_prep_kernel(pm_ref, qn_ref, wa_ref, wb_ref, kvn_ref, ct_ref, st_ref, kt_ref,
                     q_ref, lat_ref, rope_ref, rope128_ref):
    pm = pm_ref[0]
    qn = _rms(pm[:, :Q_LORA], qn_ref[...]).astype(bf16)
    qa = jnp.dot(qn, wa_ref[...], preferred_element_type=f32)
    qb = jnp.dot(qn, wb_ref[...], preferred_element_type=f32)
    ct = ct_ref[...]
    st = st_ref[...]
    for h in range(MLA_HEADS):
        sl = slice(h * QK_PAD, (h + 1) * QK_PAD)
        q_ref[0, h] = (qa[:, sl] * ct + qb[:, sl] * st).astype(bf16)
    lat_ref[0] = _rms(pm[:, Q_LORA:Q_LORA + KV_LORA], kvn_ref[...])
    t = pm[:, Q_LORA + KV_LORA:] * kt_ref[...]
    kr = t + pltpu.roll(t, QK_ROPE, axis=1)
    lane = lax.broadcasted_iota(i32, kr.shape, 1)
    rope128_ref[0] = jnp.where(lane < QK_ROPE, kr, 0.0)
    rope_ref[0] = kr[:, :QK_ROPE]


def _mla_prep(pm, q_norm, w_qa, w_qb, kv_norm, ct, st, kt):
    b, s, _ = pm.shape
    ts = _token_tile(s)
    hq = MLA_HEADS * QK_PAD
    return pl.pallas_call(
        _mla_prep_kernel,
        name="mla_prep",
        out_shape=(jax.ShapeDtypeStruct((b, MLA_HEADS, s, QK_PAD), bf16),
                   jax.ShapeDtypeStruct((b, s, KV_LORA), f32),
                   jax.ShapeDtypeStruct((b, s, QK_ROPE), f32),
                   jax.ShapeDtypeStruct((b, s, 2 * QK_ROPE), f32)),
        grid=(b, s // ts),
        in_specs=[pl.BlockSpec((1, ts, MLA_PAD), lambda i, j: (i, j, 0)),
                  pl.BlockSpec((1, Q_LORA), lambda i, j: (0, 0)),
                  pl.BlockSpec((Q_LORA, hq), lambda i, j: (0, 0)),
                  pl.BlockSpec((Q_LORA, hq), lambda i, j: (0, 0)),
                  pl.BlockSpec((1, KV_LORA), lambda i, j: (0, 0)),
                  pl.BlockSpec((ts, QK_PAD), lambda i, j: (j, 0)),
                  pl.BlockSpec((ts, QK_PAD), lambda i, j: (j, 0)),
                  pl.BlockSpec((ts, 2 * QK_ROPE), lambda i, j: (j, 0))],
        out_specs=(pl.BlockSpec((1, MLA_HEADS, ts, QK_PAD), lambda i, j: (i, 0, j, 0)),
                   pl.BlockSpec((1, ts, KV_LORA), lambda i, j: (i, j, 0)),
                   pl.BlockSpec((1, ts, QK_ROPE), lambda i, j: (i, j, 0)),
                   pl.BlockSpec((1, ts, 2 * QK_ROPE), lambda i, j: (i, j, 0))),
        compiler_params=_cparams(("arbitrary", "arbitrary")),
    )(pm, q_norm.reshape(1, -1), w_qa, w_qb, kv_norm.reshape(1, -1), ct, st, kt)


def _kv_proj_kernel(lat_ref, rope_ref, wk_ref, wv_ref, k_ref, v_ref):
    lat = lat_ref[0].astype(bf16)
    kn = jnp.dot(lat, wk_ref[...], preferred_element_type=f32)
    vv = jnp.dot(lat, wv_ref[...], preferred_element_type=f32)
    rope = rope_ref[0].astype(bf16)
    for h in range(MLA_HEADS):
        k_ref[0, h, :, :QK_NOPE] = kn[:, h * QK_NOPE:(h + 1) * QK_NOPE].astype(bf16)
        k_ref[0, h, :, QK_NOPE:] = rope
        v_ref[0, h] = vv[:, h * V_HEAD:(h + 1) * V_HEAD].astype(bf16)


def _kv_proj(lat, rope128, w_kn, w_v):
    b, s, _ = lat.shape
    ts = s if s <= KV_WHOLE_MAX else _token_tile(s)
    return pl.pallas_call(
        _kv_proj_kernel,
        name="kv_proj",
        out_shape=(jax.ShapeDtypeStruct((b, MLA_HEADS, s, QK_PAD), bf16),
                   jax.ShapeDtypeStruct((b, MLA_HEADS, s, V_HEAD), bf16)),
        grid=(b, s // ts),
        in_specs=[pl.BlockSpec((1, ts, KV_LORA), lambda i, j: (i, j, 0)),
                  pl.BlockSpec((1, ts, 2 * QK_ROPE), lambda i, j: (i, j, 0)),
                  pl.BlockSpec((KV_LORA, MLA_HEADS * QK_NOPE), lambda i, j: (0, 0)),
                  pl.BlockSpec((KV_LORA, MLA_HEADS * V_HEAD), lambda i, j: (0, 0))],
        out_specs=(pl.BlockSpec((1, MLA_HEADS, ts, QK_PAD), lambda i, j: (i, 0, j, 0)),
                   pl.BlockSpec((1, MLA_HEADS, ts, V_HEAD), lambda i, j: (i, 0, j, 0))),
        compiler_params=_cparams(("arbitrary", "arbitrary")),
    )(lat, rope128, w_kn, w_v)


def _flash_kernel(qi_ref, kj_ref, q_ref, k_ref, v_ref, o_ref, m_scr, l_scr, acc_scr, *, tile, sub):
    step = pl.program_id(1)
    qi = qi_ref[step]
    kj = kj_ref[step]
    lanes = m_scr.shape[1]

    @pl.when(kj == 0)
    def _():
        m_scr[...] = jnp.full(m_scr.shape, -jnp.inf, f32)
        l_scr[...] = jnp.zeros(l_scr.shape, f32)
        acc_scr[...] = jnp.zeros(acc_scr.shape, f32)

    def update(c, r0, masked):
        rows = pl.ds(r0, tile - r0)
        keys = pl.ds(c * sub, sub)
        s = _dg(q_ref[0, 0, rows, :], k_ref[0, 0, keys, :], 1, 1)
        if masked:
            qpos = r0 + lax.broadcasted_iota(i32, s.shape, 0)
            kpos = c * sub + lax.broadcasted_iota(i32, s.shape, 1)
            s = jnp.where(kpos < (qpos // CHUNK + 1) * CHUNK, s, -jnp.inf)
        m_prev = m_scr[rows, :]
        m_new = jnp.maximum(m_prev, jnp.max(s, axis=1, keepdims=True))
        alpha = jnp.exp2(m_prev - m_new)
        p = jnp.exp2(s - jnp.concatenate([m_new] * (sub // lanes), axis=1))
        psum = p[:, :lanes]
        for t in range(1, sub // lanes):
            psum = psum + p[:, t * lanes:(t + 1) * lanes]
        l_scr[rows, :] = alpha * l_scr[rows, :] + psum
        acc_scr[rows, :] = alpha * acc_scr[rows, :] + jnp.dot(
            p.astype(bf16), v_ref[0, 0, keys, :], preferred_element_type=f32)
        m_scr[rows, :] = m_new

    n_sub = tile // sub

    @pl.when(kj < qi)
    def _():
        for c in range(n_sub):
            update(c, 0, False)

    @pl.when(kj == qi)
    def _():
        for c in range(n_sub):
            update(c, c * sub, True)
        o_ref[0] = acc_scr[...] / jnp.sum(l_scr[...], axis=1, keepdims=True)


def _flash_attention(q, k, v):
    b, nh, s, _ = q.shape
    assert b == 1
    tile = min(ATTN_TILE, s)
    sub = min(ATTN_SUB, tile)
    assert sub % CHUNK == 0 and tile % sub == 0
    nt = s // tile
    qi = [i for i in range(nt) for _ in range(i + 1)]
    kj = [j for i in range(nt) for j in range(i + 1)]
    grid_spec = pltpu.PrefetchScalarGridSpec(
        num_scalar_prefetch=2,
        grid=(nh, len(qi)),
        in_specs=[pl.BlockSpec((1, 1, tile, QK_PAD), lambda h, t, qi, kj: (0, h, qi[t], 0)),
                  pl.BlockSpec((1, 1, tile, QK_PAD), lambda h, t, qi, kj: (0, h, kj[t], 0)),
                  pl.BlockSpec((1, 1, tile, V_HEAD), lambda h, t, qi, kj: (0, h, kj[t], 0))],
        out_specs=pl.BlockSpec((1, tile, V_HEAD), lambda h, t, qi, kj: (0, qi[t], h)),
        scratch_shapes=[pltpu.VMEM((tile, V_HEAD), f32), pltpu.VMEM((tile, V_HEAD), f32),
                        pltpu.VMEM((tile, V_HEAD), f32)])
    return pl.pallas_call(
        functools.partial(_flash_kernel, tile=tile, sub=sub),
        name="flash_attn",
        out_shape=jax.ShapeDtypeStruct((1, s, MLA_WIDTH), f32),
        grid_spec=grid_spec,
        compiler_params=_cparams(("arbitrary", "arbitrary")),
    )(jnp.asarray(qi, i32), jnp.asarray(kj, i32), q, k, v)


def _cached_attn_kernel(q_ref, k_ref, v_ref, o_ref, *, past, n_keys):
    for h in range(MLA_HEADS):
        s = _dg(q_ref[0, h], k_ref[0, h], 1, 1)
        qpos = past + lax.broadcasted_iota(i32, s.shape, 0)
        kpos = lax.broadcasted_iota(i32, s.shape, 1)
        limit = (qpos // CHUNK + 1) * CHUNK
        s = jnp.where((kpos < limit) & (kpos < n_keys), s, -jnp.inf)
        m = jnp.max(s, axis=1, keepdims=True)
        p = jnp.exp2(s - m)
        o = jnp.dot(p.astype(bf16), v_ref[0, h], preferred_element_type=f32)
        o_ref[0, :, h * V_HEAD:(h + 1) * V_HEAD] = o / jnp.sum(p, axis=1, keepdims=True)


def _cached_attention(q, k, v, past, n_keys):
    b, nh, sq, _ = q.shape
    sk = k.shape[2]
    return pl.pallas_call(
        functools.partial(_cached_attn_kernel, past=past, n_keys=n_keys),
        name="cached_attn",
        out_shape=jax.ShapeDtypeStruct((b, sq, MLA_WIDTH), f32),
        grid=(b,),
        in_specs=[pl.BlockSpec((1, nh, sq, QK_PAD), lambda i: (i, 0, 0, 0)),
                  pl.BlockSpec((1, nh, sk, QK_PAD), lambda i: (i, 0, 0, 0)),
                  pl.BlockSpec((1, nh, sk, V_HEAD), lambda i: (i, 0, 0, 0))],
        out_specs=pl.BlockSpec((1, sq, MLA_WIDTH), lambda i: (i, 0, 0)),
        compiler_params=_cparams(("arbitrary",)),
    )(q, k, v)


def _rwkv_prep_kernel(p_ref, sp_ref, mu_ref, w0_ref, w2_ref, a0_ref, a2_ref, g2_ref,
                      kk_ref, ka_ref, rk_ref, seg_ref,
                      r_ref, lw_ref, kh_ref, v_ref, al_ref, be_ref, g_ref, bo_ref, ns_ref,
                      carry):
    @pl.when(pl.program_id(1) == 0)
    def _():
        carry[...] = sp_ref[0]

    p = p_ref[0]
    ts = p.shape[0]
    row = lax.broadcasted_iota(i32, p.shape, 0)
    prev = jnp.where(row == 0, carry[...], pltpu.roll(p, 1, axis=0))
    last = p[ts - 1:ts, :]
    carry[...] = last
    ns_ref[0] = last
    xs = p + (prev - p) * mu_ref[...]
    w = RWKV_WIDTH
    r, k, v = xs[:, :w], xs[:, w:2 * w], xs[:, 2 * w:3 * w]
    wa = xs[:, 3 * w:3 * w + DECAY_LORA + ICLR_LORA]
    gd = xs[:, 3 * w + DECAY_LORA + ICLR_LORA:]
    z = -(w0_ref[...] + _dot3(jnp.tanh(wa), w2_ref[...]))
    softplus = jnp.maximum(z, 0.0) + jnp.log(1.0 + jnp.exp(-jnp.abs(z)))
    lw = -jnp.exp(-softplus - 0.5)
    a = _sigmoid(a0_ref[...] + _dot3(wa, a2_ref[...]))
    g = _dot3(_sigmoid(gd), g2_ref[...])
    seg = seg_ref[...]
    kk = k * kk_ref[...]
    kk = kk * lax.rsqrt(jnp.maximum(_seg_sum(kk * kk, seg), 1e-24))
    kh = k * (1.0 + (a - 1.0) * ka_ref[...])
    r_ref[0] = r
    lw_ref[0] = lw
    kh_ref[0] = kh
    v_ref[0] = v
    al_ref[0] = -kk
    be_ref[0] = kk * a
    g_ref[0] = g
    bo_ref[0] = _seg_sum(r * kh * rk_ref[...], seg) * v


def _rwkv_prep(p_rw, shift_prev, wts):
    b, s, n = p_rw.shape
    ts = _token_tile(s)
    w = RWKV_WIDTH
    row = lambda width: pl.BlockSpec((1, width), lambda i, j: (0, 0))
    mat = lambda k: pl.BlockSpec((k, w), lambda i, j: (0, 0))
    tok = pl.BlockSpec((1, ts, w), lambda i, j: (i, j, 0))
    outs = pl.pallas_call(
        _rwkv_prep_kernel,
        name="rwkv_prep",
        out_shape=tuple([jax.ShapeDtypeStruct((b, s, w), f32)] * 8
                        + [jax.ShapeDtypeStruct((b, 1, n), f32)]),
        grid=(b, s // ts),
        in_specs=[pl.BlockSpec((1, ts, n), lambda i, j: (i, j, 0)),
                  pl.BlockSpec((1, 1, n), lambda i, j: (i, 0, 0)),
                  row(n), row(w), mat(PAIR_W), row(w), mat(PAIR_W), mat(GATE_LORA),
                  row(w), row(w), row(w), mat(w)],
        out_specs=tuple([tok] * 8 + [pl.BlockSpec((1, 1, n), lambda i, j: (i, 0, 0))]),
        scratch_shapes=[pltpu.VMEM((1, n), f32)],
        compiler_params=_cparams(("arbitrary", "arbitrary")),
    )(p_rw, shift_prev.reshape(b, 1, n), wts["shift_mu"], wts["decay_w0"], wts["decay_w2p"],
      wts["iclr_a0"], wts["iclr_a2p"], wts["gate_g2"], wts["k_k"], wts["k_a"], wts["r_k"],
      wts["seg"])
    return outs[:8], outs[8].reshape(b, n)


def _mm(a, b, ca, cb, passes):
    if passes == 1:
        return _dg(a.astype(bf16), b.astype(bf16), ca, cb)
    return _dot3(a, b, ca, cb)


def _scan_kernel(r_ref, lw_ref, k_ref, v_ref, al_ref, be_ref, s0_ref, y_ref, sf_ref, st, *, c):
    @pl.when(pl.program_id(1) == 0)
    def _():
        st[...] = s0_ref[0]

    rows = r_ref.shape[1]
    nch = rows // c
    c2 = 2 * c
    rowb = lax.broadcasted_iota(i32, (rows, rows), 0)
    colb = lax.broadcasted_iota(i32, (rows, rows), 1)
    same_chunk = (rowb // c) == (colb // c)

    lw = lw_ref[0]
    cum = _dot_exact_lhs((same_chunk & (colb <= rowb)).astype(bf16), lw)
    tot = _dot_exact_lhs(same_chunk.astype(bf16), lw)
    g_inv = jnp.exp(-cum)
    g_tail = jnp.exp(tot - cum)
    r_t = r_ref[0] * jnp.exp(cum)
    a_t = al_ref[0] * jnp.exp(cum - lw)
    b_t = be_ref[0] * g_inv
    k_t = k_ref[0] * g_inv
    b_c = be_ref[0] * g_tail
    k_c = k_ref[0] * g_tail
    v_all = v_ref[0]

    ri = lax.broadcasted_iota(i32, (c2, c2), 0)
    ci = lax.broadcasted_iota(i32, (c2, c2), 1)
    same = (ri // c) == (ci // c)
    strict = same & (ci < ri)
    incl = same & (ci <= ri)
    eye = (ri == ci).astype(f32)
    head0 = lax.broadcasted_iota(i32, (c, PAIR_W), 1) < RWKV_HEAD
    ones = jnp.ones((c, PAIR_W), bf16)

    def stack(x):
        return jnp.concatenate([jnp.where(head0, x, 0.0), jnp.where(head0, 0.0, x)], axis=0)

    chains = [(j, p) for j in range(nch) for p in range(N_PAIRS)]

    def part(t, j, p):
        return t[j * c:(j + 1) * c, p * PAIR_W:(p + 1) * PAIR_W]

    ops = {}
    for j, p in chains:
        ops[j, p] = {n: stack(part(t, j, p)) for n, t in
                     (("a", a_t), ("r", r_t), ("b", b_t), ("k", k_t), ("v", v_all),
                      ("bc", b_c), ("kc", k_c))}

    gram = {jp: _mm(jnp.concatenate([o["a"], o["r"]], axis=0),
                    jnp.concatenate([o["b"], o["k"]], axis=0), 1, 1, SCAN_PASSES_GRAM)
            for jp, o in ops.items()}
    a_ab = {jp: jnp.where(strict, g[:c2, :c2], 0.0) for jp, g in gram.items()}
    a_ak = {jp: jnp.where(strict, g[:c2, c2:], 0.0) for jp, g in gram.items()}
    a_rb = {jp: jnp.where(incl, g[c2:, :c2], 0.0) for jp, g in gram.items()}
    a_rk = {jp: jnp.where(incl, g[c2:, c2:], 0.0) for jp, g in gram.items()}
    t_inv = {jp: eye + low for jp, low in a_ab.items()}
    pw = dict(a_ab)
    for _ in range(max(c.bit_length() - 2, 0)):
        pw = {jp: _mm(m, m, 1, 0, SCAN_PASSES_INV) for jp, m in pw.items()}
        t_inv = {jp: t_inv[jp] + _mm(t_inv[jp], pw[jp], 1, 0, SCAN_PASSES_INV) for jp in pw}
    decay = {}
    for j, p in chains:
        lh, lm, ll = _split3(part(lw, j, p))
        decay[j, p] = jnp.exp(_dg(lh, ones, 0, 0) + (_dg(lm, ones, 0, 0) + _dg(ll, ones, 0, 0)))

    states = [st[p] for p in range(N_PAIRS)]
    y_rows = []
    prs = range(N_PAIRS)
    for j in range(nch):
        x = [_mm(jnp.concatenate([ops[j, p]["a"], a_ak[j, p]], axis=1),
                 jnp.concatenate([states[p], ops[j, p]["v"]], axis=0), 1, 0, SCAN_PASSES_STATE)
             for p in prs]
        u = [_mm(t_inv[j, p], x[p], 1, 0, SCAN_PASSES_STATE) for p in prs]
        upd = [_mm(jnp.concatenate([ops[j, p]["bc"], ops[j, p]["kc"]], axis=0),
                   jnp.concatenate([u[p], ops[j, p]["v"]], axis=0), 0, 0, SCAN_PASSES_STATE)
               for p in prs]
        y = [_mm(jnp.concatenate([ops[j, p]["r"], a_rb[j, p], a_rk[j, p]], axis=1),
                 jnp.concatenate([states[p], u[p], ops[j, p]["v"]], axis=0), 1, 0, SCAN_PASSES_OUT)
             for p in prs]
        states = [states[p] * decay[j, p] + upd[p] for p in prs]
        y_rows.append(jnp.concatenate([yp[:c] + yp[c:] for yp in y], axis=1))

    y_ref[0] = jnp.concatenate(y_rows, axis=0)
    for p in prs:
        st[p] = states[p]
        sf_ref[0, p] = states[p]


def _scan_rows(s):
    return SCAN_CHUNK * min(SCAN_STEP_CHUNKS, -(-s // SCAN_CHUNK))


def _rwkv_scan(seqs, s0_pairs):
    r, lw, kh, v, al, be = seqs
    b, s, w = r.shape
    rows = _scan_rows(s)
    tok = pl.BlockSpec((1, rows, w), lambda i, j: (i, j, 0))
    stt = pl.BlockSpec((1, N_PAIRS, PAIR_W, PAIR_W), lambda i, j: (i, 0, 0, 0))
    return pl.pallas_call(
        functools.partial(_scan_kernel, c=SCAN_CHUNK),
        name="rwkv_scan",
        out_shape=(jax.ShapeDtypeStruct((b, s, w), f32),
                   jax.ShapeDtypeStruct((b, N_PAIRS, PAIR_W, PAIR_W), f32)),
        grid=(b, s // rows),
        in_specs=[tok] * 6 + [stt],
        out_specs=(tok, stt),
        scratch_shapes=[pltpu.VMEM((N_PAIRS, PAIR_W, PAIR_W), f32)],
        compiler_params=_cparams(("arbitrary", "arbitrary")),
    )(r, lw, kh, v, al, be, s0_pairs)


def _outproj_kernel(om_ref, y_ref, bo_ref, g_ref, x_ref, mod_ref, mn_ref, lng_ref, lnb_ref,
                    seg_ref, wo_ref, nf_ref, wr_ref, br_ref, tri_ref, cnt0_ref,
                    x1_ref, h2_ref, gate_ref, idx_ref, rank_ref, cnt_ref, run):
    @pl.when((pl.program_id(0) == 0) & (pl.program_id(1) == 0))
    def _():
        run[...] = cnt0_ref[...]

    y = y_ref[0]
    seg = seg_ref[...]
    inv_n = 1.0 / RWKV_HEAD
    yc = y - _seg_sum(y, seg) * inv_n
    var = _seg_sum(yc * yc, seg) * inv_n
    o_rw = (yc * lax.rsqrt(var + LN_X_EPS) * lng_ref[...] + lnb_ref[...] + bo_ref[0]) * g_ref[0]

    om = _rms(om_ref[0], mn_ref[...])
    mix = (jnp.dot(om.astype(bf16), wo_ref[:MLA_WIDTH, :], preferred_element_type=f32)
           + jnp.dot(o_rw.astype(bf16), wo_ref[MLA_WIDTH:, :], preferred_element_type=f32))
    x1 = x_ref[0] + mod_ref[0, 2:3, :] * mix
    x1_ref[0] = x1
    h2 = _rms(x1, nf_ref[...]) * (1.0 + mod_ref[0, 4:5, :]) + mod_ref[0, 3:4, :]
    h2_ref[0] = h2

    logits = _dot3(h2, wr_ref[...]) + br_ref[...]
    ts = logits.shape[0]
    lane_e = lax.broadcasted_iota(i32, logits.shape, 1)
    lane_w = lax.broadcasted_iota(i32, (ts, ROUTE_W), 1)
    tri = tri_ref[...]
    base = run[...]
    gates = jnp.zeros((ts, ROUTE_W), f32)
    idxs = jnp.zeros((ts, ROUTE_W), i32)
    ranks = jnp.zeros((ts, ROUTE_W), i32)
    vals = []
    work = logits
    for k in range(TOP_K):
        m = jnp.max(work, axis=1, keepdims=True)
        sel = jnp.min(jnp.where(work == m, lane_e, N_EXPERTS), axis=1, keepdims=True)
        hit = lane_e == sel
        work = jnp.where(hit, -jnp.inf, work)
        vals.append(m)
        onehot = hit.astype(bf16)
        before = jnp.dot(tri, onehot, preferred_element_type=f32)
        rank = jnp.sum(jnp.where(hit, before + base, 0.0), axis=1, keepdims=True)
        base = base + jnp.sum(hit.astype(f32), axis=0, keepdims=True)
        idxs = jnp.where(lane_w == k, sel, idxs)
        ranks = jnp.where(lane_w == k, rank.astype(i32), ranks)
    run[...] = base
    es = [jnp.exp(vk - vals[0]) for vk in vals]
    inv = 1.0 / (es[0] + es[1] + es[2] + es[3])
    for k in range(TOP_K):
        gates = jnp.where(lane_w == k, es[k] * inv, gates)
    gate_ref[0] = gates
    idx_ref[0] = idxs
    rank_ref[0] = ranks
    cnt_ref[...] = base


def _outproj(o_mla, y_rw, bonus, gate_rw, x, mod, wts, counts0):
    b, s, d = x.shape
    w = RWKV_WIDTH
    ts = _token_tile(s)
    tok = lambda width: pl.BlockSpec((1, ts, width), lambda i, j: (i, j, 0))
    full = lambda r, c: pl.BlockSpec((r, c), lambda i, j: (0, 0))
    tri = (lax.broadcasted_iota(i32, (ts, ts), 1) < lax.broadcasted_iota(i32, (ts, ts), 0)
           ).astype(bf16)
    return pl.pallas_call(
        _outproj_kernel,
        name="outproj_route",
        out_shape=(jax.ShapeDtypeStruct((b, s, d), f32),
                   jax.ShapeDtypeStruct((b, s, d), f32),
                   jax.ShapeDtypeStruct((b, s, ROUTE_W), f32),
                   jax.ShapeDtypeStruct((b, s, ROUTE_W), i32),
                   jax.ShapeDtypeStruct((b, s, ROUTE_W), i32),
                   jax.ShapeDtypeStruct((1, N_EXPERTS), f32)),
        grid=(b, s // ts),
        in_specs=[tok(MLA_WIDTH), tok(w), tok(w), tok(w), tok(d),
                  pl.BlockSpec((1, 6, d), lambda i, j: (i, 0, 0)),
                  full(1, MLA_WIDTH), full(1, w), full(1, w), full(w, w),
                  full(d, d), full(1, d), full(d, N_EXPERTS),
                  full(1, N_EXPERTS), full(ts, ts), full(1, N_EXPERTS)],
        out_specs=(tok(d), tok(d), tok(ROUTE_W), tok(ROUTE_W), tok(ROUTE_W),
                   full(1, N_EXPERTS)),
        scratch_shapes=[pltpu.VMEM((1, N_EXPERTS), f32)],
        compiler_params=_cparams(("arbitrary", "arbitrary")),
    )(o_mla, y_rw, bonus, gate_rw, x, mod, wts["mla_out_norm"], wts["lnx_g"], wts["lnx_b"],
      wts["seg"], wts["w_out_bf"], wts["norm_ffn"], wts["w_router"], wts["b_router"], tri, counts0)


def _to_row_tiles(tiles_ref, x, first=0):
    sub, lanes = ROW_TILE
    for c in range(sub):
        tiles_ref[pl.ds(first * sub + c, x.shape[0], stride=sub), :] = x[:, c * lanes:(c + 1) * lanes]


def _from_row_tiles(tiles_ref, n, first=0):
    sub = ROW_TILE[0]
    return jnp.concatenate([tiles_ref[pl.ds(first * sub + c, n, stride=sub), :]
                            for c in range(sub)], axis=1)


def _row_tile(ref, t):
    sub = ROW_TILE[0]
    return ref.at[pl.ds(pl.multiple_of(t * sub, sub), sub), :]


def _scatter_rows(dest_ref, h_ref, hb, xs_ref, sem):
    tt = h_ref.shape[1]
    _to_row_tiles(hb, h_ref[0])

    def row_copy(t, d):
        return pltpu.make_async_copy(_row_tile(hb, t), _row_tile(xs_ref, d), sem)

    def issue(t, carry):
        for k in range(TOP_K):
            row_copy(t, dest_ref[0, 0, TOP_K * t + k]).start(priority=k % 2)
        return carry

    def drain(t, carry):
        for k in range(TOP_K):
            row_copy(0, 0).wait()
        return carry

    lax.fori_loop(0, tt, issue, 0)
    lax.fori_loop(0, tt, drain, 0)


def _dispatch_kernel(zrow_ref, dest_ref, h_ref, dest2_ref, h2_ref, xs_ref, zeros, hb, hb2, sem,
                     zsem):
    sub = ROW_TILE[0]
    tm = zeros.shape[0] // sub

    @pl.when(pl.program_id(0) == 0)
    def _():
        zeros[...] = jnp.zeros(zeros.shape, f32)

        def tile_copy(row):
            first = pl.multiple_of(row * sub, tm * sub)
            return pltpu.make_async_copy(zeros, xs_ref.at[pl.ds(first, tm * sub), :], zsem)

        def issue(e, carry):
            @pl.when(zrow_ref[e] >= 0)
            def _():
                tile_copy(zrow_ref[e]).start()
            return carry

        def drain(e, carry):
            @pl.when(zrow_ref[e] >= 0)
            def _():
                tile_copy(0).wait()
            return carry

        lax.fori_loop(0, N_EXPERTS, issue, 0)
        lax.fori_loop(0, N_EXPERTS, drain, 0)

        n_used = zrow_ref[N_EXPERTS]
        n_tiles = xs_ref.shape[0] // (tm * sub)

        def issue_tail(i, carry):
            tile_copy(i * tm).start()
            return carry

        def drain_tail(i, carry):
            tile_copy(0).wait()
            return carry

        lax.fori_loop(n_used, n_tiles, issue_tail, 0)
        lax.fori_loop(n_used, n_tiles, drain_tail, 0)

    _scatter_rows(dest_ref, h_ref, hb, xs_ref, sem)

    @pl.when(pl.program_id(0) == pl.num_programs(0) - 1)
    def _():
        _scatter_rows(dest2_ref, h2_ref, hb2, xs_ref, sem)


def _route_tile(s):
    t = ROUTE_TILE
    while s % t:
        t //= 2
    assert t >= 8, s
    return t


def _dispatch(h_big, dest_big, h_small, dest_small, zrow, n_rows):
    d = h_big.shape[-1]
    sub, lanes = ROW_TILE
    assert d == sub * lanes
    h_big = h_big.reshape(1, -1, d)
    h_small = h_small.reshape(1, -1, d)
    s, s2 = h_big.shape[1], h_small.shape[1]
    tt = _route_tile(s)
    nt = s // tt
    grid_spec = pltpu.PrefetchScalarGridSpec(
        num_scalar_prefetch=1,
        grid=(nt,),
        in_specs=[pl.BlockSpec((1, 1, tt * TOP_K), lambda j, z: (j, 0, 0),
                               memory_space=pltpu.SMEM),
                  pl.BlockSpec((1, tt, d), lambda j, z: (0, j, 0)),
                  pl.BlockSpec((1, 1, s2 * TOP_K), lambda j, z: (0, 0, 0),
                               memory_space=pltpu.SMEM),
                  pl.BlockSpec((1, s2, d), lambda j, z: (0, 0, 0))],
        out_specs=pl.BlockSpec(memory_space=pl.ANY),
        scratch_shapes=[pltpu.VMEM((EXPERT_TILE * sub, lanes), f32),
                        pltpu.VMEM((tt * sub, lanes), f32), pltpu.VMEM((s2 * sub, lanes), f32),
                        pltpu.SemaphoreType.DMA(()), pltpu.SemaphoreType.DMA(())])
    return pl.pallas_call(
        _dispatch_kernel,
        name="moe_dispatch",
        out_shape=jax.ShapeDtypeStruct((n_rows * sub, lanes), f32),
        grid_spec=grid_spec,
        compiler_params=_cparams(("arbitrary",)),
    )(zrow, dest_big.reshape(nt, 1, tt * TOP_K), h_big, dest_small.reshape(1, 1, s2 * TOP_K),
      h_small)


def _expert_kernel(be_ref, nu_ref, x_ref, wgu_ref, bgu_ref, wd_ref, bd_ref, y_ref,
                   wgu_bf, wd_bf):
    i = pl.program_id(0)
    prev = be_ref[jnp.maximum(i - 1, 0)]
    fresh = (i == 0) | (be_ref[i] != prev)

    @pl.when(fresh & (i < nu_ref[0]))
    def _():
        wgu_bf[...] = wgu_ref[0].astype(bf16)
        wd_bf[...] = wd_ref[0].astype(bf16)

    @pl.when(i < nu_ref[0])
    def _():
        x = _from_row_tiles(x_ref, x_ref.shape[0] // ROW_TILE[0]).astype(bf16)
        gu = jnp.dot(x, wgu_bf[...], preferred_element_type=f32) + bgu_ref[0]
        gate = jnp.minimum(gu[:, :D_FF], SWIGLU_LIMIT)
        up = jnp.clip(gu[:, D_FF:], -SWIGLU_LIMIT, SWIGLU_LIMIT)
        act = (up + 1.0) * (gate * _sigmoid(SWIGLU_ALPHA * gate))
        _to_row_tiles(y_ref, jnp.dot(act.astype(bf16), wd_bf[...], preferred_element_type=f32)
                      + bd_ref[0])

    @pl.when(i >= nu_ref[0])
    def _():
        y_ref[...] = jnp.zeros(y_ref.shape, f32)


def _experts(xs, block_e, n_used, w_gate_up, b_gate_up, w_down, b_down):
    sub, lanes = ROW_TILE
    n_rows = xs.shape[0] // sub
    d = w_down.shape[-1]
    tm = EXPERT_TILE
    nb = n_rows // tm
    last = lambda i, be, nu: jnp.minimum(i, nu[0] - 1)
    grid_spec = pltpu.PrefetchScalarGridSpec(
        num_scalar_prefetch=2,
        grid=(nb,),
        in_specs=[pl.BlockSpec((tm * sub, lanes), lambda i, be, nu: (last(i, be, nu), 0)),
                  pl.BlockSpec((1, d, 2 * D_FF), lambda i, be, nu: (be[last(i, be, nu)], 0, 0)),
                  pl.BlockSpec((1, 1, 2 * D_FF), lambda i, be, nu: (be[last(i, be, nu)], 0, 0)),
                  pl.BlockSpec((1, D_FF, d), lambda i, be, nu: (be[last(i, be, nu)], 0, 0)),
                  pl.BlockSpec((1, 1, d), lambda i, be, nu: (be[last(i, be, nu)], 0, 0))],
        out_specs=pl.BlockSpec((tm * sub, lanes), lambda i, be, nu: (i, 0)),
        scratch_shapes=[pltpu.VMEM((d, 2 * D_FF), bf16), pltpu.VMEM((D_FF, d), bf16)])
    return pl.pallas_call(
        _expert_kernel,
        name="moe_experts",
        out_shape=jax.ShapeDtypeStruct((n_rows * sub, lanes), f32),
        grid_spec=grid_spec,
        compiler_params=_cparams(("arbitrary",)),
    )(block_e, n_used, xs, w_gate_up, b_gate_up.reshape(N_EXPERTS, 1, -1), w_down,
      b_down.reshape(N_EXPERTS, 1, -1))


def _combine_kernel(dest_ref, x1_ref, gate_ref, mod_ref, nf_ref, yb_ref, o_ref, buf, sems, *,
                    final, parts):
    tt = x1_ref.shape[1]
    pt = tt // parts

    def row_copy(t, k, d, q):
        return pltpu.make_async_copy(_row_tile(yb_ref, d), _row_tile(buf, k * tt + t), sems.at[q])

    for q in range(parts):
        def issue(t, carry, q=q):
            for k in range(TOP_K):
                row_copy(t, k, dest_ref[0, 0, TOP_K * t + k], q).start(priority=k % 2)
            return carry
        lax.fori_loop(q * pt, (q + 1) * pt, issue, 0)

    for q in range(parts):
        def drain(t, carry, q=q):
            for k in range(TOP_K):
                row_copy(0, 0, 0, q).wait()
            return carry
        lax.fori_loop(0, pt, drain, 0)
        rows = pl.ds(q * pt, pt)
        gates = gate_ref[0, rows, :]
        y = gates[:, 0:1] * _from_row_tiles(buf, pt, q * pt)
        for k in range(1, TOP_K):
            y = y + gates[:, k:k + 1] * _from_row_tiles(buf, pt, k * tt + q * pt)
        x = x1_ref[0, rows, :] + mod_ref[0, 5:6, :] * y
        o_ref[0, rows, :] = _rms(x, nf_ref[...]) if final else x


def _combine(x1, gates, mod, dest, yb, norm_final, final):
    b, s, d = x1.shape
    tt = _route_tile(s)
    nt = s // tt
    parts = min(COMBINE_PARTS, tt // 8)
    dest_t = dest.reshape(b * nt, 1, tt * TOP_K)
    return pl.pallas_call(
        functools.partial(_combine_kernel, final=final, parts=parts),
        name="moe_combine",
        out_shape=jax.ShapeDtypeStruct((b, s, d), f32),
        grid=(b, nt),
        in_specs=[pl.BlockSpec((1, 1, tt * TOP_K), lambda i, j: (i * nt + j, 0, 0),
                               memory_space=pltpu.SMEM),
                  pl.BlockSpec((1, tt, d), lambda i, j: (i, j, 0)),
                  pl.BlockSpec((1, tt, ROUTE_W), lambda i, j: (i, j, 0)),
                  pl.BlockSpec((1, 6, d), lambda i, j: (i, 0, 0)),
                  pl.BlockSpec((1, d), lambda i, j: (0, 0)),
                  pl.BlockSpec(memory_space=pl.ANY)],
        out_specs=pl.BlockSpec((1, tt, d), lambda i, j: (i, j, 0)),
        scratch_shapes=[pltpu.VMEM((TOP_K * tt * ROW_TILE[0], ROW_TILE[1]), f32),
                        pltpu.SemaphoreType.DMA((parts,))],
        compiler_params=_cparams(("arbitrary", "arbitrary")),
    )(dest_t, x1, gates, mod, norm_final.reshape(1, d), yb)


def _rot_half_cols(w):
    half = w.shape[-1] // 2
    return jnp.concatenate([-w[..., half:], w[..., :half]], axis=-1)


def _layer_weights(w_in, q_norm, w_uq, kv_norm, w_ukv, mla_out_norm, shift_mu, decay_w0, decay_w2,
                   iclr_a0, iclr_a2, gate_g2, k_k, k_a, r_k, lnx_g, lnx_b, w_out, norm_ffn,
                   w_router, b_router):
    d = w_in.shape[0]
    kr_cols = w_in[:, Q_LORA + KV_LORA:MLA_COLS]
    w_in_ext = jnp.concatenate([w_in[:, :MLA_COLS], _rot_half_cols(kr_cols), w_in[:, MLA_COLS:]],
                               axis=1)
    uq = w_uq.reshape(Q_LORA, MLA_HEADS, QK_NOPE + QK_ROPE)
    zq = jnp.zeros((Q_LORA, MLA_HEADS, QK_PAD - QK_NOPE - QK_ROPE), f32)
    w_qa = jnp.concatenate([uq, zq], axis=2).reshape(Q_LORA, MLA_HEADS * QK_PAD)
    w_qb = jnp.concatenate([jnp.zeros_like(uq[..., :QK_NOPE]), _rot_half_cols(uq[..., QK_NOPE:]), zq],
                           axis=2).reshape(Q_LORA, MLA_HEADS * QK_PAD)
    ukv = w_ukv.reshape(KV_LORA, MLA_HEADS, QK_NOPE + V_HEAD)
    zl = jnp.zeros((DECAY_LORA, RWKV_WIDTH), f32)
    hid = jnp.arange(RWKV_WIDTH) // RWKV_HEAD
    return dict(
        w_in_bf=w_in_ext.astype(bf16),
        q_norm=q_norm, kv_norm=kv_norm,
        w_qa=w_qa.astype(bf16), w_qb=w_qb.astype(bf16),
        w_kn=ukv[..., :QK_NOPE].reshape(KV_LORA, -1).astype(bf16),
        w_v=ukv[..., QK_NOPE:].reshape(KV_LORA, -1).astype(bf16),
        mla_out_norm=mla_out_norm.reshape(1, -1),
        shift_mu=shift_mu.reshape(1, -1), decay_w0=decay_w0.reshape(1, -1),
        decay_w2p=jnp.concatenate([decay_w2, zl], axis=0),
        iclr_a0=iclr_a0.reshape(1, -1),
        iclr_a2p=jnp.concatenate([zl, iclr_a2], axis=0),
        gate_g2=gate_g2, k_k=k_k.reshape(1, -1), k_a=k_a.reshape(1, -1), r_k=r_k.reshape(1, -1),
        lnx_g=lnx_g.reshape(1, -1), lnx_b=lnx_b.reshape(1, -1),
        seg=(hid[:, None] == hid[None, :]).astype(bf16),
        w_out_bf=w_out.astype(bf16), norm_ffn=norm_ffn.reshape(1, d),
        w_router=w_router, b_router=b_router.reshape(1, -1),
    )


def _rope_tables(pos):
    inv = ROPE_THETA ** (-jnp.arange(0, QK_ROPE, 2, dtype=f32) / QK_ROPE)
    ang = pos.astype(f32)[:, None] * inv[None, :]
    cos, sin = jnp.cos(ang), jnp.sin(ang)
    cos2 = jnp.concatenate([cos, cos], axis=1)
    sin2 = jnp.concatenate([sin, sin], axis=1)
    scale = (QK_NOPE + QK_ROPE) ** -0.5 * LOG2_E
    n = pos.shape[0]
    zpad = jnp.zeros((n, QK_PAD - QK_NOPE - QK_ROPE), f32)
    ct = jnp.concatenate([jnp.full((n, QK_NOPE), scale, f32), cos2 * scale, zpad], axis=1)
    st = jnp.concatenate([jnp.zeros((n, QK_NOPE), f32), sin2 * scale, zpad], axis=1)
    kt = jnp.concatenate([cos2, sin2], axis=1)
    return ct, st, kt


def _pair_states(state):
    b = state.shape[0]
    s = jnp.swapaxes(state, -1, -2).reshape(b, N_PAIRS, 2, RWKV_HEAD, RWKV_HEAD)
    z = jnp.zeros_like(s[:, :, 0])
    top = jnp.concatenate([s[:, :, 0], z], axis=-1)
    bot = jnp.concatenate([z, s[:, :, 1]], axis=-1)
    return jnp.concatenate([top, bot], axis=-2)


def _unpair_states(sp):
    b = sp.shape[0]
    h0 = sp[:, :, :RWKV_HEAD, :RWKV_HEAD]
    h1 = sp[:, :, RWKV_HEAD:, RWKV_HEAD:]
    heads = jnp.stack([h0, h1], axis=2).reshape(b, RWKV_HEADS, RWKV_HEAD, RWKV_HEAD)
    return jnp.swapaxes(heads, -1, -2)


def _mix_path(x, mod, pos, cache_lat, cache_rope, wkv_prev, shift_prev, wts, counts0):
    b, s, d = x.shape
    pm, p_rw = _inproj(x, mod, wts["norm_mix"], wts["w_in_bf"])
    ct, st, kt = _rope_tables(pos)
    q, new_lat, new_rope, rope128 = _mla_prep(pm, wts["q_norm"], wts["w_qa"], wts["w_qb"],
                                              wts["kv_norm"], ct, st, kt)
    if cache_lat is None:
        k, v = _kv_proj(new_lat, rope128, wts["w_kn"], wts["w_v"])
        o_mla = _flash_attention(q, k, v)
    else:
        past = cache_lat.shape[1]
        n_keys = past + s
        sk = -(-n_keys // 128) * 128
        lat_all = jnp.concatenate([cache_lat, new_lat, jnp.zeros((b, sk - n_keys, KV_LORA), f32)],
                                  axis=1)
        cache_rope128 = jnp.concatenate([cache_rope, jnp.zeros_like(cache_rope)], axis=-1)
        rope_all = jnp.concatenate([cache_rope128, rope128,
                                    jnp.zeros((b, sk - n_keys, 2 * QK_ROPE), f32)], axis=1)
        k, v = _kv_proj(lat_all, rope_all, wts["w_kn"], wts["w_v"])
        o_mla = _cached_attention(q, k, v, past, n_keys)

    seqs, new_shift = _rwkv_prep(p_rw, shift_prev, wts)
    scan_in, gate_rw, bonus = seqs[:6], seqs[6], seqs[7]
    scan_rows = _scan_rows(s)
    s_pad = -(-s // scan_rows) * scan_rows
    if s_pad != s:
        scan_in = tuple(jnp.pad(t, ((0, 0), (0, s_pad - s), (0, 0))) for t in scan_in)
    y_rw, s_fin = _rwkv_scan(scan_in, _pair_states(wkv_prev))
    new_wkv = _unpair_states(s_fin)

    x1, h2, gates, idx, rank, counts = _outproj(o_mla, y_rw[:, :s], bonus, gate_rw, x, mod, wts,
                                                counts0)
    return dict(x1=x1, h2=h2, gates=gates, idx=idx, rank=rank, counts=counts,
                new_lat=new_lat, new_rope=new_rope, new_wkv=new_wkv, new_shift=new_shift)


def kernel(x_prompt, x_sample, cache_kv_latent, cache_k_rope, state_wkv, state_shift, c_prompt, c_sample, w_ada, b_ada, norm_mix, w_in, q_norm, w_uq, kv_norm, w_ukv, mla_out_norm, shift_mu, decay_w0, decay_w2, iclr_a0, iclr_a2, gate_g2, k_k, k_a, r_k, lnx_g, lnx_b, w_out, norm_ffn, w_router, b_router, w_gate_up, b_gate_up, w_down, b_down, norm_final):
    depth = w_ada.shape[0]
    bp, seq_p, d = x_prompt.shape
    bs, seq_s, _ = x_sample.shape
    past = cache_kv_latent.shape[2]
    pos_p = jnp.arange(seq_p, dtype=i32)
    pos_s = past + jnp.arange(seq_s, dtype=i32)
    zero_wkv = jnp.zeros((bp, RWKV_HEADS, RWKV_HEAD, RWKV_HEAD), f32)
    zero_shift = jnp.zeros((bp, RWKV_COLS), f32)
    n_c = bp + bs
    c_rows = -(-n_c // 8) * 8
    c_all = jnp.concatenate([c_prompt, c_sample, jnp.zeros((c_rows - n_c, d), f32)], axis=0)

    hp, hs = x_prompt, x_sample
    outs_p = [[], [], [], []]
    outs_s = [[], [], [], []]
    tm = EXPERT_TILE
    n_pairs = (bp * seq_p + bs * seq_s) * TOP_K
    n_blocks = (n_pairs + N_EXPERTS * (tm - 1) + tm - 1) // tm
    for l in range(depth):
        wts = _layer_weights(w_in[l], q_norm[l], w_uq[l], kv_norm[l], w_ukv[l], mla_out_norm[l],
                             shift_mu[l], decay_w0[l], decay_w2[l], iclr_a0[l], iclr_a2[l],
                             gate_g2[l], k_k[l], k_a[l], r_k[l], lnx_g[l], lnx_b[l], w_out[l],
                             norm_ffn[l], w_router[l], b_router[l])
        wts["norm_mix"] = norm_mix[l]
        mod = _modulation(c_all, w_ada[l], b_ada[l]).reshape(c_rows, 6, d)
        mod_p, mod_s = mod[:bp], mod[bp:n_c]

        rp = _mix_path(hp, mod_p, pos_p, None, None, zero_wkv, zero_shift, wts,
                       jnp.zeros((1, N_EXPERTS), f32))
        rs = _mix_path(hs, mod_s, pos_s, cache_kv_latent[l], cache_k_rope[l], state_wkv[l],
                       state_shift[l], wts, rp["counts"])

        counts = rs["counts"][0].astype(i32)
        padded = (counts + tm - 1) // tm * tm
        pad_end = jnp.cumsum(padded)
        pad_start = pad_end - padded
        experts = jnp.arange(N_EXPERTS, dtype=i32)

        def slots(r):
            idx = r["idx"][..., :TOP_K]
            start = jnp.sum(jnp.where(idx[..., None] == experts, pad_start, 0), axis=-1)
            return start + r["rank"][..., :TOP_K]

        dest_p, dest_s = slots(rp), slots(rs)
        tile_row = jnp.arange(n_blocks, dtype=i32) * tm
        block_e = jnp.minimum(jnp.sum((pad_end[None, :] <= tile_row[:, None]).astype(i32), axis=1),
                              N_EXPERTS - 1)
        n_used = (pad_end[-1:] // tm).astype(i32)
        zrow = jnp.concatenate([jnp.where(padded > 0, pad_end - tm, -1), n_used]).astype(i32)

        xs = _dispatch(rp["h2"], dest_p, rs["h2"], dest_s, zrow, n_blocks * tm)
        yb = _experts(xs, block_e, n_used, w_gate_up[l], b_gate_up[l], w_down[l], b_down[l])

        last = l == depth - 1
        hp = _combine(rp["x1"], rp["gates"], mod_p, dest_p, yb, norm_final, last)
        hs = _combine(rs["x1"], rs["gates"], mod_s, dest_s, yb, norm_final, last)
        for acc, r in ((outs_p, rp), (outs_s, rs)):
            acc[0].append(r["new_lat"])
            acc[1].append(r["new_rope"])
            acc[2].append(r["new_wkv"])
            acc[3].append(r["new_shift"])

    return (hp, hs, jnp.stack(outs_p[0]), jnp.stack(outs_p[1]), jnp.stack(outs_p[2]),
            jnp.stack(outs_p[3]), jnp.stack(outs_s[0]), jnp.stack(outs_s[1]),
            jnp.stack(outs_s[2]), jnp.stack(outs_s[3]))
```

```python
import functools

import jax
import jax.numpy as jnp
from jax import lax
from jax.experimental import pallas as pl
from jax.experimental.pallas import tpu as pltpu

f32 = jnp.float32
bf16 = jnp.bfloat16
i32 = jnp.int32

D_MODEL = 1024
CHUNK = 64
MLA_HEADS = 4
QK_NOPE = 128
QK_ROPE = 64
V_HEAD = 128
Q_LORA = 256
KV_LORA = 128
ROPE_THETA = 10000.0
MLA_WIDTH = MLA_HEADS * V_HEAD
RWKV_HEAD = 64
RWKV_HEADS = 8
RWKV_WIDTH = RWKV_HEADS * RWKV_HEAD
DECAY_LORA = 64
ICLR_LORA = 64
GATE_LORA = 128
LN_X_EPS = 64e-5
MLA_COLS = Q_LORA + KV_LORA + QK_ROPE
RWKV_COLS = 3 * RWKV_WIDTH + DECAY_LORA + ICLR_LORA + GATE_LORA
N_EXPERTS = 32
TOP_K = 4
D_FF = 1024
SWIGLU_ALPHA = 1.702
SWIGLU_LIMIT = 7.0
NORM_EPS = 1e-6
LOG2_E = 1.4426950408889634

MLA_PAD = 512
QK_PAD = 256
N_PAIRS = RWKV_HEADS // 2
PAIR_W = 2 * RWKV_HEAD
ROUTE_W = 128
ROW_TILE = (8, 128)

TOKEN_TILE = 256
KV_WHOLE_MAX = 2048
ATTN_TILE = 2048
ATTN_SUB = 512
SCAN_CHUNK = 64
SCAN_STEP_CHUNKS = 4
SCAN_PASSES_GRAM = 1
SCAN_PASSES_INV = 1
SCAN_PASSES_STATE = 1
SCAN_PASSES_OUT = 1
EXPERT_TILE = 512
ROUTE_TILE = 512
COMBINE_PARTS = 4
VMEM_LIMIT = 56 * 1024 * 1024


def _token_tile(s):
    t = TOKEN_TILE
    while s % t:
        t //= 2
    assert t >= 8, s
    return t


def _cparams(sem, vmem=None):
    return pltpu.CompilerParams(dimension_semantics=sem, vmem_limit_bytes=vmem or VMEM_LIMIT)


def _dot(a, b):
    return jnp.dot(a.astype(bf16), b.astype(bf16), preferred_element_type=f32)


def _dg(a, b, ca, cb):
    return lax.dot_general(a, b, (((ca,), (cb,)), ((), ())), preferred_element_type=f32)


def _split2(x):
    hi = x.astype(bf16)
    lo = (x - hi.astype(f32)).astype(bf16)
    return hi, lo


def _split3(x):
    hi = x.astype(bf16)
    r1 = x - hi.astype(f32)
    mid = r1.astype(bf16)
    lo = (r1 - mid.astype(f32)).astype(bf16)
    return hi, mid, lo


def _dot3(a, b, ca=1, cb=0):
    ah, al = _split2(a)
    bh, bl = _split2(b)
    return _dg(ah, bh, ca, cb) + (_dg(ah, bl, ca, cb) + _dg(al, bh, ca, cb))


def _dot_exact_lhs(a_bf, b):
    bh, bm, bl = _split3(b)
    return _dg(a_bf, bh, 1, 0) + (_dg(a_bf, bm, 1, 0) + _dg(a_bf, bl, 1, 0))


def _seg_sum(x, seg_bf):
    xh, xl = _split2(x)
    return _dg(xh, seg_bf, 1, 0) + _dg(xl, seg_bf, 1, 0)


def _rms(x, g):
    return x * lax.rsqrt(jnp.mean(x * x, axis=-1, keepdims=True) + NORM_EPS) * g


def _sigmoid(x):
    return 1.0 / (1.0 + jnp.exp(-x))


def _mod_kernel(c_ref, w_ref, b_ref, o_ref):
    c = c_ref[...]
    o_ref[...] = _dot3(c * _sigmoid(c), w_ref[...]) + b_ref[...]


def _modulation(c_all, w_ada, b_ada):
    rows, d = c_all.shape
    n = w_ada.shape[1]
    tn = 1536
    return pl.pallas_call(
        _mod_kernel,
        name="adaln_mod",
        out_shape=jax.ShapeDtypeStruct((rows, n), f32),
        grid=(n // tn,),
        in_specs=[pl.BlockSpec((rows, d), lambda j: (0, 0)),
                  pl.BlockSpec((d, tn), lambda j: (0, j)),
                  pl.BlockSpec((1, tn), lambda j: (0, j))],
        out_specs=pl.BlockSpec((rows, tn), lambda j: (0, j)),
        compiler_params=_cparams(("arbitrary",)),
    )(c_all, w_ada, b_ada.reshape(1, n))


def _project_kv(lat, rope128, wk_ref, wv_ref, k_ref, v_ref):
    lat = lat.astype(bf16)
    kn = jnp.dot(lat, wk_ref[...], preferred_element_type=f32)
    vv = jnp.dot(lat, wv_ref[...], preferred_element_type=f32)
    rope = rope128.astype(bf16)
    for h in range(MLA_HEADS):
        k_ref[0, h, :, :QK_NOPE] = kn[:, h * QK_NOPE:(h + 1) * QK_NOPE].astype(bf16)
        k_ref[0, h, :, QK_NOPE:] = rope
        v_ref[0, h] = vv[:, h * V_HEAD:(h + 1) * V_HEAD].astype(bf16)


def _front_kernel(*refs, with_kv):
    (x_ref, mod_ref, nw_ref, w_ref, qn_ref, wa_ref, wb_ref, kvn_ref, ct_ref, st_ref, kt_ref,
     wk_ref, wv_ref, sp_ref, mu_ref, w0_ref, w2_ref, a0_ref, a2_ref, g2_ref, kk_ref, ka_ref,
     rk_ref, seg_ref) = refs[:24]
    outs = refs[24:-1]
    carry = refs[-1]
    q_ref, lat_ref, rope_ref, rope128_ref = outs[:4]
    outs = outs[4:]
    if with_kv:
        k_ref, v_ref = outs[:2]
        outs = outs[2:]
    r_ref, lw_ref, kh_ref, vv_ref, al_ref, be_ref, g_ref, bo_ref, ns_ref = outs

    h = _rms(x_ref[0], nw_ref[...])
    h = h * (1.0 + mod_ref[0, 1:2, :]) + mod_ref[0, 0:1, :]
    proj = jnp.dot(h.astype(bf16), w_ref[...], preferred_element_type=f32)
    pm = proj[:, :MLA_PAD]
    p = proj[:, MLA_PAD:]

    qn = _rms(pm[:, :Q_LORA], qn_ref[...]).astype(bf16)
    qa = jnp.dot(qn, wa_ref[...], preferred_element_type=f32)
    qb = jnp.dot(qn, wb_ref[...], preferred_element_type=f32)
    ct = ct_ref[...]
    st = st_ref[...]
    for hd in range(MLA_HEADS):
        sl = slice(hd * QK_PAD, (hd + 1) * QK_PAD)
        q_ref[0, hd] = (qa[:, sl] * ct + qb[:, sl] * st).astype(bf16)
    lat = _rms(pm[:, Q_LORA:Q_LORA + KV_LORA], kvn_ref[...])
    lat_ref[0] = lat
    t = pm[:, Q_LORA + KV_LORA:] * kt_ref[...]
    kr = t + pltpu.roll(t, QK_ROPE, axis=1)
    lane = lax.broadcasted_iota(i32, kr.shape, 1)
    rope128 = jnp.where(lane < QK_ROPE, kr, 0.0)
    rope128_ref[0] = rope128
    rope_ref[0] = kr[:, :QK_ROPE]
    if with_kv:
        _project_kv(lat, rope128, wk_ref, wv_ref, k_ref, v_ref)

    @pl.when(pl.program_id(1) == 0)
    def _():
        carry[...] = sp_ref[0]

    ts = p.shape[0]
    row = lax.broadcasted_iota(i32, p.shape, 0)
    prev = jnp.where(row == 0, carry[...], pltpu.roll(p, 1, axis=0))
    last = p[ts - 1:ts, :]
    carry[...] = last
    ns_ref[0] = last
    xs = p + (prev - p) * mu_ref[...]
    w = RWKV_WIDTH
    r, k, v = xs[:, :w], xs[:, w:2 * w], xs[:, 2 * w:3 * w]
    wa = xs[:, 3 * w:3 * w + DECAY_LORA + ICLR_LORA]
    gd = xs[:, 3 * w + DECAY_LORA + ICLR_LORA:]
    z = -(w0_ref[...] + _dot3(jnp.tanh(wa), w2_ref[...]))
    softplus = jnp.maximum(z, 0.0) + jnp.log(1.0 + jnp.exp(-jnp.abs(z)))
    lw = -jnp.exp(-softplus - 0.5)
    a = _sigmoid(a0_ref[...] + _dot3(wa, a2_ref[...]))
    g = _dot3(_sigmoid(gd), g2_ref[...])
    seg = seg_ref[...]
    kk = k * kk_ref[...]
    kk = kk * lax.rsqrt(jnp.maximum(_seg_sum(kk * kk, seg), 1e-24))
    kh = k * (1.0 + (a - 1.0) * ka_ref[...])
    r_ref[0] = r
    lw_ref[0] = lw
    kh_ref[0] = kh
    vv_ref[0] = v
    al_ref[0] = -kk
    be_ref[0] = kk * a
    g_ref[0] = g
    bo_ref[0] = _seg_sum(r * kh * rk_ref[...], seg) * v


def _front(x, mod, shift_prev, wts, tables, with_kv):
    b, s, d = x.shape
    ts = _token_tile(s)
    n_in = wts["w_in_bf"].shape[1]
    hq = MLA_HEADS * QK_PAD
    n = RWKV_COLS
    w = RWKV_WIDTH
    ct, st, kt = tables
    full = lambda r, c: pl.BlockSpec((r, c), lambda i, j: (0, 0))
    tok = lambda width: pl.BlockSpec((1, ts, width), lambda i, j: (i, j, 0))
    heads = lambda width: pl.BlockSpec((1, MLA_HEADS, ts, width), lambda i, j: (i, 0, j, 0))
    in_specs = [tok(d), pl.BlockSpec((1, 6, d), lambda i, j: (i, 0, 0)), full(1, d), full(d, n_in),
                full(1, Q_LORA), full(Q_LORA, hq), full(Q_LORA, hq), full(1, KV_LORA),
                pl.BlockSpec((ts, QK_PAD), lambda i, j: (j, 0)),
                pl.BlockSpec((ts, QK_PAD), lambda i, j: (j, 0)),
                pl.BlockSpec((ts, 2 * QK_ROPE), lambda i, j: (j, 0)),
                full(KV_LORA, MLA_HEADS * QK_NOPE), full(KV_LORA, MLA_HEADS * V_HEAD),
                pl.BlockSpec((1, 1, n), lambda i, j: (i, 0, 0)),
                full(1, n), full(1, w), full(PAIR_W, w), full(1, w), full(PAIR_W, w),
                full(GATE_LORA, w), full(1, w), full(1, w), full(1, w), full(w, w)]
    out_shape = [jax.ShapeDtypeStruct((b, MLA_HEADS, s, QK_PAD), bf16),
                 jax.ShapeDtypeStruct((b, s, KV_LORA), f32),
                 jax.ShapeDtypeStruct((b, s, QK_ROPE), f32),
                 jax.ShapeDtypeStruct((b, s, 2 * QK_ROPE), f32)]
    out_specs = [heads(QK_PAD), tok(KV_LORA), tok(QK_ROPE), tok(2 * QK_ROPE)]
    if with_kv:
        out_shape += [jax.ShapeDtypeStruct((b, MLA_HEADS, s, QK_PAD), bf16),
                      jax.ShapeDtypeStruct((b, MLA_HEADS, s, V_HEAD), bf16)]
        out_specs += [heads(QK_PAD), heads(V_HEAD)]
    out_shape += [jax.ShapeDtypeStruct((b, s, w), f32)] * 8 + [jax.ShapeDtypeStruct((b, 1, n), f32)]
    out_specs += [tok(w)] * 8 + [pl.BlockSpec((1, 1, n), lambda i, j: (i, 0, 0))]
    outs = pl.pallas_call(
        functools.partial(_front_kernel, with_kv=with_kv),
        name="front",
        out_shape=tuple(out_shape),
        grid=(b, s // ts),
        in_specs=in_specs,
        out_specs=tuple(out_specs),
        scratch_shapes=[pltpu.VMEM((1, n), f32)],
        compiler_params=_cparams(("arbitrary", "arbitrary")),
    )(x, mod, wts["norm_mix"].reshape(1, d), wts["w_in_bf"], wts["q_norm"].reshape(1, -1),
      wts["w_qa"], wts["w_qb"], wts["kv_norm"].reshape(1, -1), ct, st, kt, wts["w_kn"], wts["w_v"],
      shift_prev.reshape(b, 1, n), wts["shift_mu"], wts["decay_w0"], wts["decay_w2p"],
      wts["iclr_a0"], wts["iclr_a2p"], wts["gate_g2"], wts["k_k"], wts["k_a"], wts["r_k"],
      wts["seg"])
    outs = list(outs)
    q, new_lat, new_rope, rope128 = outs[:4]
    kv = tuple(outs[4:6]) if with_kv else None
    rest = outs[6:] if with_kv else outs[4:]
    return q, new_lat, new_rope, rope128, kv, tuple(rest[:8]), rest[8].reshape(b, n)


def _kv_proj_kernel(lat_ref, rope_ref, wk_ref, wv_ref, k_ref, v_ref):
    _project_kv(lat_ref[0], rope_ref[0], wk_ref, wv_ref, k_ref, v_ref)


def _kv_proj(lat, rope128, w_kn, w_v):
    b, s, _ = lat.shape
    ts = s if s <= KV_WHOLE_MAX else _token_tile(s)
    return pl.pallas_call(
        _kv_proj_kernel,
        name="kv_proj",
        out_shape=(jax.ShapeDtypeStruct((b, MLA_HEADS, s, QK_PAD), bf16),
                   jax.ShapeDtypeStruct((b, MLA_HEADS, s, V_HEAD), bf16)),
        grid=(b, s // ts),
        in_specs=[pl.BlockSpec((1, ts, KV_LORA), lambda i, j: (i, j, 0)),
                  pl.BlockSpec((1, ts, 2 * QK_ROPE), lambda i, j: (i, j, 0)),
                  pl.BlockSpec((KV_LORA, MLA_HEADS * QK_NOPE), lambda i, j: (0, 0)),
                  pl.BlockSpec((KV_LORA, MLA_HEADS * V_HEAD), lambda i, j: (0, 0))],
        out_specs=(pl.BlockSpec((1, MLA_HEADS, ts, QK_PAD), lambda i, j: (i, 0, j, 0)),
                   pl.BlockSpec((1, MLA_HEADS, ts, V_HEAD), lambda i, j: (i, 0, j, 0))),
        compiler_params=_cparams(("arbitrary", "arbitrary")),
    )(lat, rope128, w_kn, w_v)


def _flash_kernel(qi_ref, kj_ref, q_ref, k_ref, v_ref, o_ref, m_scr, l_scr, acc_scr, *, tile, sub):
    step = pl.program_id(1)
    qi = qi_ref[step]
    kj = kj_ref[step]
    lanes = m_scr.shape[1]

    @pl.when(kj == 0)
    def _():
        m_scr[...] = jnp.full(m_scr.shape, -jnp.inf, f32)
        l_scr[...] = jnp.zeros(l_scr.shape, f32)
        acc_scr[...] = jnp.zeros(acc_scr.shape, f32)

    def update(c, r0, masked):
        rows = pl.ds(r0, tile - r0)
        keys = pl.ds(c * sub, sub)
        s = _dg(q_ref[0, 0, rows, :], k_ref[0, 0, keys, :], 1, 1)
        if masked:
            qpos = r0 + lax.broadcasted_iota(i32, s.shape, 0)
            kpos = c * sub + lax.broadcasted_iota(i32, s.shape, 1)
            s = jnp.where(kpos < (qpos // CHUNK + 1) * CHUNK, s, -jnp.inf)
        m_prev = m_scr[rows, :]
        m_new = jnp.maximum(m_prev, jnp.max(s, axis=1, keepdims=True))
        alpha = jnp.exp2(m_prev - m_new)
        p = jnp.exp2(s - jnp.concatenate([m_new] * (sub // lanes), axis=1))
        psum = p[:, :lanes]
        for t in range(1, sub // lanes):
            psum = psum + p[:, t * lanes:(t + 1) * lanes]
        l_scr[rows, :] = alpha * l_scr[rows, :] + psum
        acc_scr[rows, :] = alpha * acc_scr[rows, :] + jnp.dot(
            p.astype(bf16), v_ref[0, 0, keys, :], preferred_element_type=f32)
        m_scr[rows, :] = m_new

    n_sub = tile // sub

    @pl.when(kj < qi)
    def _():
        for c in range(n_sub):
            update(c, 0, False)

    @pl.when(kj == qi)
    def _():
        for c in range(n_sub):
            update(c, c * sub, True)
        o_ref[0] = acc_scr[...] / jnp.sum(l_scr[...], axis=1, keepdims=True)


def _flash_attention(q, k, v):
    b, nh, s, _ = q.shape
    assert b == 1
    tile = min(ATTN_TILE, s)
    sub = min(ATTN_SUB, tile)
    assert sub % CHUNK == 0 and tile % sub == 0
    nt = s // tile
    qi = [i for i in range(nt) for _ in range(i + 1)]
    kj = [j for i in range(nt) for j in range(i + 1)]
    grid_spec = pltpu.PrefetchScalarGridSpec(
        num_scalar_prefetch=2,
        grid=(nh, len(qi)),
        in_specs=[pl.BlockSpec((1, 1, tile, QK_PAD), lambda h, t, qi, kj: (0, h, qi[t], 0)),
                  pl.BlockSpec((1, 1, tile, QK_PAD), lambda h, t, qi, kj: (0, h, kj[t], 0)),
                  pl.BlockSpec((1, 1, tile, V_HEAD), lambda h, t, qi, kj: (0, h, kj[t], 0))],
        out_specs=pl.BlockSpec((1, tile, V_HEAD), lambda h, t, qi, kj: (0, qi[t], h)),
        scratch_shapes=[pltpu.VMEM((tile, V_HEAD), f32), pltpu.VMEM((tile, V_HEAD), f32),
                        pltpu.VMEM((tile, V_HEAD), f32)])
    return pl.pallas_call(
        functools.partial(_flash_kernel, tile=tile, sub=sub),
        name="flash_attn",
        out_shape=jax.ShapeDtypeStruct((1, s, MLA_WIDTH), f32),
        grid_spec=grid_spec,
        compiler_params=_cparams(("arbitrary", "arbitrary")),
    )(jnp.asarray(qi, i32), jnp.asarray(kj, i32), q, k, v)


def _cached_attn_kernel(q_ref, k_ref, v_ref, o_ref, *, past, n_keys):
    for h in range(MLA_HEADS):
        s = _dg(q_ref[0, h], k_ref[0, h], 1, 1)
        qpos = past + lax.broadcasted_iota(i32, s.shape, 0)
        kpos = lax.broadcasted_iota(i32, s.shape, 1)
        limit = (qpos // CHUNK + 1) * CHUNK
        s = jnp.where((kpos < limit) & (kpos < n_keys), s, -jnp.inf)
        m = jnp.max(s, axis=1, keepdims=True)
        p = jnp.exp2(s - m)
        o = jnp.dot(p.astype(bf16), v_ref[0, h], preferred_element_type=f32)
        o_ref[0, :, h * V_HEAD:(h + 1) * V_HEAD] = o / jnp.sum(p, axis=1, keepdims=True)


def _cached_attention(q, k, v, past, n_keys):
    b, nh, sq, _ = q.shape
    sk = k.shape[2]
    return pl.pallas_call(
        functools.partial(_cached_attn_kernel, past=past, n_keys=n_keys),
        name="cached_attn",
        out_shape=jax.ShapeDtypeStruct((b, sq, MLA_WIDTH), f32),
        grid=(b,),
        in_specs=[pl.BlockSpec((1, nh, sq, QK_PAD), lambda i: (i, 0, 0, 0)),
                  pl.BlockSpec((1, nh, sk, QK_PAD), lambda i: (i, 0, 0, 0)),
                  pl.BlockSpec((1, nh, sk, V_HEAD), lambda i: (i, 0, 0, 0))],
        out_specs=pl.BlockSpec((1, sq, MLA_WIDTH), lambda i: (i, 0, 0)),
        compiler_params=_cparams(("arbitrary",)),
    )(q, k, v)


def _mm(a, b, ca, cb, passes):
    if passes == 1:
        return _dg(a.astype(bf16), b.astype(bf16), ca, cb)
    return _dot3(a, b, ca, cb)


def _scan_kernel(r_ref, lw_ref, k_ref, v_ref, al_ref, be_ref, s0_ref, y_ref, sf_ref, st, *, c):
    @pl.when(pl.program_id(1) == 0)
    def _():
        st[...] = s0_ref[0]

    rows = r_ref.shape[1]
    nch = rows // c
    c2 = 2 * c
    rowb = lax.broadcasted_iota(i32, (rows, rows), 0)
    colb = lax.broadcasted_iota(i32, (rows, rows), 1)
    same_chunk = (rowb // c) == (colb // c)

    lw = lw_ref[0]
    cum = _dot_exact_lhs((same_chunk & (colb <= rowb)).astype(bf16), lw)
    tot = _dot_exact_lhs(same_chunk.astype(bf16), lw)
    g_inv = jnp.exp(-cum)
    g_tail = jnp.exp(tot - cum)
    r_t = r_ref[0] * jnp.exp(cum)
    a_t = al_ref[0] * jnp.exp(cum - lw)
    b_t = be_ref[0] * g_inv
    k_t = k_ref[0] * g_inv
    b_c = be_ref[0] * g_tail
    k_c = k_ref[0] * g_tail
    v_all = v_ref[0]

    ri = lax.broadcasted_iota(i32, (c2, c2), 0)
    ci = lax.broadcasted_iota(i32, (c2, c2), 1)
    same = (ri // c) == (ci // c)
    strict = same & (ci < ri)
    incl = same & (ci <= ri)
    eye = (ri == ci).astype(f32)
    head0 = lax.broadcasted_iota(i32, (c, PAIR_W), 1) < RWKV_HEAD
    ones = jnp.ones((c, PAIR_W), bf16)

    def stack(x):
        return jnp.concatenate([jnp.where(head0, x, 0.0), jnp.where(head0, 0.0, x)], axis=0)

    chains = [(j, p) for j in range(nch) for p in range(N_PAIRS)]

    def part(t, j, p):
        return t[j * c:(j + 1) * c, p * PAIR_W:(p + 1) * PAIR_W]

    ops = {}
    for j, p in chains:
        ops[j, p] = {n: stack(part(t, j, p)) for n, t in
                     (("a", a_t), ("r", r_t), ("b", b_t), ("k", k_t), ("v", v_all),
                      ("bc", b_c), ("kc", k_c))}

    gram = {jp: _mm(jnp.concatenate([o["a"], o["r"]], axis=0),
                    jnp.concatenate([o["b"], o["k"]], axis=0), 1, 1, SCAN_PASSES_GRAM)
            for jp, o in ops.items()}
    a_ab = {jp: jnp.where(strict, g[:c2, :c2], 0.0) for jp, g in gram.items()}
    a_ak = {jp: jnp.where(strict, g[:c2, c2:], 0.0) for jp, g in gram.items()}
    a_rb = {jp: jnp.where(incl, g[c2:, :c2], 0.0) for jp, g in gram.items()}
    a_rk = {jp: jnp.where(incl, g[c2:, c2:], 0.0) for jp, g in gram.items()}
    t_inv = {jp: eye + low for jp, low in a_ab.items()}
    pw = dict(a_ab)
    for _ in range(max(c.bit_length() - 2, 0)):
        pw = {jp: _mm(m, m, 1, 0, SCAN_PASSES_INV) for jp, m in pw.items()}
        t_inv = {jp: t_inv[jp] + _mm(t_inv[jp], pw[jp], 1, 0, SCAN_PASSES_INV) for jp in pw}
    decay = {}
    for j, p in chains:
        lh, lm, ll = _split3(part(lw, j, p))
        decay[j, p] = jnp.exp(_dg(lh, ones, 0, 0) + (_dg(lm, ones, 0, 0) + _dg(ll, ones, 0, 0)))

    states = [st[p] for p in range(N_PAIRS)]
    y_rows = []
    prs = range(N_PAIRS)
    for j in range(nch):
        x = [_mm(jnp.concatenate([ops[j, p]["a"], a_ak[j, p]], axis=1),
                 jnp.concatenate([states[p], ops[j, p]["v"]], axis=0), 1, 0, SCAN_PASSES_STATE)
             for p in prs]
        u = [_mm(t_inv[j, p], x[p], 1, 0, SCAN_PASSES_STATE) for p in prs]
        upd = [_mm(jnp.concatenate([ops[j, p]["bc"], ops[j, p]["kc"]], axis=0),
                   jnp.concatenate([u[p], ops[j, p]["v"]], axis=0), 0, 0, SCAN_PASSES_STATE)
               for p in prs]
        y = [_mm(jnp.concatenate([ops[j, p]["r"], a_rb[j, p], a_rk[j, p]], axis=1),
                 jnp.concatenate([states[p], u[p], ops[j, p]["v"]], axis=0), 1, 0, SCAN_PASSES_OUT)
             for p in prs]
        states = [states[p] * decay[j, p] + upd[p] for p in prs]
        y_rows.append(jnp.concatenate([yp[:c] + yp[c:] for yp in y], axis=1))

    y_ref[0] = jnp.concatenate(y_rows, axis=0)
    for p in prs:
        st[p] = states[p]
        sf_ref[0, p] = states[p]


def _scan_rows(s):
    return SCAN_CHUNK * min(SCAN_STEP_CHUNKS, -(-s // SCAN_CHUNK))


def _rwkv_scan(seqs, s0_pairs):
    r, lw, kh, v, al, be = seqs
    b, s, w = r.shape
    rows = _scan_rows(s)
    tok = pl.BlockSpec((1, rows, w), lambda i, j: (i, j, 0))
    stt = pl.BlockSpec((1, N_PAIRS, PAIR_W, PAIR_W), lambda i, j: (i, 0, 0, 0))
    return pl.pallas_call(
        functools.partial(_scan_kernel, c=SCAN_CHUNK),
        name="rwkv_scan",
        out_shape=(jax.ShapeDtypeStruct((b, s, w), f32),
                   jax.ShapeDtypeStruct((b, N_PAIRS, PAIR_W, PAIR_W), f32)),
        grid=(b, s // rows),
        in_specs=[tok] * 6 + [stt],
        out_specs=(tok, stt),
        scratch_shapes=[pltpu.VMEM((N_PAIRS, PAIR_W, PAIR_W), f32)],
        compiler_params=_cparams(("arbitrary", "arbitrary")),
    )(r, lw, kh, v, al, be, s0_pairs)


def _outproj_kernel(om_ref, y_ref, bo_ref, g_ref, x_ref, mod_ref, mn_ref, lng_ref, lnb_ref,
                    seg_ref, wo_ref, nf_ref, wr_ref, br_ref, tri_ref, cnt0_ref,
                    x1_ref, h2_ref, gate_ref, idx_ref, rank_ref, cnt_ref, run):
    @pl.when((pl.program_id(0) == 0) & (pl.program_id(1) == 0))
    def _():
        run[...] = cnt0_ref[...]

    y = y_ref[0]
    seg = seg_ref[...]
    inv_n = 1.0 / RWKV_HEAD
    yc = y - _seg_sum(y, seg) * inv_n
    var = _seg_sum(yc * yc, seg) * inv_n
    o_rw = (yc * lax.rsqrt(var + LN_X_EPS) * lng_ref[...] + lnb_ref[...] + bo_ref[0]) * g_ref[0]

    om = _rms(om_ref[0], mn_ref[...])
    mix = (jnp.dot(om.astype(bf16), wo_ref[:MLA_WIDTH, :], preferred_element_type=f32)
           + jnp.dot(o_rw.astype(bf16), wo_ref[MLA_WIDTH:, :], preferred_element_type=f32))
    x1 = x_ref[0] + mod_ref[0, 2:3, :] * mix
    x1_ref[0] = x1
    h2 = _rms(x1, nf_ref[...]) * (1.0 + mod_ref[0, 4:5, :]) + mod_ref[0, 3:4, :]
    h2_ref[0] = h2

    logits = _dot3(h2, wr_ref[...]) + br_ref[...]
    ts = logits.shape[0]
    lane_e = lax.broadcasted_iota(i32, logits.shape, 1)
    lane_w = lax.broadcasted_iota(i32, (ts, ROUTE_W), 1)
    tri = tri_ref[...]
    base = run[...]
    gates = jnp.zeros((ts, ROUTE_W), f32)
    idxs = jnp.zeros((ts, ROUTE_W), i32)
    ranks = jnp.zeros((ts, ROUTE_W), i32)
    vals = []
    work = logits
    for k in range(TOP_K):
        m = jnp.max(work, axis=1, keepdims=True)
        sel = jnp.min(jnp.where(work == m, lane_e, N_EXPERTS), axis=1, keepdims=True)
        hit = lane_e == sel
        work = jnp.where(hit, -jnp.inf, work)
        vals.append(m)
        onehot = hit.astype(bf16)
        before = jnp.dot(tri, onehot, preferred_element_type=f32)
        rank = jnp.sum(jnp.where(hit, before + base, 0.0), axis=1, keepdims=True)
        base = base + jnp.sum(hit.astype(f32), axis=0, keepdims=True)
        idxs = jnp.where(lane_w == k, sel, idxs)
        ranks = jnp.where(lane_w == k, rank.astype(i32), ranks)
    run[...] = base
    es = [jnp.exp(vk - vals[0]) for vk in vals]
    inv = 1.0 / (es[0] + es[1] + es[2] + es[3])
    for k in range(TOP_K):
        gates = jnp.where(lane_w == k, es[k] * inv, gates)
    gate_ref[0] = gates
    idx_ref[0] = idxs
    rank_ref[0] = ranks
    cnt_ref[...] = base


def _outproj(o_mla, y_rw, bonus, gate_rw, x, mod, wts, counts0):
    b, s, d = x.shape
    w = RWKV_WIDTH
    ts = _token_tile(s)
    tok = lambda width: pl.BlockSpec((1, ts, width), lambda i, j: (i, j, 0))
    full = lambda r, c: pl.BlockSpec((r, c), lambda i, j: (0, 0))
    tri = (lax.broadcasted_iota(i32, (ts, ts), 1) < lax.broadcasted_iota(i32, (ts, ts), 0)
           ).astype(bf16)
    return pl.pallas_call(
        _outproj_kernel,
        name="outproj_route",
        out_shape=(jax.ShapeDtypeStruct((b, s, d), f32),
                   jax.ShapeDtypeStruct((b, s, d), f32),
                   jax.ShapeDtypeStruct((b, s, ROUTE_W), f32),
                   jax.ShapeDtypeStruct((b, s, ROUTE_W), i32),
                   jax.ShapeDtypeStruct((b, s, ROUTE_W), i32),
                   jax.ShapeDtypeStruct((1, N_EXPERTS), f32)),
        grid=(b, s // ts),
        in_specs=[tok(MLA_WIDTH), tok(w), tok(w), tok(w), tok(d),
                  pl.BlockSpec((1, 6, d), lambda i, j: (i, 0, 0)),
                  full(1, MLA_WIDTH), full(1, w), full(1, w), full(w, w),
                  full(d, d), full(1, d), full(d, N_EXPERTS),
                  full(1, N_EXPERTS), full(ts, ts), full(1, N_EXPERTS)],
        out_specs=(tok(d), tok(d), tok(ROUTE_W), tok(ROUTE_W), tok(ROUTE_W),
                   full(1, N_EXPERTS)),
        scratch_shapes=[pltpu.VMEM((1, N_EXPERTS), f32)],
        compiler_params=_cparams(("arbitrary", "arbitrary")),
    )(o_mla, y_rw, bonus, gate_rw, x, mod, wts["mla_out_norm"], wts["lnx_g"], wts["lnx_b"],
      wts["seg"], wts["w_out_bf"], wts["norm_ffn"], wts["w_router"], wts["b_router"], tri, counts0)


def _to_row_tiles(tiles_ref, x, first=0):
    sub, lanes = ROW_TILE
    for c in range(sub):
        tiles_ref[pl.ds(first * sub + c, x.shape[0], stride=sub), :] = x[:, c * lanes:(c + 1) * lanes]


def _from_row_tiles(tiles_ref, n, first=0):
    sub = ROW_TILE[0]
    return jnp.concatenate([tiles_ref[pl.ds(first * sub + c, n, stride=sub), :]
                            for c in range(sub)], axis=1)


def _row_tile(ref, t):
    sub = ROW_TILE[0]
    return ref.at[pl.ds(pl.multiple_of(t * sub, sub), sub), :]


def _scatter_rows(dest_ref, h_ref, hb, xs_ref, sem):
    tt = h_ref.shape[1]
    _to_row_tiles(hb, h_ref[0])

    def row_copy(t, d):
        return pltpu.make_async_copy(_row_tile(hb, t), _row_tile(xs_ref, d), sem)

    def issue(t, carry):
        for k in range(TOP_K):
            row_copy(t, dest_ref[0, 0, TOP_K * t + k]).start(priority=k % 2)
        return carry

    def drain(t, carry):
        for k in range(TOP_K):
            row_copy(0, 0).wait()
        return carry

    lax.fori_loop(0, tt, issue, 0)
    lax.fori_loop(0, tt, drain, 0)


def _dispatch_kernel(zrow_ref, dest_ref, h_ref, dest2_ref, h2_ref, xs_ref, zeros, hb, hb2, sem,
                     zsem):
    sub = ROW_TILE[0]
    tm = zeros.shape[0] // sub

    @pl.when(pl.program_id(0) == 0)
    def _():
        zeros[...] = jnp.zeros(zeros.shape, f32)

        def tile_copy(row):
            first = pl.multiple_of(row * sub, tm * sub)
            return pltpu.make_async_copy(zeros, xs_ref.at[pl.ds(first, tm * sub), :], zsem)

        def issue(e, carry):
            @pl.when(zrow_ref[e] >= 0)
            def _():
                tile_copy(zrow_ref[e]).start()
            return carry

        def drain(e, carry):
            @pl.when(zrow_ref[e] >= 0)
            def _():
                tile_copy(0).wait()
            return carry

        lax.fori_loop(0, N_EXPERTS, issue, 0)
        lax.fori_loop(0, N_EXPERTS, drain, 0)

        n_used = zrow_ref[N_EXPERTS]
        n_tiles = xs_ref.shape[0] // (tm * sub)

        def issue_tail(i, carry):
            tile_copy(i * tm).start()
            return carry

        def drain_tail(i, carry):
            tile_copy(0).wait()
            return carry

        lax.fori_loop(n_used, n_tiles, issue_tail, 0)
        lax.fori_loop(n_used, n_tiles, drain_tail, 0)

    _scatter_rows(dest_ref, h_ref, hb, xs_ref, sem)

    @pl.when(pl.program_id(0) == pl.num_programs(0) - 1)
    def _():
        _scatter_rows(dest2_ref, h2_ref, hb2, xs_ref, sem)


def _route_tile(s):
    t = ROUTE_TILE
    while s % t:
        t //= 2
    assert t >= 8, s
    return t


def _dispatch(h_big, dest_big, h_small, dest_small, zrow, n_rows):
    d = h_big.shape[-1]
    sub, lanes = ROW_TILE
    assert d == sub * lanes
    h_big = h_big.reshape(1, -1, d)
    h_small = h_small.reshape(1, -1, d)
    s, s2 = h_big.shape[1], h_small.shape[1]
    tt = _route_tile(s)
    nt = s // tt
    grid_spec = pltpu.PrefetchScalarGridSpec(
        num_scalar_prefetch=1,
        grid=(nt,),
        in_specs=[pl.BlockSpec((1, 1, tt * TOP_K), lambda j, z: (j, 0, 0),
                               memory_space=pltpu.SMEM),
                  pl.BlockSpec((1, tt, d), lambda j, z: (0, j, 0)),
                  pl.BlockSpec((1, 1, s2 * TOP_K), lambda j, z: (0, 0, 0),
                               memory_space=pltpu.SMEM),
                  pl.BlockSpec((1, s2, d), lambda j, z: (0, 0, 0))],
        out_specs=pl.BlockSpec(memory_space=pl.ANY),
        scratch_shapes=[pltpu.VMEM((EXPERT_TILE * sub, lanes), f32),
                        pltpu.VMEM((tt * sub, lanes), f32), pltpu.VMEM((s2 * sub, lanes), f32),
                        pltpu.SemaphoreType.DMA(()), pltpu.SemaphoreType.DMA(())])
    return pl.pallas_call(
        _dispatch_kernel,
        name="moe_dispatch",
        out_shape=jax.ShapeDtypeStruct((n_rows * sub, lanes), f32),
        grid_spec=grid_spec,
        compiler_params=_cparams(("arbitrary",)),
    )(zrow, dest_big.reshape(nt, 1, tt * TOP_K), h_big, dest_small.reshape(1, 1, s2 * TOP_K),
      h_small)


def _expert_kernel(be_ref, nu_ref, x_ref, wgu_ref, bgu_ref, wd_ref, bd_ref, y_ref,
                   wgu_bf, wd_bf):
    i = pl.program_id(0)
    prev = be_ref[jnp.maximum(i - 1, 0)]
    fresh = (i == 0) | (be_ref[i] != prev)

    @pl.when(fresh & (i < nu_ref[0]))
    def _():
        wgu_bf[...] = wgu_ref[0].astype(bf16)
        wd_bf[...] = wd_ref[0].astype(bf16)

    @pl.when(i < nu_ref[0])
    def _():
        x = _from_row_tiles(x_ref, x_ref.shape[0] // ROW_TILE[0]).astype(bf16)
        gu = jnp.dot(x, wgu_bf[...], preferred_element_type=f32) + bgu_ref[0]
        gate = jnp.minimum(gu[:, :D_FF], SWIGLU_LIMIT)
        up = jnp.clip(gu[:, D_FF:], -SWIGLU_LIMIT, SWIGLU_LIMIT)
        act = (up + 1.0) * (gate * _sigmoid(SWIGLU_ALPHA * gate))
        _to_row_tiles(y_ref, jnp.dot(act.astype(bf16), wd_bf[...], preferred_element_type=f32)
                      + bd_ref[0])

    @pl.when(i >= nu_ref[0])
    def _():
        y_ref[...] = jnp.zeros(y_ref.shape, f32)


def _experts(xs, block_e, n_used, w_gate_up, b_gate_up, w_down, b_down):
    sub, lanes = ROW_TILE
    n_rows = xs.shape[0] // sub
    d = w_down.shape[-1]
    tm = EXPERT_TILE
    nb = n_rows // tm
    last = lambda i, be, nu: jnp.minimum(i, nu[0] - 1)
    grid_spec = pltpu.PrefetchScalarGridSpec(
        num_scalar_prefetch=2,
        grid=(nb,),
        in_specs=[pl.BlockSpec((tm * sub, lanes), lambda i, be, nu: (last(i, be, nu), 0)),
                  pl.BlockSpec((1, d, 2 * D_FF), lambda i, be, nu: (be[last(i, be, nu)], 0, 0)),
                  pl.BlockSpec((1, 1, 2 * D_FF), lambda i, be, nu: (be[last(i, be, nu)], 0, 0)),
                  pl.BlockSpec((1, D_FF, d), lambda i, be, nu: (be[last(i, be, nu)], 0, 0)),
                  pl.BlockSpec((1, 1, d), lambda i, be, nu: (be[last(i, be, nu)], 0, 0))],
        out_specs=pl.BlockSpec((tm * sub, lanes), lambda i, be, nu: (i, 0)),
        scratch_shapes=[pltpu.VMEM((d, 2 * D_FF), bf16), pltpu.VMEM((D_FF, d), bf16)])
    return pl.pallas_call(
        _expert_kernel,
        name="moe_experts",
        out_shape=jax.ShapeDtypeStruct((n_rows * sub, lanes), f32),
        grid_spec=grid_spec,
        compiler_params=_cparams(("arbitrary",)),
    )(block_e, n_used, xs, w_gate_up, b_gate_up.reshape(N_EXPERTS, 1, -1), w_down,
      b_down.reshape(N_EXPERTS, 1, -1))


def _combine_kernel(dest_ref, x1_ref, gate_ref, mod_ref, nf_ref, yb_ref, o_ref, buf, sems, *,
                    final, parts):
    tt = x1_ref.shape[1]
    pt = tt // parts

    def row_copy(t, k, d, q):
        return pltpu.make_async_copy(_row_tile(yb_ref, d), _row_tile(buf, k * tt + t), sems.at[q])

    for q in range(parts):
        def issue(t, carry, q=q):
            for k in range(TOP_K):
                row_copy(t, k, dest_ref[0, 0, TOP_K * t + k], q).start(priority=k % 2)
            return carry
        lax.fori_loop(q * pt, (q + 1) * pt, issue, 0)

    for q in range(parts):
        def drain(t, carry, q=q):
            for k in range(TOP_K):
                row_copy(0, 0, 0, q).wait()
            return carry
        lax.fori_loop(0, pt, drain, 0)
        rows = pl.ds(q * pt, pt)
        gates = gate_ref[0, rows, :]
        y = gates[:, 0:1] * _from_row_tiles(buf, pt, q * pt)
        for k in range(1, TOP_K):
            y = y + gates[:, k:k + 1] * _from_row_tiles(buf, pt, k * tt + q * pt)
        x = x1_ref[0, rows, :] + mod_ref[0, 5:6, :] * y
        o_ref[0, rows, :] = _rms(x, nf_ref[...]) if final else x


def _combine(x1, gates, mod, dest, yb, norm_final, final):
    b, s, d = x1.shape
    tt = _route_tile(s)
    nt = s // tt
    parts = min(COMBINE_PARTS, tt // 8)
    dest_t = dest.reshape(b * nt, 1, tt * TOP_K)
    return pl.pallas_call(
        functools.partial(_combine_kernel, final=final, parts=parts),
        name="moe_combine",
        out_shape=jax.ShapeDtypeStruct((b, s, d), f32),
        grid=(b, nt),
        in_specs=[pl.BlockSpec((1, 1, tt * TOP_K), lambda i, j: (i * nt + j, 0, 0),
                               memory_space=pltpu.SMEM),
                  pl.BlockSpec((1, tt, d), lambda i, j: (i, j, 0)),
                  pl.BlockSpec((1, tt, ROUTE_W), lambda i, j: (i, j, 0)),
                  pl.BlockSpec((1, 6, d), lambda i, j: (i, 0, 0)),
                  pl.BlockSpec((1, d), lambda i, j: (0, 0)),
                  pl.BlockSpec(memory_space=pl.ANY)],
        out_specs=pl.BlockSpec((1, tt, d), lambda i, j: (i, j, 0)),
        scratch_shapes=[pltpu.VMEM((TOP_K * tt * ROW_TILE[0], ROW_TILE[1]), f32),
                        pltpu.SemaphoreType.DMA((parts,))],
        compiler_params=_cparams(("arbitrary", "arbitrary")),
    )(dest_t, x1, gates, mod, norm_final.reshape(1, d), yb)


def _rot_half_cols(w):
    half = w.shape[-1] // 2
    return jnp.concatenate([-w[..., half:], w[..., :half]], axis=-1)


def _layer_weights(w_in, q_norm, w_uq, kv_norm, w_ukv, mla_out_norm, shift_mu, decay_w0, decay_w2,
                   iclr_a0, iclr_a2, gate_g2, k_k, k_a, r_k, lnx_g, lnx_b, w_out, norm_ffn,
                   w_router, b_router):
    d = w_in.shape[0]
    kr_cols = w_in[:, Q_LORA + KV_LORA:MLA_COLS]
    w_in_ext = jnp.concatenate([w_in[:, :MLA_COLS], _rot_half_cols(kr_cols), w_in[:, MLA_COLS:]],
                               axis=1)
    uq = w_uq.reshape(Q_LORA, MLA_HEADS, QK_NOPE + QK_ROPE)
    zq = jnp.zeros((Q_LORA, MLA_HEADS, QK_PAD - QK_NOPE - QK_ROPE), f32)
    w_qa = jnp.concatenate([uq, zq], axis=2).reshape(Q_LORA, MLA_HEADS * QK_PAD)
    w_qb = jnp.concatenate([jnp.zeros_like(uq[..., :QK_NOPE]), _rot_half_cols(uq[..., QK_NOPE:]), zq],
                           axis=2).reshape(Q_LORA, MLA_HEADS * QK_PAD)
    ukv = w_ukv.reshape(KV_LORA, MLA_HEADS, QK_NOPE + V_HEAD)
    zl = jnp.zeros((DECAY_LORA, RWKV_WIDTH), f32)
    hid = jnp.arange(RWKV_WIDTH) // RWKV_HEAD
    return dict(
        w_in_bf=w_in_ext.astype(bf16),
        q_norm=q_norm, kv_norm=kv_norm,
        w_qa=w_qa.astype(bf16), w_qb=w_qb.astype(bf16),
        w_kn=ukv[..., :QK_NOPE].reshape(KV_LORA, -1).astype(bf16),
        w_v=ukv[..., QK_NOPE:].reshape(KV_LORA, -1).astype(bf16),
        mla_out_norm=mla_out_norm.reshape(1, -1),
        shift_mu=shift_mu.reshape(1, -1), decay_w0=decay_w0.reshape(1, -1),
        decay_w2p=jnp.concatenate([decay_w2, zl], axis=0),
        iclr_a0=iclr_a0.reshape(1, -1),
        iclr_a2p=jnp.concatenate([zl, iclr_a2], axis=0),
        gate_g2=gate_g2, k_k=k_k.reshape(1, -1), k_a=k_a.reshape(1, -1), r_k=r_k.reshape(1, -1),
        lnx_g=lnx_g.reshape(1, -1), lnx_b=lnx_b.reshape(1, -1),
        seg=(hid[:, None] == hid[None, :]).astype(bf16),
        w_out_bf=w_out.astype(bf16), norm_ffn=norm_ffn.reshape(1, d),
        w_router=w_router, b_router=b_router.reshape(1, -1),
    )


def _rope_tables(pos):
    inv = ROPE_THETA ** (-jnp.arange(0, QK_ROPE, 2, dtype=f32) / QK_ROPE)
    ang = pos.astype(f32)[:, None] * inv[None, :]
    cos, sin = jnp.cos(ang), jnp.sin(ang)
    cos2 = jnp.concatenate([cos, cos], axis=1)
    sin2 = jnp.concatenate([sin, sin], axis=1)
    scale = (QK_NOPE + QK_ROPE) ** -0.5 * LOG2_E
    n = pos.shape[0]
    zpad = jnp.zeros((n, QK_PAD - QK_NOPE - QK_ROPE), f32)
    ct = jnp.concatenate([jnp.full((n, QK_NOPE), scale, f32), cos2 * scale, zpad], axis=1)
    st = jnp.concatenate([jnp.zeros((n, QK_NOPE), f32), sin2 * scale, zpad], axis=1)
    kt = jnp.concatenate([cos2, sin2], axis=1)
    return ct, st, kt


def _pair_states(state):
    b = state.shape[0]
    s = jnp.swapaxes(state, -1, -2).reshape(b, N_PAIRS, 2, RWKV_HEAD, RWKV_HEAD)
    z = jnp.zeros_like(s[:, :, 0])
    top = jnp.concatenate([s[:, :, 0], z], axis=-1)
    bot = jnp.concatenate([z, s[:, :, 1]], axis=-1)
    return jnp.concatenate([top, bot], axis=-2)


def _unpair_states(sp):
    b = sp.shape[0]
    h0 = sp[:, :, :RWKV_HEAD, :RWKV_HEAD]
    h1 = sp[:, :, RWKV_HEAD:, RWKV_HEAD:]
    heads = jnp.stack([h0, h1], axis=2).reshape(b, RWKV_HEADS, RWKV_HEAD, RWKV_HEAD)
    return jnp.swapaxes(heads, -1, -2)


def _mix_path(x, mod, pos, cache_lat, cache_rope, wkv_prev, shift_prev, wts, counts0):
    b, s, d = x.shape
    q, new_lat, new_rope, rope128, kv, seqs, new_shift = _front(
        x, mod, shift_prev, wts, _rope_tables(pos), with_kv=cache_lat is None)
    if cache_lat is None:
        o_mla = _flash_attention(q, *kv)
    else:
        past = cache_lat.shape[1]
        n_keys = past + s
        sk = -(-n_keys // 128) * 128
        lat_all = jnp.concatenate([cache_lat, new_lat, jnp.zeros((b, sk - n_keys, KV_LORA), f32)],
                                  axis=1)
        cache_rope128 = jnp.concatenate([cache_rope, jnp.zeros_like(cache_rope)], axis=-1)
        rope_all = jnp.concatenate([cache_rope128, rope128,
                                    jnp.zeros((b, sk - n_keys, 2 * QK_ROPE), f32)], axis=1)
        k, v = _kv_proj(lat_all, rope_all, wts["w_kn"], wts["w_v"])
        o_mla = _cached_attention(q, k, v, past, n_keys)

    scan_in, gate_rw, bonus = seqs[:6], seqs[6], seqs[7]
    scan_rows = _scan_rows(s)
    s_pad = -(-s // scan_rows) * scan_rows
    if s_pad != s:
        scan_in = tuple(jnp.pad(t, ((0, 0), (0, s_pad - s), (0, 0))) for t in scan_in)
    y_rw, s_fin = _rwkv_scan(scan_in, _pair_states(wkv_prev))
    new_wkv = _unpair_states(s_fin)

    x1, h2, gates, idx, rank, counts = _outproj(o_mla, y_rw[:, :s], bonus, gate_rw, x, mod, wts,
                                                counts0)
    return dict(x1=x1, h2=h2, gates=gates, idx=idx, rank=rank, counts=counts,
                new_lat=new_lat, new_rope=new_rope, new_wkv=new_wkv, new_shift=new_shift)


def kernel(x_prompt, x_sample, cache_kv_latent, cache_k_rope, state_wkv, state_shift, c_prompt, c_sample, w_ada, b_ada, norm_mix, w_in, q_norm, w_uq, kv_norm, w_ukv, mla_out_norm, shift_mu, decay_w0, decay_w2, iclr_a0, iclr_a2, gate_g2, k_k, k_a, r_k, lnx_g, lnx_b, w_out, norm_ffn, w_router, b_router, w_gate_up, b_gate_up, w_down, b_down, norm_final):
    depth = w_ada.shape[0]
    bp, seq_p, d = x_prompt.shape
    bs, seq_s, _ = x_sample.shape
    past = cache_kv_latent.shape[2]
    pos_p = jnp.arange(seq_p, dtype=i32)
    pos_s = past + jnp.arange(seq_s, dtype=i32)
    zero_wkv = jnp.zeros((bp, RWKV_HEADS, RWKV_HEAD, RWKV_HEAD), f32)
    zero_shift = jnp.zeros((bp, RWKV_COLS), f32)
    n_c = bp + bs
    c_rows = -(-n_c // 8) * 8
    c_all = jnp.concatenate([c_prompt, c_sample, jnp.zeros((c_rows - n_c, d), f32)], axis=0)

    hp, hs = x_prompt, x_sample
    outs_p = [[], [], [], []]
    outs_s = [[], [], [], []]
    tm = EXPERT_TILE
    n_pairs = (bp * seq_p + bs * seq_s) * TOP_K
    n_blocks = (n_pairs + N_EXPERTS * (tm - 1) + tm - 1) // tm
    for l in range(depth):
        wts = _layer_weights(w_in[l], q_norm[l], w_uq[l], kv_norm[l], w_ukv[l], mla_out_norm[l],
                             shift_mu[l], decay_w0[l], decay_w2[l], iclr_a0[l], iclr_a2[l],
                             gate_g2[l], k_k[l], k_a[l], r_k[l], lnx_g[l], lnx_b[l], w_out[l],
                             norm_ffn[l], w_router[l], b_router[l])
        wts["norm_mix"] = norm_mix[l]
        mod = _modulation(c_all, w_ada[l], b_ada[l]).reshape(c_rows, 6, d)
        mod_p, mod_s = mod[:bp], mod[bp:n_c]

        rp = _mix_path(hp, mod_p, pos_p, None, None, zero_wkv, zero_shift, wts,
                       jnp.zeros((1, N_EXPERTS), f32))
        rs = _mix_path(hs, mod_s, pos_s, cache_kv_latent[l], cache_k_rope[l], state_wkv[l],
                       state_shift[l], wts, rp["counts"])

        counts = rs["counts"][0].astype(i32)
        padded = (counts + tm - 1) // tm * tm
        pad_end = jnp.cumsum(padded)
        pad_start = pad_end - padded
        experts = jnp.arange(N_EXPERTS, dtype=i32)

        def slots(r):
            idx = r["idx"][..., :TOP_K]
            start = jnp.sum(jnp.where(idx[..., None] == experts, pad_start, 0), axis=-1)
            return start + r["rank"][..., :TOP_K]

        dest_p, dest_s = slots(rp), slots(rs)
        tile_row = jnp.arange(n_blocks, dtype=i32) * tm
        block_e = jnp.minimum(jnp.sum((pad_end[None, :] <= tile_row[:, None]).astype(i32), axis=1),
                              N_EXPERTS - 1)
        n_used = (pad_end[-1:] // tm).astype(i32)
        zrow = jnp.concatenate([jnp.where(padded > 0, pad_end - tm, -1), n_used]).astype(i32)

        xs = _dispatch(rp["h2"], dest_p, rs["h2"], dest_s, zrow, n_blocks * tm)
        yb = _experts(xs, block_e, n_used, w_gate_up[l], b_gate_up[l], w_down[l], b_down[l])

        last = l == depth - 1
        hp = _combine(rp["x1"], rp["gates"], mod_p, dest_p, yb, norm_final, last)
        hs = _combine(rs["x1"], rs["gates"], mod_s, dest_s, yb, norm_final, last)
        for acc, r in ((outs_p, rp), (outs_s, rs)):
            acc[0].append(r["new_lat"])
            acc[1].append(r["new_rope"])
            acc[2].append(r["new_wkv"])
            acc[3].append(r["new_shift"])

    return (hp, hs, jnp.stack(outs_p[0]), jnp.stack(outs_p[1]), jnp.stack(outs_p[2]),
            jnp.stack(outs_p[3]), jnp.stack(outs_s[0]), jnp.stack(outs_s[1]),
            jnp.stack(outs_s[2]), jnp.stack(outs_s[3]))
```

```python
import functools

import jax
import jax.numpy as jnp
from jax import lax
from jax.experimental import pallas as pl
from jax.experimental.pallas import tpu as pltpu

f32 = jnp.float32
bf16 = jnp.bfloat16
i32 = jnp.int32

D_MODEL = 1024
CHUNK = 64
MLA_HEADS = 4
QK_NOPE = 128
QK_ROPE = 64
V_HEAD = 128
Q_LORA = 256
KV_LORA = 128
ROPE_THETA = 10000.0
MLA_WIDTH = MLA_HEADS * V_HEAD
RWKV_HEAD = 64
RWKV_HEADS = 8
RWKV_WIDTH = RWKV_HEADS * RWKV_HEAD
DECAY_LORA = 64
ICLR_LORA = 64
GATE_LORA = 128
LN_X_EPS = 64e-5
MLA_COLS = Q_LORA + KV_LORA + QK_ROPE
RWKV_COLS = 3 * RWKV_WIDTH + DECAY_LORA + ICLR_LORA + GATE_LORA
N_EXPERTS = 32
TOP_K = 4
D_FF = 1024
SWIGLU_ALPHA = 1.702
SWIGLU_LIMIT = 7.0
NORM_EPS = 1e-6
LOG2_E = 1.4426950408889634

MLA_PAD = 512
QK_PAD = 256
N_PAIRS = RWKV_HEADS // 2
PAIR_W = 2 * RWKV_HEAD
ROUTE_W = 128
ROW_TILE = (8, 128)

TOKEN_TILE = 256
KV_WHOLE_MAX = 2048
ATTN_TILE = 2048
ATTN_SUB = 512
SCAN_CHUNK = 64
SCAN_STEP_CHUNKS = 4
SCAN_PASSES_GRAM = 1
SCAN_PASSES_INV = 1
SCAN_PASSES_STATE = 1
SCAN_PASSES_OUT = 1
EXPERT_TILE = 512
ROUTE_TILE = 1024
COMBINE_PARTS = 4
VMEM_LIMIT = 56 * 1024 * 1024


def _token_tile(s):
    t = TOKEN_TILE
    while s % t:
        t //= 2
    assert t >= 8, s
    return t


def _cparams(sem, vmem=None):
    return pltpu.CompilerParams(dimension_semantics=sem, vmem_limit_bytes=vmem or VMEM_LIMIT)


def _dot(a, b):
    return jnp.dot(a.astype(bf16), b.astype(bf16), preferred_element_type=f32)


def _dg(a, b, ca, cb):
    return lax.dot_general(a, b, (((ca,), (cb,)), ((), ())), preferred_element_type=f32)


def _split2(x):
    hi = x.astype(bf16)
    lo = (x - hi.astype(f32)).astype(bf16)
    return hi, lo


def _split3(x):
    hi = x.astype(bf16)
    r1 = x - hi.astype(f32)
    mid = r1.astype(bf16)
    lo = (r1 - mid.astype(f32)).astype(bf16)
    return hi, mid, lo


def _dot3(a, b, ca=1, cb=0):
    ah, al = _split2(a)
    bh, bl = _split2(b)
    return _dg(ah, bh, ca, cb) + (_dg(ah, bl, ca, cb) + _dg(al, bh, ca, cb))


def _dot_exact_lhs(a_bf, b):
    bh, bm, bl = _split3(b)
    return _dg(a_bf, bh, 1, 0) + (_dg(a_bf, bm, 1, 0) + _dg(a_bf, bl, 1, 0))


def _seg_sum(x, seg_bf):
    xh, xl = _split2(x)
    return _dg(xh, seg_bf, 1, 0) + _dg(xl, seg_bf, 1, 0)


def _rms(x, g):
    return x * lax.rsqrt(jnp.mean(x * x, axis=-1, keepdims=True) + NORM_EPS) * g


def _sigmoid(x):
    return 1.0 / (1.0 + jnp.exp(-x))


def _mod_kernel(c_ref, w_ref, b_ref, o_ref):
    c = c_ref[...]
    o_ref[...] = _dot3(c * _sigmoid(c), w_ref[...]) + b_ref[...]


def _modulation(c_all, w_ada, b_ada):
    rows, d = c_all.shape
    n = w_ada.shape[1]
    tn = 1536
    return pl.pallas_call(
        _mod_kernel,
        name="adaln_mod",
        out_shape=jax.ShapeDtypeStruct((rows, n), f32),
        grid=(n // tn,),
        in_specs=[pl.BlockSpec((rows, d), lambda j: (0, 0)),
                  pl.BlockSpec((d, tn), lambda j: (0, j)),
                  pl.BlockSpec((1, tn), lambda j: (0, j))],
        out_specs=pl.BlockSpec((rows, tn), lambda j: (0, j)),
        compiler_params=_cparams(("arbitrary",)),
    )(c_all, w_ada, b_ada.reshape(1, n))


def _project_kv(lat, rope128, wk_ref, wv_ref, k_ref, v_ref):
    lat = lat.astype(bf16)
    kn = jnp.dot(lat, wk_ref[...], preferred_element_type=f32)
    vv = jnp.dot(lat, wv_ref[...], preferred_element_type=f32)
    rope = rope128.astype(bf16)
    for h in range(MLA_HEADS):
        k_ref[0, h, :, :QK_NOPE] = kn[:, h * QK_NOPE:(h + 1) * QK_NOPE].astype(bf16)
        k_ref[0, h, :, QK_NOPE:] = rope
        v_ref[0, h] = vv[:, h * V_HEAD:(h + 1) * V_HEAD].astype(bf16)


def _front_kernel(*refs, with_kv):
    (x_ref, mod_ref, nw_ref, w_ref, qn_ref, wa_ref, wb_ref, kvn_ref, ct_ref, st_ref, kt_ref,
     wk_ref, wv_ref, sp_ref, mu_ref, w0_ref, w2_ref, a0_ref, a2_ref, g2_ref, kk_ref, ka_ref,
     rk_ref, seg_ref) = refs[:24]
    outs = refs[24:-1]
    carry = refs[-1]
    q_ref, lat_ref, rope_ref, rope128_ref = outs[:4]
    outs = outs[4:]
    if with_kv:
        k_ref, v_ref = outs[:2]
        outs = outs[2:]
    r_ref, lw_ref, kh_ref, vv_ref, al_ref, be_ref, g_ref, bo_ref, ns_ref = outs

    h = _rms(x_ref[0], nw_ref[...])
    h = h * (1.0 + mod_ref[0, 1:2, :]) + mod_ref[0, 0:1, :]
    proj = jnp.dot(h.astype(bf16), w_ref[...], preferred_element_type=f32)
    pm = proj[:, :MLA_PAD]
    p = proj[:, MLA_PAD:]

    qn = _rms(pm[:, :Q_LORA], qn_ref[...]).astype(bf16)
    qa = jnp.dot(qn, wa_ref[...], preferred_element_type=f32)
    qb = jnp.dot(qn, wb_ref[...], preferred_element_type=f32)
    ct = ct_ref[...]
    st = st_ref[...]
    for hd in range(MLA_HEADS):
        sl = slice(hd * QK_PAD, (hd + 1) * QK_PAD)
        q_ref[0, hd] = (qa[:, sl] * ct + qb[:, sl] * st).astype(bf16)
    lat = _rms(pm[:, Q_LORA:Q_LORA + KV_LORA], kvn_ref[...])
    lat_ref[0] = lat
    t = pm[:, Q_LORA + KV_LORA:] * kt_ref[...]
    kr = t + pltpu.roll(t, QK_ROPE, axis=1)
    lane = lax.broadcasted_iota(i32, kr.shape, 1)
    rope128 = jnp.where(lane < QK_ROPE, kr, 0.0)
    rope128_ref[0] = rope128
    rope_ref[0] = kr[:, :QK_ROPE]
    if with_kv:
        _project_kv(lat, rope128, wk_ref, wv_ref, k_ref, v_ref)

    @pl.when(pl.program_id(1) == 0)
    def _():
        carry[...] = sp_ref[0]

    ts = p.shape[0]
    row = lax.broadcasted_iota(i32, p.shape, 0)
    prev = jnp.where(row == 0, carry[...], pltpu.roll(p, 1, axis=0))
    last = p[ts - 1:ts, :]
    carry[...] = last
    ns_ref[0] = last
    xs = p + (prev - p) * mu_ref[...]
    w = RWKV_WIDTH
    r, k, v = xs[:, :w], xs[:, w:2 * w], xs[:, 2 * w:3 * w]
    wa = xs[:, 3 * w:3 * w + DECAY_LORA + ICLR_LORA]
    gd = xs[:, 3 * w + DECAY_LORA + ICLR_LORA:]
    z = -(w0_ref[...] + _dot3(jnp.tanh(wa), w2_ref[...]))
    softplus = jnp.maximum(z, 0.0) + jnp.log(1.0 + jnp.exp(-jnp.abs(z)))
    lw = -jnp.exp(-softplus - 0.5)
    a = _sigmoid(a0_ref[...] + _dot3(wa, a2_ref[...]))
    g = _dot3(_sigmoid(gd), g2_ref[...])
    seg = seg_ref[...]
    kk = k * kk_ref[...]
    kk = kk * lax.rsqrt(jnp.maximum(_seg_sum(kk * kk, seg), 1e-24))
    kh = k * (1.0 + (a - 1.0) * ka_ref[...])
    r_ref[0] = r
    lw_ref[0] = lw
    kh_ref[0] = kh
    vv_ref[0] = v
    al_ref[0] = -kk
    be_ref[0] = kk * a
    g_ref[0] = g
    bo_ref[0] = _seg_sum(r * kh * rk_ref[...], seg) * v


def _front(x, mod, shift_prev, wts, tables, with_kv):
    b, s, d = x.shape
    ts = _token_tile(s)
    n_in = wts["w_in_bf"].shape[1]
    hq = MLA_HEADS * QK_PAD
    n = RWKV_COLS
    w = RWKV_WIDTH
    ct, st, kt = tables
    full = lambda r, c: pl.BlockSpec((r, c), lambda i, j: (0, 0))
    tok = lambda width: pl.BlockSpec((1, ts, width), lambda i, j: (i, j, 0))
    heads = lambda width: pl.BlockSpec((1, MLA_HEADS, ts, width), lambda i, j: (i, 0, j, 0))
    in_specs = [tok(d), pl.BlockSpec((1, 6, d), lambda i, j: (i, 0, 0)), full(1, d), full(d, n_in),
                full(1, Q_LORA), full(Q_LORA, hq), full(Q_LORA, hq), full(1, KV_LORA),
                pl.BlockSpec((ts, QK_PAD), lambda i, j: (j, 0)),
                pl.BlockSpec((ts, QK_PAD), lambda i, j: (j, 0)),
                pl.BlockSpec((ts, 2 * QK_ROPE), lambda i, j: (j, 0)),
                full(KV_LORA, MLA_HEADS * QK_NOPE), full(KV_LORA, MLA_HEADS * V_HEAD),
                pl.BlockSpec((1, 1, n), lambda i, j: (i, 0, 0)),
                full(1, n), full(1, w), full(PAIR_W, w), full(1, w), full(PAIR_W, w),
                full(GATE_LORA, w), full(1, w), full(1, w), full(1, w), full(w, w)]
    out_shape = [jax.ShapeDtypeStruct((b, MLA_HEADS, s, QK_PAD), bf16),
                 jax.ShapeDtypeStruct((b, s, KV_LORA), f32),
                 jax.ShapeDtypeStruct((b, s, QK_ROPE), f32),
                 jax.ShapeDtypeStruct((b, s, 2 * QK_ROPE), f32)]
    out_specs = [heads(QK_PAD), tok(KV_LORA), tok(QK_ROPE), tok(2 * QK_ROPE)]
    if with_kv:
        out_shape += [jax.ShapeDtypeStruct((b, MLA_HEADS, s, QK_PAD), bf16),
                      jax.ShapeDtypeStruct((b, MLA_HEADS, s, V_HEAD), bf16)]
        out_specs += [heads(QK_PAD), heads(V_HEAD)]
    out_shape += [jax.ShapeDtypeStruct((b, s, w), f32)] * 8 + [jax.ShapeDtypeStruct((b, 1, n), f32)]
    out_specs += [tok(w)] * 8 + [pl.BlockSpec((1, 1, n), lambda i, j: (i, 0, 0))]
    outs = pl.pallas_call(
        functools.partial(_front_kernel, with_kv=with_kv),
        name="front",
        out_shape=tuple(out_shape),
        grid=(b, s // ts),
        in_specs=in_specs,
        out_specs=tuple(out_specs),
        scratch_shapes=[pltpu.VMEM((1, n), f32)],
        compiler_params=_cparams(("arbitrary", "arbitrary")),
    )(x, mod, wts["norm_mix"].reshape(1, d), wts["w_in_bf"], wts["q_norm"].reshape(1, -1),
      wts["w_qa"], wts["w_qb"], wts["kv_norm"].reshape(1, -1), ct, st, kt, wts["w_kn"], wts["w_v"],
      shift_prev.reshape(b, 1, n), wts["shift_mu"], wts["decay_w0"], wts["decay_w2p"],
      wts["iclr_a0"], wts["iclr_a2p"], wts["gate_g2"], wts["k_k"], wts["k_a"], wts["r_k"],
      wts["seg"])
    outs = list(outs)
    q, new_lat, new_rope, rope128 = outs[:4]
    kv = tuple(outs[4:6]) if with_kv else None
    rest = outs[6:] if with_kv else outs[4:]
    return q, new_lat, new_rope, rope128, kv, tuple(rest[:8]), rest[8].reshape(b, n)


def _kv_proj_kernel(lat_ref, rope_ref, wk_ref, wv_ref, k_ref, v_ref):
    _project_kv(lat_ref[0], rope_ref[0], wk_ref, wv_ref, k_ref, v_ref)


def _kv_proj(lat, rope128, w_kn, w_v):
    b, s, _ = lat.shape
    ts = s if s <= KV_WHOLE_MAX else _token_tile(s)
    return pl.pallas_call(
        _kv_proj_kernel,
        name="kv_proj",
        out_shape=(jax.ShapeDtypeStruct((b, MLA_HEADS, s, QK_PAD), bf16),
                   jax.ShapeDtypeStruct((b, MLA_HEADS, s, V_HEAD), bf16)),
        grid=(b, s // ts),
        in_specs=[pl.BlockSpec((1, ts, KV_LORA), lambda i, j: (i, j, 0)),
                  pl.BlockSpec((1, ts, 2 * QK_ROPE), lambda i, j: (i, j, 0)),
                  pl.BlockSpec((KV_LORA, MLA_HEADS * QK_NOPE), lambda i, j: (0, 0)),
                  pl.BlockSpec((KV_LORA, MLA_HEADS * V_HEAD), lambda i, j: (0, 0))],
        out_specs=(pl.BlockSpec((1, MLA_HEADS, ts, QK_PAD), lambda i, j: (i, 0, j, 0)),
                   pl.BlockSpec((1, MLA_HEADS, ts, V_HEAD), lambda i, j: (i, 0, j, 0))),
        compiler_params=_cparams(("arbitrary", "arbitrary")),
    )(lat, rope128, w_kn, w_v)


def _flash_kernel(qi_ref, kj_ref, q_ref, k_ref, v_ref, o_ref, m_scr, l_scr, acc_scr, *, tile, sub):
    step = pl.program_id(1)
    qi = qi_ref[step]
    kj = kj_ref[step]
    lanes = m_scr.shape[1]

    @pl.when(kj == 0)
    def _():
        m_scr[...] = jnp.full(m_scr.shape, -jnp.inf, f32)
        l_scr[...] = jnp.zeros(l_scr.shape, f32)
        acc_scr[...] = jnp.zeros(acc_scr.shape, f32)

    def update(c, r0, masked):
        rows = pl.ds(r0, tile - r0)
        keys = pl.ds(c * sub, sub)
        s = _dg(q_ref[0, 0, rows, :], k_ref[0, 0, keys, :], 1, 1)
        if masked:
            qpos = r0 + lax.broadcasted_iota(i32, s.shape, 0)
            kpos = c * sub + lax.broadcasted_iota(i32, s.shape, 1)
            s = jnp.where(kpos < (qpos // CHUNK + 1) * CHUNK, s, -jnp.inf)
        m_prev = m_scr[rows, :]
        m_new = jnp.maximum(m_prev, jnp.max(s, axis=1, keepdims=True))
        alpha = jnp.exp2(m_prev - m_new)
        p = jnp.exp2(s - jnp.concatenate([m_new] * (sub // lanes), axis=1))
        psum = p[:, :lanes]
        for t in range(1, sub // lanes):
            psum = psum + p[:, t * lanes:(t + 1) * lanes]
        l_scr[rows, :] = alpha * l_scr[rows, :] + psum
        acc_scr[rows, :] = alpha * acc_scr[rows, :] + jnp.dot(
            p.astype(bf16), v_ref[0, 0, keys, :], preferred_element_type=f32)
        m_scr[rows, :] = m_new

    n_sub = tile // sub

    @pl.when(kj < qi)
    def _():
        for c in range(n_sub):
            update(c, 0, False)

    @pl.when(kj == qi)
    def _():
        for c in range(n_sub):
            update(c, c * sub, True)
        o_ref[0] = acc_scr[...] / jnp.sum(l_scr[...], axis=1, keepdims=True)


def _flash_attention(q, k, v):
    b, nh, s, _ = q.shape
    assert b == 1
    tile = min(ATTN_TILE, s)
    sub = min(ATTN_SUB, tile)
    assert sub % CHUNK == 0 and tile % sub == 0
    nt = s // tile
    qi = [i for i in range(nt) for _ in range(i + 1)]
    kj = [j for i in range(nt) for j in range(i + 1)]
    grid_spec = pltpu.PrefetchScalarGridSpec(
        num_scalar_prefetch=2,
        grid=(nh, len(qi)),
        in_specs=[pl.BlockSpec((1, 1, tile, QK_PAD), lambda h, t, qi, kj: (0, h, qi[t], 0)),
                  pl.BlockSpec((1, 1, tile, QK_PAD), lambda h, t, qi, kj: (0, h, kj[t], 0)),
                  pl.BlockSpec((1, 1, tile, V_HEAD), lambda h, t, qi, kj: (0, h, kj[t], 0))],
        out_specs=pl.BlockSpec((1, tile, V_HEAD), lambda h, t, qi, kj: (0, qi[t], h)),
        scratch_shapes=[pltpu.VMEM((tile, V_HEAD), f32), pltpu.VMEM((tile, V_HEAD), f32),
                        pltpu.VMEM((tile, V_HEAD), f32)])
    return pl.pallas_call(
        functools.partial(_flash_kernel, tile=tile, sub=sub),
        name="flash_attn",
        out_shape=jax.ShapeDtypeStruct((1, s, MLA_WIDTH), f32),
        grid_spec=grid_spec,
        compiler_params=_cparams(("arbitrary", "arbitrary")),
    )(jnp.asarray(qi, i32), jnp.asarray(kj, i32), q, k, v)


def _cached_attn_kernel(q_ref, k_ref, v_ref, o_ref, *, past, n_keys):
    for h in range(MLA_HEADS):
        s = _dg(q_ref[0, h], k_ref[0, h], 1, 1)
        qpos = past + lax.broadcasted_iota(i32, s.shape, 0)
        kpos = lax.broadcasted_iota(i32, s.shape, 1)
        limit = (qpos // CHUNK + 1) * CHUNK
        s = jnp.where((kpos < limit) & (kpos < n_keys), s, -jnp.inf)
        m = jnp.max(s, axis=1, keepdims=True)
        p = jnp.exp2(s - m)
        o = jnp.dot(p.astype(bf16), v_ref[0, h], preferred_element_type=f32)
        o_ref[0, :, h * V_HEAD:(h + 1) * V_HEAD] = o / jnp.sum(p, axis=1, keepdims=True)


def _cached_attention(q, k, v, past, n_keys):
    b, nh, sq, _ = q.shape
    sk = k.shape[2]
    return pl.pallas_call(
        functools.partial(_cached_attn_kernel, past=past, n_keys=n_keys),
        name="cached_attn",
        out_shape=jax.ShapeDtypeStruct((b, sq, MLA_WIDTH), f32),
        grid=(b,),
        in_specs=[pl.BlockSpec((1, nh, sq, QK_PAD), lambda i: (i, 0, 0, 0)),
                  pl.BlockSpec((1, nh, sk, QK_PAD), lambda i: (i, 0, 0, 0)),
                  pl.BlockSpec((1, nh, sk, V_HEAD), lambda i: (i, 0, 0, 0))],
        out_specs=pl.BlockSpec((1, sq, MLA_WIDTH), lambda i: (i, 0, 0)),
        compiler_params=_cparams(("arbitrary",)),
    )(q, k, v)


def _mm(a, b, ca, cb, passes):
    if passes == 1:
        return _dg(a.astype(bf16), b.astype(bf16), ca, cb)
    return _dot3(a, b, ca, cb)


def _scan_kernel(r_ref, lw_ref, k_ref, v_ref, al_ref, be_ref, s0_ref, y_ref, sf_ref, st, *, c):
    @pl.when(pl.program_id(1) == 0)
    def _():
        st[...] = s0_ref[0]

    rows = r_ref.shape[1]
    nch = rows // c
    c2 = 2 * c
    rowb = lax.broadcasted_iota(i32, (rows, rows), 0)
    colb = lax.broadcasted_iota(i32, (rows, rows), 1)
    same_chunk = (rowb // c) == (colb // c)

    lw = lw_ref[0]
    cum = _dot_exact_lhs((same_chunk & (colb <= rowb)).astype(bf16), lw)
    tot = _dot_exact_lhs(same_chunk.astype(bf16), lw)
    g_inv = jnp.exp(-cum)
    g_tail = jnp.exp(tot - cum)
    r_t = r_ref[0] * jnp.exp(cum)
    a_t = al_ref[0] * jnp.exp(cum - lw)
    b_t = be_ref[0] * g_inv
    k_t = k_ref[0] * g_inv
    b_c = be_ref[0] * g_tail
    k_c = k_ref[0] * g_tail
    v_all = v_ref[0]

    ri = lax.broadcasted_iota(i32, (c2, c2), 0)
    ci = lax.broadcasted_iota(i32, (c2, c2), 1)
    same = (ri // c) == (ci // c)
    strict = same & (ci < ri)
    incl = same & (ci <= ri)
    eye = (ri == ci).astype(f32)
    head0 = lax.broadcasted_iota(i32, (c, PAIR_W), 1) < RWKV_HEAD
    ones = jnp.ones((c, PAIR_W), bf16)

    def stack(x):
        return jnp.concatenate([jnp.where(head0, x, 0.0), jnp.where(head0, 0.0, x)], axis=0)

    chains = [(j, p) for j in range(nch) for p in range(N_PAIRS)]

    def part(t, j, p):
        return t[j * c:(j + 1) * c, p * PAIR_W:(p + 1) * PAIR_W]

    ops = {}
    for j, p in chains:
        ops[j, p] = {n: stack(part(t, j, p)) for n, t in
                     (("a", a_t), ("r", r_t), ("b", b_t), ("k", k_t), ("v", v_all),
                      ("bc", b_c), ("kc", k_c))}

    gram = {jp: _mm(jnp.concatenate([o["a"], o["r"]], axis=0),
                    jnp.concatenate([o["b"], o["k"]], axis=0), 1, 1, SCAN_PASSES_GRAM)
            for jp, o in ops.items()}
    a_ab = {jp: jnp.where(strict, g[:c2, :c2], 0.0) for jp, g in gram.items()}
    a_ak = {jp: jnp.where(strict, g[:c2, c2:], 0.0) for jp, g in gram.items()}
    a_rb = {jp: jnp.where(incl, g[c2:, :c2], 0.0) for jp, g in gram.items()}
    a_rk = {jp: jnp.where(incl, g[c2:, c2:], 0.0) for jp, g in gram.items()}
    t_inv = {jp: eye + low for jp, low in a_ab.items()}
    pw = dict(a_ab)
    for _ in range(max(c.bit_length() - 2, 0)):
        pw = {jp: _mm(m, m, 1, 0, SCAN_PASSES_INV) for jp, m in pw.items()}
        t_inv = {jp: t_inv[jp] + _mm(t_inv[jp], pw[jp], 1, 0, SCAN_PASSES_INV) for jp in pw}
    decay = {}
    for j, p in chains:
        lh, lm, ll = _split3(part(lw, j, p))
        decay[j, p] = jnp.exp(_dg(lh, ones, 0, 0) + (_dg(lm, ones, 0, 0) + _dg(ll, ones, 0, 0)))

    states = [st[p] for p in range(N_PAIRS)]
    y_rows = []
    prs = range(N_PAIRS)
    for j in range(nch):
        x = [_mm(jnp.concatenate([ops[j, p]["a"], a_ak[j, p]], axis=1),
                 jnp.concatenate([states[p], ops[j, p]["v"]], axis=0), 1, 0, SCAN_PASSES_STATE)
             for p in prs]
        u = [_mm(t_inv[j, p], x[p], 1, 0, SCAN_PASSES_STATE) for p in prs]
        upd = [_mm(jnp.concatenate([ops[j, p]["bc"], ops[j, p]["kc"]], axis=0),
                   jnp.concatenate([u[p], ops[j, p]["v"]], axis=0), 0, 0, SCAN_PASSES_STATE)
               for p in prs]
        y = [_mm(jnp.concatenate([ops[j, p]["r"], a_rb[j, p], a_rk[j, p]], axis=1),
                 jnp.concatenate([states[p], u[p], ops[j, p]["v"]], axis=0), 1, 0, SCAN_PASSES_OUT)
             for p in prs]
        states = [states[p] * decay[j, p] + upd[p] for p in prs]
        y_rows.append(jnp.concatenate([yp[:c] + yp[c:] for yp in y], axis=1))

    y_ref[0] = jnp.concatenate(y_rows, axis=0)
    for p in prs:
        st[p] = states[p]
        sf_ref[0, p] = states[p]


def _scan_rows(s):
    return SCAN_CHUNK * min(SCAN_STEP_CHUNKS, -(-s // SCAN_CHUNK))


def _rwkv_scan(seqs, s0_pairs):
    r, lw, kh, v, al, be = seqs
    b, s, w = r.shape
    rows = _scan_rows(s)
    tok = pl.BlockSpec((1, rows, w), lambda i, j: (i, j, 0))
    stt = pl.BlockSpec((1, N_PAIRS, PAIR_W, PAIR_W), lambda i, j: (i, 0, 0, 0))
    return pl.pallas_call(
        functools.partial(_scan_kernel, c=SCAN_CHUNK),
        name="rwkv_scan",
        out_shape=(jax.ShapeDtypeStruct((b, s, w), f32),
                   jax.ShapeDtypeStruct((b, N_PAIRS, PAIR_W, PAIR_W), f32)),
        grid=(b, s // rows),
        in_specs=[tok] * 6 + [stt],
        out_specs=(tok, stt),
        scratch_shapes=[pltpu.VMEM((N_PAIRS, PAIR_W, PAIR_W), f32)],
        compiler_params=_cparams(("arbitrary", "arbitrary")),
    )(r, lw, kh, v, al, be, s0_pairs)


def _outproj_kernel(om_ref, y_ref, bo_ref, g_ref, x_ref, mod_ref, mn_ref, lng_ref, lnb_ref,
                    seg_ref, wo_ref, nf_ref, wr_ref, br_ref, tri_ref, cnt0_ref,
                    x1_ref, h2_ref, gate_ref, idx_ref, rank_ref, cnt_ref, run):
    @pl.when((pl.program_id(0) == 0) & (pl.program_id(1) == 0))
    def _():
        run[...] = cnt0_ref[...]

    y = y_ref[0]
    seg = seg_ref[...]
    inv_n = 1.0 / RWKV_HEAD
    yc = y - _seg_sum(y, seg) * inv_n
    var = _seg_sum(yc * yc, seg) * inv_n
    o_rw = (yc * lax.rsqrt(var + LN_X_EPS) * lng_ref[...] + lnb_ref[...] + bo_ref[0]) * g_ref[0]

    om = _rms(om_ref[0], mn_ref[...])
    mix = (jnp.dot(om.astype(bf16), wo_ref[:MLA_WIDTH, :], preferred_element_type=f32)
           + jnp.dot(o_rw.astype(bf16), wo_ref[MLA_WIDTH:, :], preferred_element_type=f32))
    x1 = x_ref[0] + mod_ref[0, 2:3, :] * mix
    x1_ref[0] = x1
    h2 = _rms(x1, nf_ref[...]) * (1.0 + mod_ref[0, 4:5, :]) + mod_ref[0, 3:4, :]
    h2_ref[0] = h2

    logits = _dot3(h2, wr_ref[...]) + br_ref[...]
    ts = logits.shape[0]
    lane_f = lax.broadcasted_iota(i32, logits.shape, 1).astype(f32)
    lane_w = lax.broadcasted_iota(i32, (ts, ROUTE_W), 1)
    tri = tri_ref[...]
    base = run[...]
    gates = jnp.zeros((ts, ROUTE_W), f32)
    idxs = jnp.zeros((ts, ROUTE_W), i32)
    ranks = jnp.zeros((ts, ROUTE_W), i32)
    vals = []
    work = logits
    for k in range(TOP_K):
        m = jnp.max(work, axis=1, keepdims=True)
        sel_f = jnp.min(jnp.where(work == m, lane_f, float(N_EXPERTS)), axis=1, keepdims=True)
        hit = lane_f == sel_f
        sel = sel_f.astype(i32)
        work = jnp.where(hit, -jnp.inf, work)
        vals.append(m)
        onehot = hit.astype(bf16)
        before = jnp.dot(tri, onehot, preferred_element_type=f32)
        rank = jnp.sum(jnp.where(hit, before + base, 0.0), axis=1, keepdims=True)
        base = base + jnp.sum(hit.astype(f32), axis=0, keepdims=True)
        idxs = jnp.where(lane_w == k, sel, idxs)
        ranks = jnp.where(lane_w == k, rank.astype(i32), ranks)
    run[...] = base
    es = [jnp.exp(vk - vals[0]) for vk in vals]
    inv = 1.0 / (es[0] + es[1] + es[2] + es[3])
    for k in range(TOP_K):
        gates = jnp.where(lane_w == k, es[k] * inv, gates)
    gate_ref[0] = gates
    idx_ref[0] = idxs
    rank_ref[0] = ranks
    cnt_ref[...] = base


def _outproj(o_mla, y_rw, bonus, gate_rw, x, mod, wts, counts0):
    b, s, d = x.shape
    w = RWKV_WIDTH
    ts = _token_tile(s)
    tok = lambda width: pl.BlockSpec((1, ts, width), lambda i, j: (i, j, 0))
    full = lambda r, c: pl.BlockSpec((r, c), lambda i, j: (0, 0))
    tri = (lax.broadcasted_iota(i32, (ts, ts), 1) < lax.broadcasted_iota(i32, (ts, ts), 0)
           ).astype(bf16)
    return pl.pallas_call(
        _outproj_kernel,
        name="outproj_route",
        out_shape=(jax.ShapeDtypeStruct((b, s, d), f32),
                   jax.ShapeDtypeStruct((b, s, d), f32),
                   jax.ShapeDtypeStruct((b, s, ROUTE_W), f32),
                   jax.ShapeDtypeStruct((b, s, ROUTE_W), i32),
                   jax.ShapeDtypeStruct((b, s, ROUTE_W), i32),
                   jax.ShapeDtypeStruct((1, N_EXPERTS), f32)),
        grid=(b, s // ts),
        in_specs=[tok(MLA_WIDTH), tok(w), tok(w), tok(w), tok(d),
                  pl.BlockSpec((1, 6, d), lambda i, j: (i, 0, 0)),
                  full(1, MLA_WIDTH), full(1, w), full(1, w), full(w, w),
                  full(d, d), full(1, d), full(d, N_EXPERTS),
                  full(1, N_EXPERTS), full(ts, ts), full(1, N_EXPERTS)],
        out_specs=(tok(d), tok(d), tok(ROUTE_W), tok(ROUTE_W), tok(ROUTE_W),
                   full(1, N_EXPERTS)),
        scratch_shapes=[pltpu.VMEM((1, N_EXPERTS), f32)],
        compiler_params=_cparams(("arbitrary", "arbitrary")),
    )(o_mla, y_rw, bonus, gate_rw, x, mod, wts["mla_out_norm"], wts["lnx_g"], wts["lnx_b"],
      wts["seg"], wts["w_out_bf"], wts["norm_ffn"], wts["w_router"], wts["b_router"], tri, counts0)


def _to_row_tiles(tiles_ref, x, first=0):
    sub, lanes = ROW_TILE
    for c in range(sub):
        tiles_ref[pl.ds(first * sub + c, x.shape[0], stride=sub), :] = x[:, c * lanes:(c + 1) * lanes]


def _from_row_tiles(tiles_ref, n, first=0):
    sub = ROW_TILE[0]
    return jnp.concatenate([tiles_ref[pl.ds(first * sub + c, n, stride=sub), :]
                            for c in range(sub)], axis=1)


def _row_tile(ref, t):
    sub = ROW_TILE[0]
    return ref.at[pl.ds(pl.multiple_of(t * sub, sub), sub), :]


def _scatter_rows(dest_ref, h_ref, hb, xs_ref, sem):
    tt = h_ref.shape[1]
    _to_row_tiles(hb, h_ref[0])

    def row_copy(t, d):
        return pltpu.make_async_copy(_row_tile(hb, t), _row_tile(xs_ref, d), sem)

    def issue(t, carry):
        for k in range(TOP_K):
            row_copy(t, dest_ref[0, 0, TOP_K * t + k]).start(priority=k % 2)
        return carry

    def drain(t, carry):
        for k in range(TOP_K):
            row_copy(0, 0).wait()
        return carry

    lax.fori_loop(0, tt, issue, 0)
    lax.fori_loop(0, tt, drain, 0)


def _dispatch_kernel(zrow_ref, dest_ref, h_ref, dest2_ref, h2_ref, xs_ref, zeros, hb, hb2, sem,
                     zsem):
    sub = ROW_TILE[0]
    tm = zeros.shape[0] // sub

    @pl.when(pl.program_id(0) == 0)
    def _():
        zeros[...] = jnp.zeros(zeros.shape, f32)

        def tile_copy(row):
            first = pl.multiple_of(row * sub, tm * sub)
            return pltpu.make_async_copy(zeros, xs_ref.at[pl.ds(first, tm * sub), :], zsem)

        def issue(e, carry):
            @pl.when(zrow_ref[e] >= 0)
            def _():
                tile_copy(zrow_ref[e]).start()
            return carry

        def drain(e, carry):
            @pl.when(zrow_ref[e] >= 0)
            def _():
                tile_copy(0).wait()
            return carry

        lax.fori_loop(0, N_EXPERTS, issue, 0)
        lax.fori_loop(0, N_EXPERTS, drain, 0)

        n_used = zrow_ref[N_EXPERTS]
        n_tiles = xs_ref.shape[0] // (tm * sub)

        def issue_tail(i, carry):
            tile_copy(i * tm).start()
            return carry

        def drain_tail(i, carry):
            tile_copy(0).wait()
            return carry

        lax.fori_loop(n_used, n_tiles, issue_tail, 0)
        lax.fori_loop(n_used, n_tiles, drain_tail, 0)

    _scatter_rows(dest_ref, h_ref, hb, xs_ref, sem)

    @pl.when(pl.program_id(0) == pl.num_programs(0) - 1)
    def _():
        _scatter_rows(dest2_ref, h2_ref, hb2, xs_ref, sem)


def _route_tile(s):
    t = ROUTE_TILE
    while s % t:
        t //= 2
    assert t >= 8, s
    return t


def _dispatch(h_big, dest_big, h_small, dest_small, zrow, n_rows):
    d = h_big.shape[-1]
    sub, lanes = ROW_TILE
    assert d == sub * lanes
    h_big = h_big.reshape(1, -1, d)
    h_small = h_small.reshape(1, -1, d)
    s, s2 = h_big.shape[1], h_small.shape[1]
    tt = _route_tile(s)
    nt = s // tt
    grid_spec = pltpu.PrefetchScalarGridSpec(
        num_scalar_prefetch=1,
        grid=(nt,),
        in_specs=[pl.BlockSpec((1, 1, tt * TOP_K), lambda j, z: (j, 0, 0),
                               memory_space=pltpu.SMEM),
                  pl.BlockSpec((1, tt, d), lambda j, z: (0, j, 0)),
                  pl.BlockSpec((1, 1, s2 * TOP_K), lambda j, z: (0, 0, 0),
                               memory_space=pltpu.SMEM),
                  pl.BlockSpec((1, s2, d), lambda j, z: (0, 0, 0))],
        out_specs=pl.BlockSpec(memory_space=pl.ANY),
        scratch_shapes=[pltpu.VMEM((EXPERT_TILE * sub, lanes), f32),
                        pltpu.VMEM((tt * sub, lanes), f32), pltpu.VMEM((s2 * sub, lanes), f32),
                        pltpu.SemaphoreType.DMA(()), pltpu.SemaphoreType.DMA(())])
    return pl.pallas_call(
        _dispatch_kernel,
        name="moe_dispatch",
        out_shape=jax.ShapeDtypeStruct((n_rows * sub, lanes), f32),
        grid_spec=grid_spec,
        compiler_params=_cparams(("arbitrary",)),
    )(zrow, dest_big.reshape(nt, 1, tt * TOP_K), h_big, dest_small.reshape(1, 1, s2 * TOP_K),
      h_small)


def _expert_kernel(be_ref, nu_ref, x_ref, wgu_ref, bgu_ref, wd_ref, bd_ref, y_ref,
                   wgu_bf, wd_bf):
    i = pl.program_id(0)
    prev = be_ref[jnp.maximum(i - 1, 0)]
    fresh = (i == 0) | (be_ref[i] != prev)

    @pl.when(fresh & (i < nu_ref[0]))
    def _():
        wgu_bf[...] = wgu_ref[0].astype(bf16)
        wd_bf[...] = wd_ref[0].astype(bf16)

    @pl.when(i < nu_ref[0])
    def _():
        x = _from_row_tiles(x_ref, x_ref.shape[0] // ROW_TILE[0]).astype(bf16)
        gu = jnp.dot(x, wgu_bf[...], preferred_element_type=f32) + bgu_ref[0]
        gate = jnp.minimum(gu[:, :D_FF], SWIGLU_LIMIT)
        up = jnp.clip(gu[:, D_FF:], -SWIGLU_LIMIT, SWIGLU_LIMIT)
        act = (up + 1.0) * (gate * _sigmoid(SWIGLU_ALPHA * gate))
        _to_row_tiles(y_ref, jnp.dot(act.astype(bf16), wd_bf[...], preferred_element_type=f32)
                      + bd_ref[0])

    @pl.when(i >= nu_ref[0])
    def _():
        y_ref[...] = jnp.zeros(y_ref.shape, f32)


def _experts(xs, block_e, n_used, w_gate_up, b_gate_up, w_down, b_down):
    sub, lanes = ROW_TILE
    n_rows = xs.shape[0] // sub
    d = w_down.shape[-1]
    tm = EXPERT_TILE
    nb = n_rows // tm
    last = lambda i, be, nu: jnp.minimum(i, nu[0] - 1)
    grid_spec = pltpu.PrefetchScalarGridSpec(
        num_scalar_prefetch=2,
        grid=(nb,),
        in_specs=[pl.BlockSpec((tm * sub, lanes), lambda i, be, nu: (last(i, be, nu), 0)),
                  pl.BlockSpec((1, d, 2 * D_FF), lambda i, be, nu: (be[last(i, be, nu)], 0, 0)),
                  pl.BlockSpec((1, 1, 2 * D_FF), lambda i, be, nu: (be[last(i, be, nu)], 0, 0)),
                  pl.BlockSpec((1, D_FF, d), lambda i, be, nu: (be[last(i, be, nu)], 0, 0)),
                  pl.BlockSpec((1, 1, d), lambda i, be, nu: (be[last(i, be, nu)], 0, 0))],
        out_specs=pl.BlockSpec((tm * sub, lanes), lambda i, be, nu: (i, 0)),
        scratch_shapes=[pltpu.VMEM((d, 2 * D_FF), bf16), pltpu.VMEM((D_FF, d), bf16)])
    return pl.pallas_call(
        _expert_kernel,
        name="moe_experts",
        out_shape=jax.ShapeDtypeStruct((n_rows * sub, lanes), f32),
        grid_spec=grid_spec,
        compiler_params=_cparams(("arbitrary",)),
    )(block_e, n_used, xs, w_gate_up, b_gate_up.reshape(N_EXPERTS, 1, -1), w_down,
      b_down.reshape(N_EXPERTS, 1, -1))


def _combine_kernel(dest_ref, x1_ref, gate_ref, mod_ref, nf_ref, yb_ref, o_ref, buf, sems, *,
                    final, parts):
    tt = x1_ref.shape[1]
    pt = tt // parts

    def row_copy(t, k, d, q):
        return pltpu.make_async_copy(_row_tile(yb_ref, d), _row_tile(buf, k * tt + t), sems.at[q])

    for q in range(parts):
        def issue(t, carry, q=q):
            for k in range(TOP_K):
                row_copy(t, k, dest_ref[0, 0, TOP_K * t + k], q).start(priority=k % 2)
            return carry
        lax.fori_loop(q * pt, (q + 1) * pt, issue, 0)

    for q in range(parts):
        def drain(t, carry, q=q):
            for k in range(TOP_K):
                row_copy(0, 0, 0, q).wait()
            return carry
        lax.fori_loop(0, pt, drain, 0)
        rows = pl.ds(q * pt, pt)
        gates = gate_ref[0, rows, :]
        y = gates[:, 0:1] * _from_row_tiles(buf, pt, q * pt)
        for k in range(1, TOP_K):
            y = y + gates[:, k:k + 1] * _from_row_tiles(buf, pt, k * tt + q * pt)
        x = x1_ref[0, rows, :] + mod_ref[0, 5:6, :] * y
        o_ref[0, rows, :] = _rms(x, nf_ref[...]) if final else x


def _combine(x1, gates, mod, dest, yb, norm_final, final):
    b, s, d = x1.shape
    tt = _route_tile(s)
    nt = s // tt
    parts = min(COMBINE_PARTS, tt // 8)
    dest_t = dest.reshape(b * nt, 1, tt * TOP_K)
    return pl.pallas_call(
        functools.partial(_combine_kernel, final=final, parts=parts),
        name="moe_combine",
        out_shape=jax.ShapeDtypeStruct((b, s, d), f32),
        grid=(b, nt),
        in_specs=[pl.BlockSpec((1, 1, tt * TOP_K), lambda i, j: (i * nt + j, 0, 0),
                               memory_space=pltpu.SMEM),
                  pl.BlockSpec((1, tt, d), lambda i, j: (i, j, 0)),
                  pl.BlockSpec((1, tt, ROUTE_W), lambda i, j: (i, j, 0)),
                  pl.BlockSpec((1, 6, d), lambda i, j: (i, 0, 0)),
                  pl.BlockSpec((1, d), lambda i, j: (0, 0)),
                  pl.BlockSpec(memory_space=pl.ANY)],
        out_specs=pl.BlockSpec((1, tt, d), lambda i, j: (i, j, 0)),
        scratch_shapes=[pltpu.VMEM((TOP_K * tt * ROW_TILE[0], ROW_TILE[1]), f32),
                        pltpu.SemaphoreType.DMA((parts,))],
        compiler_params=_cparams(("arbitrary", "arbitrary")),
    )(dest_t, x1, gates, mod, norm_final.reshape(1, d), yb)


def _rot_half_cols(w):
    half = w.shape[-1] // 2
    return jnp.concatenate([-w[..., half:], w[..., :half]], axis=-1)


def _layer_weights(w_in, q_norm, w_uq, kv_norm, w_ukv, mla_out_norm, shift_mu, decay_w0, decay_w2,
                   iclr_a0, iclr_a2, gate_g2, k_k, k_a, r_k, lnx_g, lnx_b, w_out, norm_ffn,
                   w_router, b_router):
    d = w_in.shape[0]
    kr_cols = w_in[:, Q_LORA + KV_LORA:MLA_COLS]
    w_in_ext = jnp.concatenate([w_in[:, :MLA_COLS], _rot_half_cols(kr_cols), w_in[:, MLA_COLS:]],
                               axis=1)
    uq = w_uq.reshape(Q_LORA, MLA_HEADS, QK_NOPE + QK_ROPE)
    zq = jnp.zeros((Q_LORA, MLA_HEADS, QK_PAD - QK_NOPE - QK_ROPE), f32)
    w_qa = jnp.concatenate([uq, zq], axis=2).reshape(Q_LORA, MLA_HEADS * QK_PAD)
    w_qb = jnp.concatenate([jnp.zeros_like(uq[..., :QK_NOPE]), _rot_half_cols(uq[..., QK_NOPE:]), zq],
                           axis=2).reshape(Q_LORA, MLA_HEADS * QK_PAD)
    ukv = w_ukv.reshape(KV_LORA, MLA_HEADS, QK_NOPE + V_HEAD)
    zl = jnp.zeros((DECAY_LORA, RWKV_WIDTH), f32)
    hid = jnp.arange(RWKV_WIDTH) // RWKV_HEAD
    return dict(
        w_in_bf=w_in_ext.astype(bf16),
        q_norm=q_norm, kv_norm=kv_norm,
        w_qa=w_qa.astype(bf16), w_qb=w_qb.astype(bf16),
        w_kn=ukv[..., :QK_NOPE].reshape(KV_LORA, -1).astype(bf16),
        w_v=ukv[..., QK_NOPE:].reshape(KV_LORA, -1).astype(bf16),
        mla_out_norm=mla_out_norm.reshape(1, -1),
        shift_mu=shift_mu.reshape(1, -1), decay_w0=decay_w0.reshape(1, -1),
        decay_w2p=jnp.concatenate([decay_w2, zl], axis=0),
        iclr_a0=iclr_a0.reshape(1, -1),
        iclr_a2p=jnp.concatenate([zl, iclr_a2], axis=0),
        gate_g2=gate_g2, k_k=k_k.reshape(1, -1), k_a=k_a.reshape(1, -1), r_k=r_k.reshape(1, -1),
        lnx_g=lnx_g.reshape(1, -1), lnx_b=lnx_b.reshape(1, -1),
        seg=(hid[:, None] == hid[None, :]).astype(bf16),
        w_out_bf=w_out.astype(bf16), norm_ffn=norm_ffn.reshape(1, d),
        w_router=w_router, b_router=b_router.reshape(1, -1),
    )


def _rope_tables(pos):
    inv = ROPE_THETA ** (-jnp.arange(0, QK_ROPE, 2, dtype=f32) / QK_ROPE)
    ang = pos.astype(f32)[:, None] * inv[None, :]
    cos, sin = jnp.cos(ang), jnp.sin(ang)
    cos2 = jnp.concatenate([cos, cos], axis=1)
    sin2 = jnp.concatenate([sin, sin], axis=1)
    scale = (QK_NOPE + QK_ROPE) ** -0.5 * LOG2_E
    n = pos.shape[0]
    zpad = jnp.zeros((n, QK_PAD - QK_NOPE - QK_ROPE), f32)
    ct = jnp.concatenate([jnp.full((n, QK_NOPE), scale, f32), cos2 * scale, zpad], axis=1)
    st = jnp.concatenate([jnp.zeros((n, QK_NOPE), f32), sin2 * scale, zpad], axis=1)
    kt = jnp.concatenate([cos2, sin2], axis=1)
    return ct, st, kt


def _pair_states(state):
    b = state.shape[0]
    s = jnp.swapaxes(state, -1, -2).reshape(b, N_PAIRS, 2, RWKV_HEAD, RWKV_HEAD)
    z = jnp.zeros_like(s[:, :, 0])
    top = jnp.concatenate([s[:, :, 0], z], axis=-1)
    bot = jnp.concatenate([z, s[:, :, 1]], axis=-1)
    return jnp.concatenate([top, bot], axis=-2)


def _unpair_states(sp):
    b = sp.shape[0]
    h0 = sp[:, :, :RWKV_HEAD, :RWKV_HEAD]
    h1 = sp[:, :, RWKV_HEAD:, RWKV_HEAD:]
    heads = jnp.stack([h0, h1], axis=2).reshape(b, RWKV_HEADS, RWKV_HEAD, RWKV_HEAD)
    return jnp.swapaxes(heads, -1, -2)


def _mix_path(x, mod, pos, cache_lat, cache_rope, wkv_prev, shift_prev, wts, counts0):
    b, s, d = x.shape
    q, new_lat, new_rope, rope128, kv, seqs, new_shift = _front(
        x, mod, shift_prev, wts, _rope_tables(pos), with_kv=cache_lat is None)
    if cache_lat is None:
        o_mla = _flash_attention(q, *kv)
    else:
        past = cache_lat.shape[1]
        n_keys = past + s
        sk = -(-n_keys // 128) * 128
        lat_all = jnp.concatenate([cache_lat, new_lat, jnp.zeros((b, sk - n_keys, KV_LORA), f32)],
                                  axis=1)
        cache_rope128 = jnp.concatenate([cache_rope, jnp.zeros_like(cache_rope)], axis=-1)
        rope_all = jnp.concatenate([cache_rope128, rope128,
                                    jnp.zeros((b, sk - n_keys, 2 * QK_ROPE), f32)], axis=1)
        k, v = _kv_proj(lat_all, rope_all, wts["w_kn"], wts["w_v"])
        o_mla = _cached_attention(q, k, v, past, n_keys)

    scan_in, gate_rw, bonus = seqs[:6], seqs[6], seqs[7]
    scan_rows = _scan_rows(s)
    s_pad = -(-s // scan_rows) * scan_rows
    if s_pad != s:
        scan_in = tuple(jnp.pad(t, ((0, 0), (0, s_pad - s), (0, 0))) for t in scan_in)
    y_rw, s_fin = _rwkv_scan(scan_in, _pair_states(wkv_prev))
    new_wkv = _unpair_states(s_fin)

    x1, h2, gates, idx, rank, counts = _outproj(o_mla, y_rw[:, :s], bonus, gate_rw, x, mod, wts,
                                                counts0)
    return dict(x1=x1, h2=h2, gates=gates, idx=idx, rank=rank, counts=counts,
                new_lat=new_lat, new_rope=new_rope, new_wkv=new_wkv, new_shift=new_shift)


def kernel(x_prompt, x_sample, cache_kv_latent, cache_k_rope, state_wkv, state_shift, c_prompt, c_sample, w_ada, b_ada, norm_mix, w_in, q_norm, w_uq, kv_norm, w_ukv, mla_out_norm, shift_mu, decay_w0, decay_w2, iclr_a0, iclr_a2, gate_g2, k_k, k_a, r_k, lnx_g, lnx_b, w_out, norm_ffn, w_router, b_router, w_gate_up, b_gate_up, w_down, b_down, norm_final):
    depth = w_ada.shape[0]
    bp, seq_p, d = x_prompt.shape
    bs, seq_s, _ = x_sample.shape
    past = cache_kv_latent.shape[2]
    pos_p = jnp.arange(seq_p, dtype=i32)
    pos_s = past + jnp.arange(seq_s, dtype=i32)
    zero_wkv = jnp.zeros((bp, RWKV_HEADS, RWKV_HEAD, RWKV_HEAD), f32)
    zero_shift = jnp.zeros((bp, RWKV_COLS), f32)
    n_c = bp + bs
    c_rows = -(-n_c // 8) * 8
    c_all = jnp.concatenate([c_prompt, c_sample, jnp.zeros((c_rows - n_c, d), f32)], axis=0)

    hp, hs = x_prompt, x_sample
    outs_p = [[], [], [], []]
    outs_s = [[], [], [], []]
    tm = EXPERT_TILE
    n_pairs = (bp * seq_p + bs * seq_s) * TOP_K
    n_blocks = (n_pairs + N_EXPERTS * (tm - 1) + tm - 1) // tm
    for l in range(depth):
        wts = _layer_weights(w_in[l], q_norm[l], w_uq[l], kv_norm[l], w_ukv[l], mla_out_norm[l],
                             shift_mu[l], decay_w0[l], decay_w2[l], iclr_a0[l], iclr_a2[l],
                             gate_g2[l], k_k[l], k_a[l], r_k[l], lnx_g[l], lnx_b[l], w_out[l],
                             norm_ffn[l], w_router[l], b_router[l])
        wts["norm_mix"] = norm_mix[l]
        mod = _modulation(c_all, w_ada[l], b_ada[l]).reshape(c_rows, 6, d)
        mod_p, mod_s = mod[:bp], mod[bp:n_c]

        rp = _mix_path(hp, mod_p, pos_p, None, None, zero_wkv, zero_shift, wts,
                       jnp.zeros((1, N_EXPERTS), f32))
        rs = _mix_path(hs, mod_s, pos_s, cache_kv_latent[l], cache_k_rope[l], state_wkv[l],
                       state_shift[l], wts, rp["counts"])

        counts = rs["counts"][0].astype(i32)
        padded = (counts + tm - 1) // tm * tm
        pad_end = jnp.cumsum(padded)
        pad_start = pad_end - padded
        experts = jnp.arange(N_EXPERTS, dtype=i32)

        def slots(r):
            idx = r["idx"][..., :TOP_K]
            start = jnp.sum(jnp.where(idx[..., None] == experts, pad_start, 0), axis=-1)
            return start + r["rank"][..., :TOP_K]

        dest_p, dest_s = slots(rp), slots(rs)
        tile_row = jnp.arange(n_blocks, dtype=i32) * tm
        block_e = jnp.minimum(jnp.sum((pad_end[None, :] <= tile_row[:, None]).astype(i32), axis=1),
                              N_EXPERTS - 1)
        n_used = (pad_end[-1:] // tm).astype(i32)
        zrow = jnp.concatenate([jnp.where(padded > 0, pad_end - tm, -1), n_used]).astype(i32)

        xs = _dispatch(rp["h2"], dest_p, rs["h2"], dest_s, zrow, n_blocks * tm)
        yb = _experts(xs, block_e, n_used, w_gate_up[l], b_gate_up[l], w_down[l], b_down[l])

        last = l == depth - 1
        hp = _combine(rp["x1"], rp["gates"], mod_p, dest_p, yb, norm_final, last)
        hs = _combine(rs["x1"], rs["gates"], mod_s, dest_s, yb, norm_final, last)
        for acc, r in ((outs_p, rp), (outs_s, rs)):
            acc[0].append(r["new_lat"])
            acc[1].append(r["new_rope"])
            acc[2].append(r["new_wkv"])
            acc[3].append(r["new_shift"])

    return (hp, hs, jnp.stack(outs_p[0]), jnp.stack(outs_p[1]), jnp.stack(outs_p[2]),
            jnp.stack(outs_p[3]), jnp.stack(outs_s[0]), jnp.stack(outs_s[1]),
            jnp.stack(outs_s[2]), jnp.stack(outs_s[3]))
```

```python
import functools

import jax
import jax.numpy as jnp
from jax import lax
from jax.experimental import pallas as pl
from jax.experimental.pallas import tpu as pltpu

f32 = jnp.float32
bf16 = jnp.bfloat16
i32 = jnp.int32

D_MODEL = 1024
CHUNK = 64
MLA_HEADS = 4
QK_NOPE = 128
QK_ROPE = 64
V_HEAD = 128
Q_LORA = 256
KV_LORA = 128
ROPE_THETA = 10000.0
MLA_WIDTH = MLA_HEADS * V_HEAD
RWKV_HEAD = 64
RWKV_HEADS = 8
RWKV_WIDTH = RWKV_HEADS * RWKV_HEAD
DECAY_LORA = 64
ICLR_LORA = 64
GATE_LORA = 128
LN_X_EPS = 64e-5
MLA_COLS = Q_LORA + KV_LORA + QK_ROPE
RWKV_COLS = 3 * RWKV_WIDTH + DECAY_LORA + ICLR_LORA + GATE_LORA
N_EXPERTS = 32
TOP_K = 4
D_FF = 1024
SWIGLU_ALPHA = 1.702
SWIGLU_LIMIT = 7.0
NORM_EPS = 1e-6
LOG2_E = 1.4426950408889634

MLA_PAD = 512
QK_PAD = 256
N_PAIRS = RWKV_HEADS // 2
PAIR_W = 2 * RWKV_HEAD
ROUTE_W = 128
ROW_TILE = (8, 128)

TOKEN_TILE = 256
KV_WHOLE_MAX = 2048
ATTN_TILE = 2048
ATTN_SUB = 512
SCAN_CHUNK = 64
SCAN_STEP_CHUNKS = 4
SCAN_PASSES_GRAM = 1
SCAN_PASSES_INV = 1
SCAN_PASSES_STATE = 3
SCAN_PASSES_OUT = 1
EXPERT_TILE = 512
ROUTE_TILE = 512
COMBINE_PARTS = 4
VMEM_LIMIT = 56 * 1024 * 1024


def _token_tile(s):
    t = TOKEN_TILE
    while s % t:
        t //= 2
    assert t >= 8, s
    return t


def _cparams(sem, vmem=None):
    return pltpu.CompilerParams(dimension_semantics=sem, vmem_limit_bytes=vmem or VMEM_LIMIT)


def _dot(a, b):
    return jnp.dot(a.astype(bf16), b.astype(bf16), preferred_element_type=f32)


def _dg(a, b, ca, cb):
    return lax.dot_general(a, b, (((ca,), (cb,)), ((), ())), preferred_element_type=f32)


def _split2(x):
    hi = x.astype(bf16)
    lo = (x - hi.astype(f32)).astype(bf16)
    return hi, lo


def _split3(x):
    hi = x.astype(bf16)
    r1 = x - hi.astype(f32)
    mid = r1.astype(bf16)
    lo = (r1 - mid.astype(f32)).astype(bf16)
    return hi, mid, lo


def _dot3(a, b, ca=1, cb=0):
    ah, al = _split2(a)
    bh, bl = _split2(b)
    return _dg(ah, bh, ca, cb) + (_dg(ah, bl, ca, cb) + _dg(al, bh, ca, cb))


def _dot_exact_lhs(a_bf, b):
    bh, bm, bl = _split3(b)
    return _dg(a_bf, bh, 1, 0) + (_dg(a_bf, bm, 1, 0) + _dg(a_bf, bl, 1, 0))


def _seg_sum(x, seg_bf):
    xh, xl = _split2(x)
    return _dg(xh, seg_bf, 1, 0) + _dg(xl, seg_bf, 1, 0)


def _rms(x, g):
    return x * lax.rsqrt(jnp.mean(x * x, axis=-1, keepdims=True) + NORM_EPS) * g


def _sigmoid(x):
    return 1.0 / (1.0 + jnp.exp(-x))


def _mod_kernel(c_ref, w_ref, b_ref, o_ref):
    c = c_ref[...]
    o_ref[...] = _dot3(c * _sigmoid(c), w_ref[...]) + b_ref[...]


def _modulation(c_all, w_ada, b_ada):
    rows, d = c_all.shape
    n = w_ada.shape[1]
    tn = 1536
    return pl.pallas_call(
        _mod_kernel,
        name="adaln_mod",
        out_shape=jax.ShapeDtypeStruct((rows, n), f32),
        grid=(n // tn,),
        in_specs=[pl.BlockSpec((rows, d), lambda j: (0, 0)),
                  pl.BlockSpec((d, tn), lambda j: (0, j)),
                  pl.BlockSpec((1, tn), lambda j: (0, j))],
        out_specs=pl.BlockSpec((rows, tn), lambda j: (0, j)),
        compiler_params=_cparams(("arbitrary",)),
    )(c_all, w_ada, b_ada.reshape(1, n))


def _project_kv(lat, rope128, wk_ref, wv_ref, k_ref, v_ref):
    lat = lat.astype(bf16)
    kn = jnp.dot(lat, wk_ref[...], preferred_element_type=f32)
    vv = jnp.dot(lat, wv_ref[...], preferred_element_type=f32)
    rope = rope128.astype(bf16)
    for h in range(MLA_HEADS):
        k_ref[0, h, :, :QK_NOPE] = kn[:, h * QK_NOPE:(h + 1) * QK_NOPE].astype(bf16)
        k_ref[0, h, :, QK_NOPE:] = rope
        v_ref[0, h] = vv[:, h * V_HEAD:(h + 1) * V_HEAD].astype(bf16)


def _front_kernel(*refs, with_kv):
    (x_ref, mod_ref, nw_ref, w_ref, qn_ref, wa_ref, wb_ref, kvn_ref, ct_ref, st_ref, kt_ref,
     wk_ref, wv_ref, sp_ref, mu_ref, w0_ref, w2_ref, a0_ref, a2_ref, g2_ref, kk_ref, ka_ref,
     rk_ref, seg_ref) = refs[:24]
    outs = refs[24:-1]
    carry = refs[-1]
    q_ref, lat_ref, rope_ref, rope128_ref = outs[:4]
    outs = outs[4:]
    if with_kv:
        k_ref, v_ref = outs[:2]
        outs = outs[2:]
    r_ref, lw_ref, kh_ref, vv_ref, al_ref, be_ref, g_ref, bo_ref, ns_ref = outs

    h = _rms(x_ref[0], nw_ref[...])
    h = h * (1.0 + mod_ref[0, 1:2, :]) + mod_ref[0, 0:1, :]
    proj = jnp.dot(h.astype(bf16), w_ref[...], preferred_element_type=f32)
    pm = proj[:, :MLA_PAD]
    p = proj[:, MLA_PAD:]

    qn = _rms(pm[:, :Q_LORA], qn_ref[...]).astype(bf16)
    qa = jnp.dot(qn, wa_ref[...], preferred_element_type=f32)
    qb = jnp.dot(qn, wb_ref[...], preferred_element_type=f32)
    ct = ct_ref[...]
    st = st_ref[...]
    for hd in range(MLA_HEADS):
        sl = slice(hd * QK_PAD, (hd + 1) * QK_PAD)
        q_ref[0, hd] = (qa[:, sl] * ct + qb[:, sl] * st).astype(bf16)
    lat = _rms(pm[:, Q_LORA:Q_LORA + KV_LORA], kvn_ref[...])
    lat_ref[0] = lat
    t = pm[:, Q_LORA + KV_LORA:] * kt_ref[...]
    kr = t + pltpu.roll(t, QK_ROPE, axis=1)
    lane = lax.broadcasted_iota(i32, kr.shape, 1)
    rope128 = jnp.where(lane < QK_ROPE, kr, 0.0)
    rope128_ref[0] = rope128
    rope_ref[0] = kr[:, :QK_ROPE]
    if with_kv:
        _project_kv(lat, rope128, wk_ref, wv_ref, k_ref, v_ref)

    @pl.when(pl.program_id(1) == 0)
    def _():
        carry[...] = sp_ref[0]

    ts = p.shape[0]
    row = lax.broadcasted_iota(i32, p.shape, 0)
    prev = jnp.where(row == 0, carry[...], pltpu.roll(p, 1, axis=0))
    last = p[ts - 1:ts, :]
    carry[...] = last
    ns_ref[0] = last
    xs = p + (prev - p) * mu_ref[...]
    w = RWKV_WIDTH
    r, k, v = xs[:, :w], xs[:, w:2 * w], xs[:, 2 * w:3 * w]
    wa = xs[:, 3 * w:3 * w + DECAY_LORA + ICLR_LORA]
    gd = xs[:, 3 * w + DECAY_LORA + ICLR_LORA:]
    z = -(w0_ref[...] + _dot3(jnp.tanh(wa), w2_ref[...]))
    softplus = jnp.maximum(z, 0.0) + jnp.log(1.0 + jnp.exp(-jnp.abs(z)))
    lw = -jnp.exp(-softplus - 0.5)
    a = _sigmoid(a0_ref[...] + _dot3(wa, a2_ref[...]))
    g = _dot3(_sigmoid(gd), g2_ref[...])
    seg = seg_ref[...]
    kk = k * kk_ref[...]
    kk = kk * lax.rsqrt(jnp.maximum(_seg_sum(kk * kk, seg), 1e-24))
    kh = k * (1.0 + (a - 1.0) * ka_ref[...])
    r_ref[0] = r
    lw_ref[0] = lw
    kh_ref[0] = kh
    vv_ref[0] = v
    al_ref[0] = -kk
    be_ref[0] = kk * a
    g_ref[0] = g
    bo_ref[0] = _seg_sum(r * kh * rk_ref[...], seg) * v


def _front(x, mod, shift_prev, wts, tables, with_kv):
    b, s, d = x.shape
    ts = _token_tile(s)
    n_in = wts["w_in_bf"].shape[1]
    hq = MLA_HEADS * QK_PAD
    n = RWKV_COLS
    w = RWKV_WIDTH
    ct, st, kt = tables
    full = lambda r, c: pl.BlockSpec((r, c), lambda i, j: (0, 0))
    tok = lambda width: pl.BlockSpec((1, ts, width), lambda i, j: (i, j, 0))
    heads = lambda width: pl.BlockSpec((1, MLA_HEADS, ts, width), lambda i, j: (i, 0, j, 0))
    in_specs = [tok(d), pl.BlockSpec((1, 6, d), lambda i, j: (i, 0, 0)), full(1, d), full(d, n_in),
                full(1, Q_LORA), full(Q_LORA, hq), full(Q_LORA, hq), full(1, KV_LORA),
                pl.BlockSpec((ts, QK_PAD), lambda i, j: (j, 0)),
                pl.BlockSpec((ts, QK_PAD), lambda i, j: (j, 0)),
                pl.BlockSpec((ts, 2 * QK_ROPE), lambda i, j: (j, 0)),
                full(KV_LORA, MLA_HEADS * QK_NOPE), full(KV_LORA, MLA_HEADS * V_HEAD),
                pl.BlockSpec((1, 1, n), lambda i, j: (i, 0, 0)),
                full(1, n), full(1, w), full(PAIR_W, w), full(1, w), full(PAIR_W, w),
                full(GATE_LORA, w), full(1, w), full(1, w), full(1, w), full(w, w)]
    out_shape = [jax.ShapeDtypeStruct((b, MLA_HEADS, s, QK_PAD), bf16),
                 jax.ShapeDtypeStruct((b, s, KV_LORA), f32),
                 jax.ShapeDtypeStruct((b, s, QK_ROPE), f32),
                 jax.ShapeDtypeStruct((b, s, 2 * QK_ROPE), f32)]
    out_specs = [heads(QK_PAD), tok(KV_LORA), tok(QK_ROPE), tok(2 * QK_ROPE)]
    if with_kv:
        out_shape += [jax.ShapeDtypeStruct((b, MLA_HEADS, s, QK_PAD), bf16),
                      jax.ShapeDtypeStruct((b, MLA_HEADS, s, V_HEAD), bf16)]
        out_specs += [heads(QK_PAD), heads(V_HEAD)]
    out_shape += [jax.ShapeDtypeStruct((b, s, w), f32)] * 8 + [jax.ShapeDtypeStruct((b, 1, n), f32)]
    out_specs += [tok(w)] * 8 + [pl.BlockSpec((1, 1, n), lambda i, j: (i, 0, 0))]
    outs = pl.pallas_call(
        functools.partial(_front_kernel, with_kv=with_kv),
        name="front",
        out_shape=tuple(out_shape),
        grid=(b, s // ts),
        in_specs=in_specs,
        out_specs=tuple(out_specs),
        scratch_shapes=[pltpu.VMEM((1, n), f32)],
        compiler_params=_cparams(("arbitrary", "arbitrary")),
    )(x, mod, wts["norm_mix"].reshape(1, d), wts["w_in_bf"], wts["q_norm"].reshape(1, -1),
      wts["w_qa"], wts["w_qb"], wts["kv_norm"].reshape(1, -1), ct, st, kt, wts["w_kn"], wts["w_v"],
      shift_prev.reshape(b, 1, n), wts["shift_mu"], wts["decay_w0"], wts["decay_w2p"],
      wts["iclr_a0"], wts["iclr_a2p"], wts["gate_g2"], wts["k_k"], wts["k_a"], wts["r_k"],
      wts["seg"])
    outs = list(outs)
    q, new_lat, new_rope, rope128 = outs[:4]
    kv = tuple(outs[4:6]) if with_kv else None
    rest = outs[6:] if with_kv else outs[4:]
    return q, new_lat, new_rope, rope128, kv, tuple(rest[:8]), rest[8].reshape(b, n)


def _kv_proj_kernel(lat_ref, rope_ref, wk_ref, wv_ref, k_ref, v_ref):
    _project_kv(lat_ref[0], rope_ref[0], wk_ref, wv_ref, k_ref, v_ref)


def _kv_proj(lat, rope128, w_kn, w_v):
    b, s, _ = lat.shape
    ts = s if s <= KV_WHOLE_MAX else _token_tile(s)
    return pl.pallas_call(
        _kv_proj_kernel,
        name="kv_proj",
        out_shape=(jax.ShapeDtypeStruct((b, MLA_HEADS, s, QK_PAD), bf16),
                   jax.ShapeDtypeStruct((b, MLA_HEADS, s, V_HEAD), bf16)),
        grid=(b, s // ts),
        in_specs=[pl.BlockSpec((1, ts, KV_LORA), lambda i, j: (i, j, 0)),
                  pl.BlockSpec((1, ts, 2 * QK_ROPE), lambda i, j: (i, j, 0)),
                  pl.BlockSpec((KV_LORA, MLA_HEADS * QK_NOPE), lambda i, j: (0, 0)),
                  pl.BlockSpec((KV_LORA, MLA_HEADS * V_HEAD), lambda i, j: (0, 0))],
        out_specs=(pl.BlockSpec((1, MLA_HEADS, ts, QK_PAD), lambda i, j: (i, 0, j, 0)),
                   pl.BlockSpec((1, MLA_HEADS, ts, V_HEAD), lambda i, j: (i, 0, j, 0))),
        compiler_params=_cparams(("arbitrary", "arbitrary")),
    )(lat, rope128, w_kn, w_v)


def _flash_kernel(qi_ref, kj_ref, q_ref, k_ref, v_ref, o_ref, m_scr, l_scr, acc_scr, *, tile, sub):
    step = pl.program_id(1)
    qi = qi_ref[step]
    kj = kj_ref[step]
    lanes = m_scr.shape[1]

    @pl.when(kj == 0)
    def _():
        m_scr[...] = jnp.full(m_scr.shape, -jnp.inf, f32)
        l_scr[...] = jnp.zeros(l_scr.shape, f32)
        acc_scr[...] = jnp.zeros(acc_scr.shape, f32)

    def update(c, r0, masked):
        rows = pl.ds(r0, tile - r0)
        keys = pl.ds(c * sub, sub)
        s = _dg(q_ref[0, 0, rows, :], k_ref[0, 0, keys, :], 1, 1)
        if masked:
            qpos = r0 + lax.broadcasted_iota(i32, s.shape, 0)
            kpos = c * sub + lax.broadcasted_iota(i32, s.shape, 1)
            s = jnp.where(kpos < (qpos // CHUNK + 1) * CHUNK, s, -jnp.inf)
        m_prev = m_scr[rows, :]
        m_new = jnp.maximum(m_prev, jnp.max(s, axis=1, keepdims=True))
        alpha = jnp.exp2(m_prev - m_new)
        p = jnp.exp2(s - jnp.concatenate([m_new] * (sub // lanes), axis=1))
        psum = p[:, :lanes]
        for t in range(1, sub // lanes):
            psum = psum + p[:, t * lanes:(t + 1) * lanes]
        l_scr[rows, :] = alpha * l_scr[rows, :] + psum
        acc_scr[rows, :] = alpha * acc_scr[rows, :] + jnp.dot(
            p.astype(bf16), v_ref[0, 0, keys, :], preferred_element_type=f32)
        m_scr[rows, :] = m_new

    n_sub = tile // sub

    @pl.when(kj < qi)
    def _():
        for c in range(n_sub):
            update(c, 0, False)

    @pl.when(kj == qi)
    def _():
        for c in range(n_sub):
            update(c, c * sub, True)
        o_ref[0] = acc_scr[...] / jnp.sum(l_scr[...], axis=1, keepdims=True)


def _flash_attention(q, k, v):
    b, nh, s, _ = q.shape
    assert b == 1
    tile = min(ATTN_TILE, s)
    sub = min(ATTN_SUB, tile)
    assert sub % CHUNK == 0 and tile % sub == 0
    nt = s // tile
    qi = [i for i in range(nt) for _ in range(i + 1)]
    kj = [j for i in range(nt) for j in range(i + 1)]
    grid_spec = pltpu.PrefetchScalarGridSpec(
        num_scalar_prefetch=2,
        grid=(nh, len(qi)),
        in_specs=[pl.BlockSpec((1, 1, tile, QK_PAD), lambda h, t, qi, kj: (0, h, qi[t], 0)),
                  pl.BlockSpec((1, 1, tile, QK_PAD), lambda h, t, qi, kj: (0, h, kj[t], 0)),
                  pl.BlockSpec((1, 1, tile, V_HEAD), lambda h, t, qi, kj: (0, h, kj[t], 0))],
        out_specs=pl.BlockSpec((1, tile, V_HEAD), lambda h, t, qi, kj: (0, qi[t], h)),
        scratch_shapes=[pltpu.VMEM((tile, V_HEAD), f32), pltpu.VMEM((tile, V_HEAD), f32),
                        pltpu.VMEM((tile, V_HEAD), f32)])
    return pl.pallas_call(
        functools.partial(_flash_kernel, tile=tile, sub=sub),
        name="flash_attn",
        out_shape=jax.ShapeDtypeStruct((1, s, MLA_WIDTH), f32),
        grid_spec=grid_spec,
        compiler_params=_cparams(("arbitrary", "arbitrary")),
    )(jnp.asarray(qi, i32), jnp.asarray(kj, i32), q, k, v)


def _cached_attn_kernel(q_ref, k_ref, v_ref, o_ref, *, past, n_keys):
    for h in range(MLA_HEADS):
        s = _dg(q_ref[0, h], k_ref[0, h], 1, 1)
        qpos = past + lax.broadcasted_iota(i32, s.shape, 0)
        kpos = lax.broadcasted_iota(i32, s.shape, 1)
        limit = (qpos // CHUNK + 1) * CHUNK
        s = jnp.where((kpos < limit) & (kpos < n_keys), s, -jnp.inf)
        m = jnp.max(s, axis=1, keepdims=True)
        p = jnp.exp2(s - m)
        o = jnp.dot(p.astype(bf16), v_ref[0, h], preferred_element_type=f32)
        o_ref[0, :, h * V_HEAD:(h + 1) * V_HEAD] = o / jnp.sum(p, axis=1, keepdims=True)


def _cached_attention(q, k, v, past, n_keys):
    b, nh, sq, _ = q.shape
    sk = k.shape[2]
    return pl.pallas_call(
        functools.partial(_cached_attn_kernel, past=past, n_keys=n_keys),
        name="cached_attn",
        out_shape=jax.ShapeDtypeStruct((b, sq, MLA_WIDTH), f32),
        grid=(b,),
        in_specs=[pl.BlockSpec((1, nh, sq, QK_PAD), lambda i: (i, 0, 0, 0)),
                  pl.BlockSpec((1, nh, sk, QK_PAD), lambda i: (i, 0, 0, 0)),
                  pl.BlockSpec((1, nh, sk, V_HEAD), lambda i: (i, 0, 0, 0))],
        out_specs=pl.BlockSpec((1, sq, MLA_WIDTH), lambda i: (i, 0, 0)),
        compiler_params=_cparams(("arbitrary",)),
    )(q, k, v)


def _mm(a, b, ca, cb, passes):
    if passes == 1:
        return _dg(a.astype(bf16), b.astype(bf16), ca, cb)
    return _dot3(a, b, ca, cb)


def _scan_kernel(r_ref, lw_ref, k_ref, v_ref, al_ref, be_ref, s0_ref, y_ref, sf_ref, st, *, c):
    @pl.when(pl.program_id(1) == 0)
    def _():
        st[...] = s0_ref[0]

    rows = r_ref.shape[1]
    nch = rows // c
    c2 = 2 * c
    rowb = lax.broadcasted_iota(i32, (rows, rows), 0)
    colb = lax.broadcasted_iota(i32, (rows, rows), 1)
    same_chunk = (rowb // c) == (colb // c)

    lw = lw_ref[0]
    cum = _dot_exact_lhs((same_chunk & (colb <= rowb)).astype(bf16), lw)
    tot = _dot_exact_lhs(same_chunk.astype(bf16), lw)
    g_inv = jnp.exp(-cum)
    g_tail = jnp.exp(tot - cum)
    r_t = r_ref[0] * jnp.exp(cum)
    a_t = al_ref[0] * jnp.exp(cum - lw)
    b_t = be_ref[0] * g_inv
    k_t = k_ref[0] * g_inv
    b_c = be_ref[0] * g_tail
    k_c = k_ref[0] * g_tail
    v_all = v_ref[0]

    ri = lax.broadcasted_iota(i32, (c2, c2), 0)
    ci = lax.broadcasted_iota(i32, (c2, c2), 1)
    same = (ri // c) == (ci // c)
    strict = same & (ci < ri)
    incl = same & (ci <= ri)
    eye = (ri == ci).astype(f32)
    head0 = lax.broadcasted_iota(i32, (c, PAIR_W), 1) < RWKV_HEAD
    ones = jnp.ones((c, PAIR_W), bf16)

    def stack(x):
        return jnp.concatenate([jnp.where(head0, x, 0.0), jnp.where(head0, 0.0, x)], axis=0)

    chains = [(j, p) for j in range(nch) for p in range(N_PAIRS)]

    def part(t, j, p):
        return t[j * c:(j + 1) * c, p * PAIR_W:(p + 1) * PAIR_W]

    ops = {}
    for j, p in chains:
        ops[j, p] = {n: stack(part(t, j, p)) for n, t in
                     (("a", a_t), ("r", r_t), ("b", b_t), ("k", k_t), ("v", v_all),
                      ("bc", b_c), ("kc", k_c))}

    gram = {jp: _mm(jnp.concatenate([o["a"], o["r"]], axis=0),
                    jnp.concatenate([o["b"], o["k"]], axis=0), 1, 1, SCAN_PASSES_GRAM)
            for jp, o in ops.items()}
    a_ab = {jp: jnp.where(strict, g[:c2, :c2], 0.0) for jp, g in gram.items()}
    a_ak = {jp: jnp.where(strict, g[:c2, c2:], 0.0) for jp, g in gram.items()}
    a_rb = {jp: jnp.where(incl, g[c2:, :c2], 0.0) for jp, g in gram.items()}
    a_rk = {jp: jnp.where(incl, g[c2:, c2:], 0.0) for jp, g in gram.items()}
    t_inv = {jp: eye + low for jp, low in a_ab.items()}
    pw = dict(a_ab)
    for _ in range(max(c.bit_length() - 2, 0)):
        pw = {jp: _mm(m, m, 1, 0, SCAN_PASSES_INV) for jp, m in pw.items()}
        t_inv = {jp: t_inv[jp] + _mm(t_inv[jp], pw[jp], 1, 0, SCAN_PASSES_INV) for jp in pw}
    decay = {}
    for j, p in chains:
        lh, lm, ll = _split3(part(lw, j, p))
        decay[j, p] = jnp.exp(_dg(lh, ones, 0, 0) + (_dg(lm, ones, 0, 0) + _dg(ll, ones, 0, 0)))

    states = [st[p] for p in range(N_PAIRS)]
    y_rows = []
    prs = range(N_PAIRS)
    for j in range(nch):
        x = [_mm(jnp.concatenate([ops[j, p]["a"], a_ak[j, p]], axis=1),
                 jnp.concatenate([states[p], ops[j, p]["v"]], axis=0), 1, 0, SCAN_PASSES_STATE)
             for p in prs]
        u = [_mm(t_inv[j, p], x[p], 1, 0, SCAN_PASSES_STATE) for p in prs]
        upd = [_mm(jnp.concatenate([ops[j, p]["bc"], ops[j, p]["kc"]], axis=0),
                   jnp.concatenate([u[p], ops[j, p]["v"]], axis=0), 0, 0, SCAN_PASSES_STATE)
               for p in prs]
        y = [_mm(jnp.concatenate([ops[j, p]["r"], a_rb[j, p], a_rk[j, p]], axis=1),
                 jnp.concatenate([states[p], u[p], ops[j, p]["v"]], axis=0), 1, 0, SCAN_PASSES_OUT)
             for p in prs]
        states = [states[p] * decay[j, p] + upd[p] for p in prs]
        y_rows.append(jnp.concatenate([yp[:c] + yp[c:] for yp in y], axis=1))

    y_ref[0] = jnp.concatenate(y_rows, axis=0)
    for p in prs:
        st[p] = states[p]
        sf_ref[0, p] = states[p]


def _scan_rows(s):
    return SCAN_CHUNK * min(SCAN_STEP_CHUNKS, -(-s // SCAN_CHUNK))


def _rwkv_scan(seqs, s0_pairs):
    r, lw, kh, v, al, be = seqs
    b, s, w = r.shape
    rows = _scan_rows(s)
    tok = pl.BlockSpec((1, rows, w), lambda i, j: (i, j, 0))
    stt = pl.BlockSpec((1, N_PAIRS, PAIR_W, PAIR_W), lambda i, j: (i, 0, 0, 0))
    return pl.pallas_call(
        functools.partial(_scan_kernel, c=SCAN_CHUNK),
        name="rwkv_scan",
        out_shape=(jax.ShapeDtypeStruct((b, s, w), f32),
                   jax.ShapeDtypeStruct((b, N_PAIRS, PAIR_W, PAIR_W), f32)),
        grid=(b, s // rows),
        in_specs=[tok] * 6 + [stt],
        out_specs=(tok, stt),
        scratch_shapes=[pltpu.VMEM((N_PAIRS, PAIR_W, PAIR_W), f32)],
        compiler_params=_cparams(("arbitrary", "arbitrary")),
    )(r, lw, kh, v, al, be, s0_pairs)


def _outproj_kernel(om_ref, y_ref, bo_ref, g_ref, x_ref, mod_ref, mn_ref, lng_ref, lnb_ref,
                    seg_ref, wo_ref, nf_ref, wr_ref, br_ref, tri_ref, cnt0_ref,
                    x1_ref, h2_ref, gate_ref, idx_ref, rank_ref, cnt_ref, run):
    @pl.when((pl.program_id(0) == 0) & (pl.program_id(1) == 0))
    def _():
        run[...] = cnt0_ref[...]

    y = y_ref[0]
    seg = seg_ref[...]
    inv_n = 1.0 / RWKV_HEAD
    yc = y - _seg_sum(y, seg) * inv_n
    var = _seg_sum(yc * yc, seg) * inv_n
    o_rw = (yc * lax.rsqrt(var + LN_X_EPS) * lng_ref[...] + lnb_ref[...] + bo_ref[0]) * g_ref[0]

    om = _rms(om_ref[0], mn_ref[...])
    mix = (jnp.dot(om.astype(bf16), wo_ref[:MLA_WIDTH, :], preferred_element_type=f32)
           + jnp.dot(o_rw.astype(bf16), wo_ref[MLA_WIDTH:, :], preferred_element_type=f32))
    x1 = x_ref[0] + mod_ref[0, 2:3, :] * mix
    x1_ref[0] = x1
    h2 = _rms(x1, nf_ref[...]) * (1.0 + mod_ref[0, 4:5, :]) + mod_ref[0, 3:4, :]
    h2_ref[0] = h2

    logits = _dot3(h2, wr_ref[...]) + br_ref[...]
    ts = logits.shape[0]
    lane_f = lax.broadcasted_iota(i32, logits.shape, 1).astype(f32)
    lane_w = lax.broadcasted_iota(i32, (ts, ROUTE_W), 1)
    tri = tri_ref[...]
    base = run[...]
    gates = jnp.zeros((ts, ROUTE_W), f32)
    idxs = jnp.zeros((ts, ROUTE_W), i32)
    ranks = jnp.zeros((ts, ROUTE_W), i32)
    vals = []
    work = logits
    for k in range(TOP_K):
        m = jnp.max(work, axis=1, keepdims=True)
        sel_f = jnp.min(jnp.where(work == m, lane_f, float(N_EXPERTS)), axis=1, keepdims=True)
        hit = lane_f == sel_f
        sel = sel_f.astype(i32)
        work = jnp.where(hit, -jnp.inf, work)
        vals.append(m)
        onehot = hit.astype(bf16)
        before = jnp.dot(tri, onehot, preferred_element_type=f32)
        rank = jnp.sum(jnp.where(hit, before + base, 0.0), axis=1, keepdims=True)
        base = base + jnp.sum(hit.astype(f32), axis=0, keepdims=True)
        idxs = jnp.where(lane_w == k, sel, idxs)
        ranks = jnp.where(lane_w == k, rank.astype(i32), ranks)
    run[...] = base
    es = [jnp.exp(vk - vals[0]) for vk in vals]
    inv = 1.0 / (es[0] + es[1] + es[2] + es[3])
    for k in range(TOP_K):
        gates = jnp.where(lane_w == k, es[k] * inv, gates)
    gate_ref[0] = gates
    idx_ref[0] = idxs
    rank_ref[0] = ranks
    cnt_ref[...] = base


def _outproj(o_mla, y_rw, bonus, gate_rw, x, mod, wts, counts0):
    b, s, d = x.shape
    w = RWKV_WIDTH
    ts = _token_tile(s)
    tok = lambda width: pl.BlockSpec((1, ts, width), lambda i, j: (i, j, 0))
    full = lambda r, c: pl.BlockSpec((r, c), lambda i, j: (0, 0))
    tri = (lax.broadcasted_iota(i32, (ts, ts), 1) < lax.broadcasted_iota(i32, (ts, ts), 0)
           ).astype(bf16)
    return pl.pallas_call(
        _outproj_kernel,
        name="outproj_route",
        out_shape=(jax.ShapeDtypeStruct((b, s, d), f32),
                   jax.ShapeDtypeStruct((b, s, d), f32),
                   jax.ShapeDtypeStruct((b, s, ROUTE_W), f32),
                   jax.ShapeDtypeStruct((b, s, ROUTE_W), i32),
                   jax.ShapeDtypeStruct((b, s, ROUTE_W), i32),
                   jax.ShapeDtypeStruct((1, N_EXPERTS), f32)),
        grid=(b, s // ts),
        in_specs=[tok(MLA_WIDTH), tok(w), tok(w), tok(w), tok(d),
                  pl.BlockSpec((1, 6, d), lambda i, j: (i, 0, 0)),
                  full(1, MLA_WIDTH), full(1, w), full(1, w), full(w, w),
                  full(d, d), full(1, d), full(d, N_EXPERTS),
                  full(1, N_EXPERTS), full(ts, ts), full(1, N_EXPERTS)],
        out_specs=(tok(d), tok(d), tok(ROUTE_W), tok(ROUTE_W), tok(ROUTE_W),
                   full(1, N_EXPERTS)),
        scratch_shapes=[pltpu.VMEM((1, N_EXPERTS), f32)],
        compiler_params=_cparams(("arbitrary", "arbitrary")),
    )(o_mla, y_rw, bonus, gate_rw, x, mod, wts["mla_out_norm"], wts["lnx_g"], wts["lnx_b"],
      wts["seg"], wts["w_out_bf"], wts["norm_ffn"], wts["w_router"], wts["b_router"], tri, counts0)


def _to_row_tiles(tiles_ref, x, first=0):
    sub, lanes = ROW_TILE
    for c in range(sub):
        tiles_ref[pl.ds(first * sub + c, x.shape[0], stride=sub), :] = x[:, c * lanes:(c + 1) * lanes]


def _from_row_tiles(tiles_ref, n, first=0):
    sub = ROW_TILE[0]
    return jnp.concatenate([tiles_ref[pl.ds(first * sub + c, n, stride=sub), :]
                            for c in range(sub)], axis=1)


def _row_tile(ref, t):
    sub = ROW_TILE[0]
    return ref.at[pl.ds(pl.multiple_of(t * sub, sub), sub), :]


def _scatter_rows(dest_ref, h_ref, hb, xs_ref, sem):
    tt = h_ref.shape[1]
    _to_row_tiles(hb, h_ref[0])

    def row_copy(t, d):
        return pltpu.make_async_copy(_row_tile(hb, t), _row_tile(xs_ref, d), sem)

    def issue(t, carry):
        for k in range(TOP_K):
            row_copy(t, dest_ref[0, 0, TOP_K * t + k]).start(priority=k % 2)
        return carry

    def drain(t, carry):
        for k in range(TOP_K):
            row_copy(0, 0).wait()
        return carry

    lax.fori_loop(0, tt, issue, 0)
    lax.fori_loop(0, tt, drain, 0)


def _dispatch_kernel(zrow_ref, dest_ref, h_ref, dest2_ref, h2_ref, xs_ref, zeros, hb, hb2, sem,
                     zsem):
    sub = ROW_TILE[0]
    tm = zeros.shape[0] // sub

    @pl.when(pl.program_id(0) == 0)
    def _():
        zeros[...] = jnp.zeros(zeros.shape, f32)

        def tile_copy(row):
            first = pl.multiple_of(row * sub, tm * sub)
            return pltpu.make_async_copy(zeros, xs_ref.at[pl.ds(first, tm * sub), :], zsem)

        def issue(e, carry):
            @pl.when(zrow_ref[e] >= 0)
            def _():
                tile_copy(zrow_ref[e]).start()
            return carry

        def drain(e, carry):
            @pl.when(zrow_ref[e] >= 0)
            def _():
                tile_copy(0).wait()
            return carry

        lax.fori_loop(0, N_EXPERTS, issue, 0)
        lax.fori_loop(0, N_EXPERTS, drain, 0)

        n_used = zrow_ref[N_EXPERTS]
        n_tiles = xs_ref.shape[0] // (tm * sub)

        def issue_tail(i, carry):
            tile_copy(i * tm).start()
            return carry

        def drain_tail(i, carry):
            tile_copy(0).wait()
            return carry

        lax.fori_loop(n_used, n_tiles, issue_tail, 0)
        lax.fori_loop(n_used, n_tiles, drain_tail, 0)

    _scatter_rows(dest_ref, h_ref, hb, xs_ref, sem)

    @pl.when(pl.program_id(0) == pl.num_programs(0) - 1)
    def _():
        _scatter_rows(dest2_ref, h2_ref, hb2, xs_ref, sem)


def _route_tile(s):
    t = ROUTE_TILE
    while s % t:
        t //= 2
    assert t >= 8, s
    return t


def _dispatch(h_big, dest_big, h_small, dest_small, zrow, n_rows):
    d = h_big.shape[-1]
    sub, lanes = ROW_TILE
    assert d == sub * lanes
    h_big = h_big.reshape(1, -1, d)
    h_small = h_small.reshape(1, -1, d)
    s, s2 = h_big.shape[1], h_small.shape[1]
    tt = _route_tile(s)
    nt = s // tt
    grid_spec = pltpu.PrefetchScalarGridSpec(
        num_scalar_prefetch=1,
        grid=(nt,),
        in_specs=[pl.BlockSpec((1, 1, tt * TOP_K), lambda j, z: (j, 0, 0),
                               memory_space=pltpu.SMEM),
                  pl.BlockSpec((1, tt, d), lambda j, z: (0, j, 0)),
                  pl.BlockSpec((1, 1, s2 * TOP_K), lambda j, z: (0, 0, 0),
                               memory_space=pltpu.SMEM),
                  pl.BlockSpec((1, s2, d), lambda j, z: (0, 0, 0))],
        out_specs=pl.BlockSpec(memory_space=pl.ANY),
        scratch_shapes=[pltpu.VMEM((EXPERT_TILE * sub, lanes), f32),
                        pltpu.VMEM((tt * sub, lanes), f32), pltpu.VMEM((s2 * sub, lanes), f32),
                        pltpu.SemaphoreType.DMA(()), pltpu.SemaphoreType.DMA(())])
    return pl.pallas_call(
        _dispatch_kernel,
        name="moe_dispatch",
        out_shape=jax.ShapeDtypeStruct((n_rows * sub, lanes), f32),
        grid_spec=grid_spec,
        compiler_params=_cparams(("arbitrary",)),
    )(zrow, dest_big.reshape(nt, 1, tt * TOP_K), h_big, dest_small.reshape(1, 1, s2 * TOP_K),
      h_small)


def _expert_kernel(be_ref, nu_ref, x_ref, wgu_ref, bgu_ref, wd_ref, bd_ref, y_ref,
                   wgu_bf, wd_bf):
    i = pl.program_id(0)
    prev = be_ref[jnp.maximum(i - 1, 0)]
    fresh = (i == 0) | (be_ref[i] != prev)

    @pl.when(fresh & (i < nu_ref[0]))
    def _():
        wgu_bf[...] = wgu_ref[0].astype(bf16)
        wd_bf[...] = wd_ref[0].astype(bf16)

    @pl.when(i < nu_ref[0])
    def _():
        x = _from_row_tiles(x_ref, x_ref.shape[0] // ROW_TILE[0]).astype(bf16)
        gu = jnp.dot(x, wgu_bf[...], preferred_element_type=f32) + bgu_ref[0]
        gate = jnp.minimum(gu[:, :D_FF], SWIGLU_LIMIT)
        up = jnp.clip(gu[:, D_FF:], -SWIGLU_LIMIT, SWIGLU_LIMIT)
        act = (up + 1.0) * (gate * _sigmoid(SWIGLU_ALPHA * gate))
        _to_row_tiles(y_ref, jnp.dot(act.astype(bf16), wd_bf[...], preferred_element_type=f32)
                      + bd_ref[0])

    @pl.when(i >= nu_ref[0])
    def _():
        y_ref[...] = jnp.zeros(y_ref.shape, f32)


def _experts(xs, block_e, n_used, w_gate_up, b_gate_up, w_down, b_down):
    sub, lanes = ROW_TILE
    n_rows = xs.shape[0] // sub
    d = w_down.shape[-1]
    tm = EXPERT_TILE
    nb = n_rows // tm
    last = lambda i, be, nu: jnp.minimum(i, nu[0] - 1)
    grid_spec = pltpu.PrefetchScalarGridSpec(
        num_scalar_prefetch=2,
        grid=(nb,),
        in_specs=[pl.BlockSpec((tm * sub, lanes), lambda i, be, nu: (last(i, be, nu), 0)),
                  pl.BlockSpec((1, d, 2 * D_FF), lambda i, be, nu: (be[last(i, be, nu)], 0, 0)),
                  pl.BlockSpec((1, 1, 2 * D_FF), lambda i, be, nu: (be[last(i, be, nu)], 0, 0)),
                  pl.BlockSpec((1, D_FF, d), lambda i, be, nu: (be[last(i, be, nu)], 0, 0)),
                  pl.BlockSpec((1, 1, d), lambda i, be, nu: (be[last(i, be, nu)], 0, 0))],
        out_specs=pl.BlockSpec((tm * sub, lanes), lambda i, be, nu: (i, 0)),
        scratch_shapes=[pltpu.VMEM((d, 2 * D_FF), bf16), pltpu.VMEM((D_FF, d), bf16)])
    return pl.pallas_call(
        _expert_kernel,
        name="moe_experts",
        out_shape=jax.ShapeDtypeStruct((n_rows * sub, lanes), f32),
        grid_spec=grid_spec,
        compiler_params=_cparams(("arbitrary",)),
    )(block_e, n_used, xs, w_gate_up, b_gate_up.reshape(N_EXPERTS, 1, -1), w_down,
      b_down.reshape(N_EXPERTS, 1, -1))


def _combine_kernel(dest_ref, x1_ref, gate_ref, mod_ref, nf_ref, yb_ref, o_ref, buf, sems, *,
                    final, parts):
    tt = x1_ref.shape[1]
    pt = tt // parts

    def row_copy(t, k, d, q):
        return pltpu.make_async_copy(_row_tile(yb_ref, d), _row_tile(buf, k * tt + t), sems.at[q])

    for q in range(parts):
        def issue(t, carry, q=q):
            for k in range(TOP_K):
                row_copy(t, k, dest_ref[0, 0, TOP_K * t + k], q).start(priority=k % 2)
            return carry
        lax.fori_loop(q * pt, (q + 1) * pt, issue, 0)

    for q in range(parts):
        def drain(t, carry, q=q):
            for k in range(TOP_K):
                row_copy(0, 0, 0, q).wait()
            return carry
        lax.fori_loop(0, pt, drain, 0)
        rows = pl.ds(q * pt, pt)
        gates = gate_ref[0, rows, :]
        y = gates[:, 0:1] * _from_row_tiles(buf, pt, q * pt)
        for k in range(1, TOP_K):
            y = y + gates[:, k:k + 1] * _from_row_tiles(buf, pt, k * tt + q * pt)
        x = x1_ref[0, rows, :] + mod_ref[0, 5:6, :] * y
        o_ref[0, rows, :] = _rms(x, nf_ref[...]) if final else x


def _combine(x1, gates, mod, dest, yb, norm_final, final):
    b, s, d = x1.shape
    tt = _route_tile(s)
    nt = s // tt
    parts = min(COMBINE_PARTS, tt // 8)
    dest_t = dest.reshape(b * nt, 1, tt * TOP_K)
    return pl.pallas_call(
        functools.partial(_combine_kernel, final=final, parts=parts),
        name="moe_combine",
        out_shape=jax.ShapeDtypeStruct((b, s, d), f32),
        grid=(b, nt),
        in_specs=[pl.BlockSpec((1, 1, tt * TOP_K), lambda i, j: (i * nt + j, 0, 0),
                               memory_space=pltpu.SMEM),
                  pl.BlockSpec((1, tt, d), lambda i, j: (i, j, 0)),
                  pl.BlockSpec((1, tt, ROUTE_W), lambda i, j: (i, j, 0)),
                  pl.BlockSpec((1, 6, d), lambda i, j: (i, 0, 0)),
                  pl.BlockSpec((1, d), lambda i, j: (0, 0)),
                  pl.BlockSpec(memory_space=pl.ANY)],
        out_specs=pl.BlockSpec((1, tt, d), lambda i, j: (i, j, 0)),
        scratch_shapes=[pltpu.VMEM((TOP_K * tt * ROW_TILE[0], ROW_TILE[1]), f32),
                        pltpu.SemaphoreType.DMA((parts,))],
        compiler_params=_cparams(("arbitrary", "arbitrary")),
    )(dest_t, x1, gates, mod, norm_final.reshape(1, d), yb)


def _rot_half_cols(w):
    half = w.shape[-1] // 2
    return jnp.concatenate([-w[..., half:], w[..., :half]], axis=-1)


def _layer_weights(w_in, q_norm, w_uq, kv_norm, w_ukv, mla_out_norm, shift_mu, decay_w0, decay_w2,
                   iclr_a0, iclr_a2, gate_g2, k_k, k_a, r_k, lnx_g, lnx_b, w_out, norm_ffn,
                   w_router, b_router):
    d = w_in.shape[0]
    kr_cols = w_in[:, Q_LORA + KV_LORA:MLA_COLS]
    w_in_ext = jnp.concatenate([w_in[:, :MLA_COLS], _rot_half_cols(kr_cols), w_in[:, MLA_COLS:]],
                               axis=1)
    uq = w_uq.reshape(Q_LORA, MLA_HEADS, QK_NOPE + QK_ROPE)
    zq = jnp.zeros((Q_LORA, MLA_HEADS, QK_PAD - QK_NOPE - QK_ROPE), f32)
    w_qa = jnp.concatenate([uq, zq], axis=2).reshape(Q_LORA, MLA_HEADS * QK_PAD)
    w_qb = jnp.concatenate([jnp.zeros_like(uq[..., :QK_NOPE]), _rot_half_cols(uq[..., QK_NOPE:]), zq],
                           axis=2).reshape(Q_LORA, MLA_HEADS * QK_PAD)
    ukv = w_ukv.reshape(KV_LORA, MLA_HEADS, QK_NOPE + V_HEAD)
    zl = jnp.zeros((DECAY_LORA, RWKV_WIDTH), f32)
    hid = jnp.arange(RWKV_WIDTH) // RWKV_HEAD
    return dict(
        w_in_bf=w_in_ext.astype(bf16),
        q_norm=q_norm, kv_norm=kv_norm,
        w_qa=w_qa.astype(bf16), w_qb=w_qb.astype(bf16),
        w_kn=ukv[..., :QK_NOPE].reshape(KV_LORA, -1).astype(bf16),
        w_v=ukv[..., QK_NOPE:].reshape(KV_LORA, -1).astype(bf16),
        mla_out_norm=mla_out_norm.reshape(1, -1),
        shift_mu=shift_mu.reshape(1, -1), decay_w0=decay_w0.reshape(1, -1),
        decay_w2p=jnp.concatenate([decay_w2, zl], axis=0),
        iclr_a0=iclr_a0.reshape(1, -1),
        iclr_a2p=jnp.concatenate([zl, iclr_a2], axis=0),
        gate_g2=gate_g2, k_k=k_k.reshape(1, -1), k_a=k_a.reshape(1, -1), r_k=r_k.reshape(1, -1),
        lnx_g=lnx_g.reshape(1, -1), lnx_b=lnx_b.reshape(1, -1),
        seg=(hid[:, None] == hid[None, :]).astype(bf16),
        w_out_bf=w_out.astype(bf16), norm_ffn=norm_ffn.reshape(1, d),
        w_router=w_router, b_router=b_router.reshape(1, -1),
    )


def _rope_tables(pos):
    inv = ROPE_THETA ** (-jnp.arange(0, QK_ROPE, 2, dtype=f32) / QK_ROPE)
    ang = pos.astype(f32)[:, None] * inv[None, :]
    cos, sin = jnp.cos(ang), jnp.sin(ang)
    cos2 = jnp.concatenate([cos, cos], axis=1)
    sin2 = jnp.concatenate([sin, sin], axis=1)
    scale = (QK_NOPE + QK_ROPE) ** -0.5 * LOG2_E
    n = pos.shape[0]
    zpad = jnp.zeros((n, QK_PAD - QK_NOPE - QK_ROPE), f32)
    ct = jnp.concatenate([jnp.full((n, QK_NOPE), scale, f32), cos2 * scale, zpad], axis=1)
    st = jnp.concatenate([jnp.zeros((n, QK_NOPE), f32), sin2 * scale, zpad], axis=1)
    kt = jnp.concatenate([cos2, sin2], axis=1)
    return ct, st, kt


def _pair_states(state):
    b = state.shape[0]
    s = jnp.swapaxes(state, -1, -2).reshape(b, N_PAIRS, 2, RWKV_HEAD, RWKV_HEAD)
    z = jnp.zeros_like(s[:, :, 0])
    top = jnp.concatenate([s[:, :, 0], z], axis=-1)
    bot = jnp.concatenate([z, s[:, :, 1]], axis=-1)
    return jnp.concatenate([top, bot], axis=-2)


def _unpair_states(sp):
    b = sp.shape[0]
    h0 = sp[:, :, :RWKV_HEAD, :RWKV_HEAD]
    h1 = sp[:, :, RWKV_HEAD:, RWKV_HEAD:]
    heads = jnp.stack([h0, h1], axis=2).reshape(b, RWKV_HEADS, RWKV_HEAD, RWKV_HEAD)
    return jnp.swapaxes(heads, -1, -2)


def _mix_path(x, mod, pos, cache_lat, cache_rope, wkv_prev, shift_prev, wts, counts0):
    b, s, d = x.shape
    q, new_lat, new_rope, rope128, kv, seqs, new_shift = _front(
        x, mod, shift_prev, wts, _rope_tables(pos), with_kv=cache_lat is None)
    if cache_lat is None:
        o_mla = _flash_attention(q, *kv)
    else:
        past = cache_lat.shape[1]
        n_keys = past + s
        sk = -(-n_keys // 128) * 128
        lat_all = jnp.concatenate([cache_lat, new_lat, jnp.zeros((b, sk - n_keys, KV_LORA), f32)],
                                  axis=1)
        cache_rope128 = jnp.concatenate([cache_rope, jnp.zeros_like(cache_rope)], axis=-1)
        rope_all = jnp.concatenate([cache_rope128, rope128,
                                    jnp.zeros((b, sk - n_keys, 2 * QK_ROPE), f32)], axis=1)
        k, v = _kv_proj(lat_all, rope_all, wts["w_kn"], wts["w_v"])
        o_mla = _cached_attention(q, k, v, past, n_keys)

    scan_in, gate_rw, bonus = seqs[:6], seqs[6], seqs[7]
    scan_rows = _scan_rows(s)
    s_pad = -(-s // scan_rows) * scan_rows
    if s_pad != s:
        scan_in = tuple(jnp.pad(t, ((0, 0), (0, s_pad - s), (0, 0))) for t in scan_in)
    y_rw, s_fin = _rwkv_scan(scan_in, _pair_states(wkv_prev))
    new_wkv = _unpair_states(s_fin)

    x1, h2, gates, idx, rank, counts = _outproj(o_mla, y_rw[:, :s], bonus, gate_rw, x, mod, wts,
                                                counts0)
    return dict(x1=x1, h2=h2, gates=gates, idx=idx, rank=rank, counts=counts,
                new_lat=new_lat, new_rope=new_rope, new_wkv=new_wkv, new_shift=new_shift)


def kernel(x_prompt, x_sample, cache_kv_latent, cache_k_rope, state_wkv, state_shift, c_prompt, c_sample, w_ada, b_ada, norm_mix, w_in, q_norm, w_uq, kv_norm, w_ukv, mla_out_norm, shift_mu, decay_w0, decay_w2, iclr_a0, iclr_a2, gate_g2, k_k, k_a, r_k, lnx_g, lnx_b, w_out, norm_ffn, w_router, b_router, w_gate_up, b_gate_up, w_down, b_down, norm_final):
    depth = w_ada.shape[0]
    bp, seq_p, d = x_prompt.shape
    bs, seq_s, _ = x_sample.shape
    past = cache_kv_latent.shape[2]
    pos_p = jnp.arange(seq_p, dtype=i32)
    pos_s = past + jnp.arange(seq_s, dtype=i32)
    zero_wkv = jnp.zeros((bp, RWKV_HEADS, RWKV_HEAD, RWKV_HEAD), f32)
    zero_shift = jnp.zeros((bp, RWKV_COLS), f32)
    n_c = bp + bs
    c_rows = -(-n_c // 8) * 8
    c_all = jnp.concatenate([c_prompt, c_sample, jnp.zeros((c_rows - n_c, d), f32)], axis=0)

    hp, hs = x_prompt, x_sample
    outs_p = [[], [], [], []]
    outs_s = [[], [], [], []]
    tm = EXPERT_TILE
    n_pairs = (bp * seq_p + bs * seq_s) * TOP_K
    n_blocks = (n_pairs + N_EXPERTS * (tm - 1) + tm - 1) // tm
    for l in range(depth):
        wts = _layer_weights(w_in[l], q_norm[l], w_uq[l], kv_norm[l], w_ukv[l], mla_out_norm[l],
                             shift_mu[l], decay_w0[l], decay_w2[l], iclr_a0[l], iclr_a2[l],
                             gate_g2[l], k_k[l], k_a[l], r_k[l], lnx_g[l], lnx_b[l], w_out[l],
                             norm_ffn[l], w_router[l], b_router[l])
        wts["norm_mix"] = norm_mix[l]
        mod = _modulation(c_all, w_ada[l], b_ada[l]).reshape(c_rows, 6, d)
        mod_p, mod_s = mod[:bp], mod[bp:n_c]

        rp = _mix_path(hp, mod_p, pos_p, None, None, zero_wkv, zero_shift, wts,
                       jnp.zeros((1, N_EXPERTS), f32))
        rs = _mix_path(hs, mod_s, pos_s, cache_kv_latent[l], cache_k_rope[l], state_wkv[l],
                       state_shift[l], wts, rp["counts"])

        counts = rs["counts"][0].astype(i32)
        padded = (counts + tm - 1) // tm * tm
        pad_end = jnp.cumsum(padded)
        pad_start = pad_end - padded
        experts = jnp.arange(N_EXPERTS, dtype=i32)

        def slots(r):
            idx = r["idx"][..., :TOP_K]
            start = jnp.sum(jnp.where(idx[..., None] == experts, pad_start, 0), axis=-1)
            return start + r["rank"][..., :TOP_K]

        dest_p, dest_s = slots(rp), slots(rs)
        tile_row = jnp.arange(n_blocks, dtype=i32) * tm
        block_e = jnp.minimum(jnp.sum((pad_end[None, :] <= tile_row[:, None]).astype(i32), axis=1),
                              N_EXPERTS - 1)
        n_used = (pad_end[-1:] // tm).astype(i32)
        zrow = jnp.concatenate([jnp.where(padded > 0, pad_end - tm, -1), n_used]).astype(i32)

        xs = _dispatch(rp["h2"], dest_p, rs["h2"], dest_s, zrow, n_blocks * tm)
        yb = _experts(xs, block_e, n_used, w_gate_up[l], b_gate_up[l], w_down[l], b_down[l])

        last = l == depth - 1
        hp = _combine(rp["x1"], rp["gates"], mod_p, dest_p, yb, norm_final, last)
        hs = _combine(rs["x1"], rs["gates"], mod_s, dest_s, yb, norm_final, last)
        for acc, r in ((outs_p, rp), (outs_s, rs)):
            acc[0].append(r["new_lat"])
            acc[1].append(r["new_rope"])
            acc[2].append(r["new_wkv"])
            acc[3].append(r["new_shift"])

    return (hp, hs, jnp.stack(outs_p[0]), jnp.stack(outs_p[1]), jnp.stack(outs_p[2]),
            jnp.stack(outs_p[3]), jnp.stack(outs_s[0]), jnp.stack(outs_s[1]),
            jnp.stack(outs_s[2]), jnp.stack(outs_s[3]))
```
